```python
import math
import jax, jax.numpy as jnp
from jax import lax
import numpy as np

D_MODEL = 1024
BATCH = 2
SEQ = 8192
DEPTH = 4
DEC_BATCH = 32
DEC_SEQ = 1
PAST_LEN = 8192
PAGE_SIZE = 128

N_MIXERS = 3
CONV_W = 3
S5_GROUP = 16
S5_GROUPS = D_MODEL // S5_GROUP
S5_STATE = 64
FOX_HEADS = 16
FOX_HEAD_DIM = D_MODEL // FOX_HEADS
Q_BLOCK = 128
D_FF = 2816
RMS_EPS = 1e-6
FORGET_BIAS = 3.0

kernel_name = 'hybrid_sconv_s5_fox_decoder_step'


def rms_norm(x, g):
    x32 = x.astype(jnp.float32)
    y = x32 * lax.rsqrt(jnp.mean(x32 * x32, axis=-1, keepdims=True) + RMS_EPS)
    return (y * g.astype(jnp.float32)).astype(x.dtype)


def causal_dwconv(x, buf, w):
    n = x.shape[1]
    xp = jnp.concatenate([buf.astype(x.dtype), x], axis=1)
    y = xp[:, 0:n] * w[0]
    for j in range(1, CONV_W):
        y = y + xp[:, j:j + n] * w[j]
    return y, xp[:, xp.shape[1] - (CONV_W - 1):]


def short_conv_mixer(x, buf, w_in, conv_w, w_out):
    b, c, h = jnp.split(x @ w_in, 3, axis=-1)
    z, new_buf = causal_dwconv(c * h, buf, conv_w)
    return (b * z) @ w_out, (new_buf,)


def _cmul(ar, ai, br, bi):
    return ar * br - ai * bi, ar * bi + ai * br


def _ssm_combine(e1, e2):
    a1r, a1i, b1r, b1i = e1
    a2r, a2i, b2r, b2i = e2
    ar, ai = _cmul(a2r, a2i, a1r, a1i)
    br, bi = _cmul(a2r, a2i, b1r, b1i)
    return ar, ai, br + b2r, bi + b2i


def s5_mixer(x, h0_re, h0_im, w_in, lam_re, lam_im, log_dt, b_re, b_im, c_re, c_im, d_skip, w_a, w_b):
    f32 = jnp.float32
    bsz, n, _ = x.shape
    u = (x @ w_in).astype(f32)
    ug = u.reshape(bsz, n, S5_GROUPS, S5_GROUP)
    lr = lam_re.astype(f32)
    li = lam_im.astype(f32)
    dt = jnp.exp(log_dt.astype(f32))[:, None]
    mag = jnp.exp(lr * dt)
    abar_r = mag * jnp.cos(li * dt)
    abar_i = mag * jnp.sin(li * dt)
    den = lr * lr + li * li
    nr = abar_r - 1.0
    kr = (nr * lr + abar_i * li) / den
    ki = (abar_i * lr - nr * li) / den
    bb_r, bb_i = _cmul(kr[..., None], ki[..., None], b_re.astype(f32), b_im.astype(f32))
    bu_r = jnp.einsum('blgn,gpn->blgp', ug, bb_r)
    bu_i = jnp.einsum('blgn,gpn->blgp', ug, bb_i)
    h0r, h0i = _cmul(abar_r, abar_i, h0_re.astype(f32), h0_im.astype(f32))
    bu_r = bu_r.at[:, 0].add(h0r)
    bu_i = bu_i.at[:, 0].add(h0i)
    a_r = jnp.broadcast_to(abar_r, bu_r.shape)
    a_i = jnp.broadcast_to(abar_i, bu_i.shape)
    _, _, hr, hi = lax.associative_scan(_ssm_combine, (a_r, a_i, bu_r, bu_i), axis=1)
    y = (jnp.einsum('blgp,gnp->blgn', hr, c_re.astype(f32))
         - jnp.einsum('blgp,gnp->blgn', hi, c_im.astype(f32)))
    y = y.reshape(bsz, n, D_MODEL) + d_skip.astype(f32) * u
    z = jax.nn.gelu(y).astype(x.dtype)
    out = (z @ w_a) * jax.nn.sigmoid(z @ w_b)
    return out, (hr[:, -1], hi[:, -1])


def forgetting_attention(q, k, v, cum_q, cum_k, q_start):
    bsz, n_q, n_h, d_h = q.shape
    n_k = k.shape[1]
    qb = min(Q_BLOCK, n_q)
    n_blk = -(-n_q // qb)
    pad = n_blk * qb - n_q
    q_p = jnp.pad(q, ((0, 0), (0, pad), (0, 0), (0, 0)))
    c_p = jnp.pad(cum_q, ((0, 0), (0, pad), (0, 0)), mode='edge')
    q_blocks = q_p.reshape(bsz, n_blk, qb, n_h, d_h).transpose(1, 0, 2, 3, 4)
    c_blocks = c_p.reshape(bsz, n_blk, qb, n_h).transpose(1, 0, 3, 2)
    pos_blocks = (q_start + jnp.arange(n_blk * qb)).reshape(n_blk, qb)
    k_pos = jnp.arange(n_k)
    c_k = cum_k.transpose(0, 2, 1)[:, :, None, :]
    scale = d_h ** -0.5

    def one_block(blk):
        q_b, c_b, pos_b = blk
        s = jnp.einsum('bqhd,bkhd->bhqk', q_b, k).astype(jnp.float32) * scale
        s = s + c_b[..., None] - c_k
        s = jnp.where(k_pos[None, :] <= pos_b[:, None], s, -jnp.inf)
        p = jax.nn.softmax(s, axis=-1)
        return jnp.einsum('bhqk,bkhd->bqhd', p.astype(v.dtype), v)

    o = lax.map(one_block, (q_blocks, c_blocks, pos_blocks))
    return o.transpose(1, 0, 2, 3, 4).reshape(bsz, n_blk * qb, n_h, d_h)[:, :n_q]


def fox_mixer(x, past_k, past_v, past_logf, w_qkvf, b_f, w_o):
    bsz, n, _ = x.shape
    proj = x @ w_qkvf
    shp = (bsz, n, FOX_HEADS, FOX_HEAD_DIM)
    q = proj[..., :D_MODEL].reshape(shp)
    k = proj[..., D_MODEL:2 * D_MODEL].reshape(shp)
    v = proj[..., 2 * D_MODEL:3 * D_MODEL].reshape(shp)
    logf = jax.nn.log_sigmoid((proj[..., 3 * D_MODEL:] + b_f).astype(jnp.float32))
    n_past = past_k.shape[1]
    k_all = jnp.concatenate([past_k.astype(k.dtype), k], axis=1)
    v_all = jnp.concatenate([past_v.astype(v.dtype), v], axis=1)
    cum = jnp.cumsum(jnp.concatenate([past_logf.astype(jnp.float32), logf], axis=1), axis=1)
    o = forgetting_attention(q, k_all, v_all, cum[:, n_past:], cum, n_past)
    return o.reshape(bsz, n, D_MODEL) @ w_o, (k, v, logf)


def conv_ffn(x, buf, w_gate, w_up, conv_w, w_down):
    g, new_buf = causal_dwconv(x @ w_gate, buf, conv_w)
    return (jax.nn.silu(g) * (x @ w_up)) @ w_down, new_buf


def run_trunk(x, mixer_states, ffn_bufs, mixer_params, norm_gains, ffn_params):
    mix_pre, mix_post, ffn_pre, ffn_post = norm_gains
    w_gate, w_up, conv_w, w_down = ffn_params
    new_states = []
    new_ffn = []
    for i in range(DEPTH):
        h = rms_norm(x, mix_pre[i])
        kind = i % N_MIXERS
        if kind == 0:
            y, st = short_conv_mixer(h, *mixer_states[i], *mixer_params[i])
        elif kind == 1:
            y, st = s5_mixer(h, *mixer_states[i], *mixer_params[i])
        else:
            y, st = fox_mixer(h, *mixer_states[i], *mixer_params[i])
        x = x + rms_norm(y.astype(x.dtype), mix_post[i])
        new_states.append(st)
        h = rms_norm(x, ffn_pre[i])
        y, fb = conv_ffn(h, ffn_bufs[i], w_gate[i], w_up[i], conv_w[i], w_down[i])
        x = x + rms_norm(y, ffn_post[i])
        new_ffn.append(fb)
    return x, new_states, jnp.stack(new_ffn)


def setup_inputs(seed: int = 0) -> dict:
    key = jax.random.key(seed)
    ks = list(jax.random.split(key, 48))
    f32 = jnp.float32

    def nrm(idx, shape, scale=1.0):
        return scale * jax.random.normal(ks[idx], shape, f32)

    D, G, P, N = D_MODEL, S5_GROUPS, S5_STATE, S5_GROUP
    H, Dh = FOX_HEADS, FOX_HEAD_DIM
    n_pages = PAST_LEN // PAGE_SIZE
    n_used = DEC_BATCH * n_pages
    n_phys = n_used + n_used // 4
    page_table = jax.random.permutation(ks[0], n_phys)[:n_used].reshape(DEC_BATCH, n_pages).astype(jnp.int32)
    inp = {}
    inp['x_prompt'] = nrm(1, (BATCH, SEQ, D))
    inp['x_sample'] = nrm(2, (DEC_BATCH, DEC_SEQ, D))
    inp['state_sconv_l0'] = nrm(3, (DEC_BATCH, CONV_W - 1, D))
    inp['state_ssm_re_l1'] = nrm(4, (DEC_BATCH, G, P), 0.1)
    inp['state_ssm_im_l1'] = nrm(5, (DEC_BATCH, G, P), 0.1)
    inp['cache_k_l2'] = nrm(6, (n_phys, PAGE_SIZE, H, Dh))
    inp['cache_v_l2'] = nrm(7, (n_phys, PAGE_SIZE, H, Dh))
    inp['cache_logf_l2'] = jax.nn.log_sigmoid(FORGET_BIAS + nrm(8, (n_phys, PAGE_SIZE, H)))
    inp['state_sconv_l3'] = nrm(9, (DEC_BATCH, CONV_W - 1, D))
    inp['state_ffn_conv'] = nrm(10, (DEPTH, DEC_BATCH, CONV_W - 1, D_FF))
    inp['page_table'] = page_table
    inp['sc_w_in_l0'] = nrm(11, (D, 3 * D), D ** -0.5)
    inp['sc_conv_w_l0'] = nrm(12, (CONV_W, D), CONV_W ** -0.5)
    inp['sc_w_out_l0'] = nrm(13, (D, D), D ** -0.5)
    inp['s5_w_in_l1'] = nrm(14, (D, D), D ** -0.5)
    inp['s5_lambda_re_l1'] = -0.5 + nrm(15, (G, P), 0.01)
    inp['s5_lambda_im_l1'] = jnp.pi * jnp.arange(P, dtype=f32)[None, :] + nrm(16, (G, P), 0.01)
    inp['s5_log_dt_l1'] = jax.random.uniform(ks[17], (G,), f32, math.log(1e-3), math.log(1e-1))
    inp['s5_b_re_l1'] = nrm(18, (G, P, N), (2 * N) ** -0.5)
    inp['s5_b_im_l1'] = nrm(19, (G, P, N), (2 * N) ** -0.5)
    inp['s5_c_re_l1'] = nrm(20, (G, N, P), (2 * P) ** -0.5)
    inp['s5_c_im_l1'] = nrm(21, (G, N, P), (2 * P) ** -0.5)
    inp['s5_d_l1'] = nrm(22, (D,))
    inp['s5_glu_wa_l1'] = nrm(23, (D, D), D ** -0.5)
    inp['s5_glu_wb_l1'] = nrm(24, (D, D), D ** -0.5)
    inp['fox_w_qkvf_l2'] = nrm(25, (D, 3 * D + H), D ** -0.5)
    inp['fox_b_f_l2'] = FORGET_BIAS + nrm(26, (H,), 0.1)
    inp['fox_w_o_l2'] = nrm(27, (D, D), D ** -0.5)
    inp['sc_w_in_l3'] = nrm(28, (D, 3 * D), D ** -0.5)
    inp['sc_conv_w_l3'] = nrm(29, (CONV_W, D), CONV_W ** -0.5)
    inp['sc_w_out_l3'] = nrm(30, (D, D), D ** -0.5)
    inp['norm_mix_pre'] = 1.0 + nrm(31, (DEPTH, D), 0.05)
    inp['norm_mix_post'] = 1.0 + nrm(32, (DEPTH, D), 0.05)
    inp['norm_ffn_pre'] = 1.0 + nrm(33, (DEPTH, D), 0.05)
    inp['norm_ffn_post'] = 1.0 + nrm(34, (DEPTH, D), 0.05)
    inp['ffn_w_gate'] = nrm(35, (DEPTH, D, D_FF), D ** -0.5)
    inp['ffn_w_up'] = nrm(36, (DEPTH, D, D_FF), D ** -0.5)
    inp['ffn_conv_w'] = nrm(37, (DEPTH, CONV_W, D_FF), CONV_W ** -0.5)
    inp['ffn_w_down'] = nrm(38, (DEPTH, D_FF, D), D_FF ** -0.5)
    return inp


def reference(x_prompt, x_sample, state_sconv_l0, state_ssm_re_l1, state_ssm_im_l1,
              cache_k_l2, cache_v_l2, cache_logf_l2, state_sconv_l3, state_ffn_conv, page_table,
              sc_w_in_l0, sc_conv_w_l0, sc_w_out_l0,
              s5_w_in_l1, s5_lambda_re_l1, s5_lambda_im_l1, s5_log_dt_l1, s5_b_re_l1, s5_b_im_l1,
              s5_c_re_l1, s5_c_im_l1, s5_d_l1, s5_glu_wa_l1, s5_glu_wb_l1,
              fox_w_qkvf_l2, fox_b_f_l2, fox_w_o_l2,
              sc_w_in_l3, sc_conv_w_l3, sc_w_out_l3,
              norm_mix_pre, norm_mix_post, norm_ffn_pre, norm_ffn_post,
              ffn_w_gate, ffn_w_up, ffn_conv_w, ffn_w_down):
    mixer_params = [
        (sc_w_in_l0, sc_conv_w_l0, sc_w_out_l0),
        (s5_w_in_l1, s5_lambda_re_l1, s5_lambda_im_l1, s5_log_dt_l1, s5_b_re_l1, s5_b_im_l1,
         s5_c_re_l1, s5_c_im_l1, s5_d_l1, s5_glu_wa_l1, s5_glu_wb_l1),
        (fox_w_qkvf_l2, fox_b_f_l2, fox_w_o_l2),
        (sc_w_in_l3, sc_conv_w_l3, sc_w_out_l3),
    ]
    norms = (norm_mix_pre, norm_mix_post, norm_ffn_pre, norm_ffn_post)
    ffn = (ffn_w_gate, ffn_w_up, ffn_conv_w, ffn_w_down)

    n_dec, n_pages = page_table.shape
    n_past = n_pages * PAGE_SIZE
    past_k = cache_k_l2[page_table].reshape(n_dec, n_past, FOX_HEADS, FOX_HEAD_DIM)
    past_v = cache_v_l2[page_table].reshape(n_dec, n_past, FOX_HEADS, FOX_HEAD_DIM)
    past_lf = cache_logf_l2[page_table].reshape(n_dec, n_past, FOX_HEADS)
    sample_states = [(state_sconv_l0,), (state_ssm_re_l1, state_ssm_im_l1),
                     (past_k, past_v, past_lf), (state_sconv_l3,)]
    y_sample, st_s, ffn_s = run_trunk(x_sample, sample_states, state_ffn_conv, mixer_params, norms, ffn)

    bp = x_prompt.shape[0]
    dt = x_prompt.dtype
    zero_conv = jnp.zeros((bp, CONV_W - 1, D_MODEL), dt)
    zero_ssm = jnp.zeros((bp, S5_GROUPS, S5_STATE), jnp.float32)
    empty_kv = jnp.zeros((bp, 0, FOX_HEADS, FOX_HEAD_DIM), dt)
    empty_lf = jnp.zeros((bp, 0, FOX_HEADS), jnp.float32)
    prompt_states = [(zero_conv,), (zero_ssm, zero_ssm), (empty_kv, empty_kv, empty_lf), (zero_conv,)]
    zero_ffn = jnp.zeros((DEPTH, bp, CONV_W - 1, D_FF), dt)
    y_prompt, st_p, ffn_p = run_trunk(x_prompt, prompt_states, zero_ffn, mixer_params, norms, ffn)

    return (y_prompt, y_sample,
            st_p[0][0], st_s[0][0],
            st_p[1][0], st_s[1][0], st_p[1][1], st_s[1][1],
            st_p[2][0], st_s[2][0], st_p[2][1], st_s[2][1], st_p[2][2], st_s[2][2],
            st_p[3][0], st_s[3][0],
            ffn_p, ffn_s)
```

```python
import functools
import math

import jax
import jax.numpy as jnp
from jax import lax
from jax.experimental import pallas as pl
from jax.experimental.pallas import tpu as pltpu

F32 = jnp.float32
BF16 = jnp.bfloat16

RMS_EPS = 1e-6
CONV_W = 3
S5_GROUP = 16
S5_STATE = 64
FOX_HEADS = 16
PAGE_SIZE = 128

LANES = 128
SUBLANES = 8
MXU_DIM = 256
VMEM_LIMIT_BYTES = 56 * 1024 * 1024

NEG_BIG = -1e30
SCAN_LANES = 512


def _cparams(*sem):
    return pltpu.CompilerParams(dimension_semantics=sem, vmem_limit_bytes=VMEM_LIMIT_BYTES)


def _const_spec(shape):
    nd = len(shape)
    return pl.BlockSpec(shape, lambda *_: (0,) * nd, pipeline_mode=pl.Buffered(1))


def _rms(x, g):
    ms = jnp.mean(x * x, axis=-1, keepdims=True)
    return x * lax.rsqrt(ms + RMS_EPS) * g


def _dot(a, b):
    return jnp.dot(a, b, preferred_element_type=F32)


def _shift_rows(cur, prev_tail, k):
    rolled = pltpu.roll(cur, k, axis=0)
    rid = lax.broadcasted_iota(jnp.int32, cur.shape, 0)
    out = rolled
    for r in range(k):
        src = prev_tail[SUBLANES - k + r:SUBLANES - k + r + 1, :]
        out = jnp.where(rid == r, src, out)
    return out


def _causal_conv3(cur, prev_tail, w):
    x1 = _shift_rows(cur, prev_tail, 1)
    x2 = _shift_rows(cur, prev_tail, 2)
    return x2 * w[0:1, :] + x1 * w[1:2, :] + cur * w[2:3, :]


def _sconv_prompt_kernel(x_ref, gpre_ref, gpost_ref, win_ref, cw_ref, wout_ref,
                         y_ref, st_ref, tail_ref):
    t = pl.program_id(1)
    nt = pl.num_programs(1)
    d = x_ref.shape[-1]
    tm = x_ref.shape[1]

    @pl.when(t == 0)
    def _():
        tail_ref[...] = jnp.zeros_like(tail_ref)

    x = x_ref[0]
    h = _rms(x, gpre_ref[...]).astype(BF16)
    proj = _dot(h, win_ref[...])
    b = proj[:, :d]
    ch = proj[:, d:2 * d] * proj[:, 2 * d:]
    z = _causal_conv3(ch, tail_ref[...], cw_ref[...])
    y = _dot((b * z).astype(BF16), wout_ref[...])
    y_ref[0] = x + _rms(y, gpost_ref[...])
    tail_ref[...] = ch[tm - SUBLANES:, :]

    @pl.when(t == nt - 1)
    def _():
        st_ref[0] = ch[tm - (CONV_W - 1):, :]


def _sconv_prompt(x, gpre, gpost, w_in, conv_w, w_out, tm):
    bsz, n, d = x.shape
    return pl.pallas_call(
        _sconv_prompt_kernel,
        grid=(bsz, n // tm),
        in_specs=[
            pl.BlockSpec((1, tm, d), lambda b, t: (b, t, 0)),
            _const_spec((1, d)), _const_spec((1, d)),
            _const_spec((d, 3 * d)), _const_spec((CONV_W, d)), _const_spec((d, d)),
        ],
        out_specs=[
            pl.BlockSpec((1, tm, d), lambda b, t: (b, t, 0)),
            pl.BlockSpec((1, CONV_W - 1, d), lambda b, t: (b, 0, 0)),
        ],
        out_shape=[jax.ShapeDtypeStruct((bsz, n, d), F32),
                   jax.ShapeDtypeStruct((bsz, CONV_W - 1, d), F32)],
        scratch_shapes=[pltpu.VMEM((SUBLANES, d), F32)],
        compiler_params=_cparams("arbitrary", "arbitrary"),
        name="sconv_prompt",
    )(x, gpre, gpost, w_in, conv_w, w_out)


def _ffn_prompt_kernel(x_ref, gpre_ref, gpost_ref, wg_ref, wu_ref, cw_ref, wd_ref,
                       y_ref, st_ref, tail_ref, *, fc):
    t = pl.program_id(1)
    nt = pl.num_programs(1)
    tm = x_ref.shape[1]
    f = wg_ref.shape[1]

    @pl.when(t == 0)
    def _():
        tail_ref[...] = jnp.zeros_like(tail_ref)

    x = x_ref[0]
    h = _rms(x, gpre_ref[...]).astype(BF16)
    y = None
    for c in range(f // fc):
        cs = slice(c * fc, (c + 1) * fc)
        g = _dot(h, wg_ref[:, cs])
        u = _dot(h, wu_ref[:, cs])
        gc = _causal_conv3(g, tail_ref[:, cs], cw_ref[:, cs])
        act = (gc * jax.nn.sigmoid(gc) * u).astype(BF16)
        part = _dot(act, wd_ref[cs, :])
        y = part if y is None else y + part
        tail_ref[:, cs] = g[tm - SUBLANES:, :]

        @pl.when(t == nt - 1)
        def _():
            st_ref[0, :, cs] = g[tm - (CONV_W - 1):, :]

    y_ref[0] = x + _rms(y, gpost_ref[...])


def _ffn_prompt(x, gpre, gpost, wg, wu, conv_w, wd, tm, fc):
    bsz, n, d = x.shape
    f = wg.shape[1]
    return pl.pallas_call(
        functools.partial(_ffn_prompt_kernel, fc=fc),
        grid=(bsz, n // tm),
        in_specs=[
            pl.BlockSpec((1, tm, d), lambda b, t: (b, t, 0)),
            _const_spec((1, d)), _const_spec((1, d)),
            _const_spec((d, f)), _const_spec((d, f)), _const_spec((CONV_W, f)), _const_spec((f, d)),
        ],
        out_specs=[
            pl.BlockSpec((1, tm, d), lambda b, t: (b, t, 0)),
            pl.BlockSpec((1, CONV_W - 1, f), lambda b, t: (b, 0, 0)),
        ],
        out_shape=[jax.ShapeDtypeStruct((bsz, n, d), F32),
                   jax.ShapeDtypeStruct((bsz, CONV_W - 1, f), F32)],
        scratch_shapes=[pltpu.VMEM((SUBLANES, f), F32)],
        compiler_params=_cparams("arbitrary", "arbitrary"),
        name="ffn_prompt",
    )(x, gpre, gpost, wg, wu, conv_w, wd)


def _s5_prep_kernel(lr_ref, li_ref, ldt_ref, bre_ref, bim_ref, pwr_ref, pwi_ref, bbr_ref, bbi_ref):
    lr = lr_ref[...]
    li = li_ref[...]
    dt = jnp.exp(ldt_ref[...])
    for k in range(SUBLANES):
        mag = jnp.exp(lr * dt * (k + 1.0))
        ang = li * dt * (k + 1.0)
        pwr_ref[k] = mag * jnp.cos(ang)
        pwi_ref[k] = mag * jnp.sin(ang)
    abar_r = pwr_ref[0]
    abar_i = pwi_ref[0]
    den = lr * lr + li * li
    nr = abar_r - 1.0
    kr = (nr * lr + abar_i * li) / den
    ki = (abar_i * lr - nr * li) / den
    bre = bre_ref[...]
    bim = bim_ref[...]
    bbr_ref[...] = kr[:, None, :] * bre - ki[:, None, :] * bim
    bbi_ref[...] = kr[:, None, :] * bim + ki[:, None, :] * bre


def _s5_prep(lam_re, lam_im, log_dt, b_re_t, b_im_t):
    g, p = lam_re.shape
    n = b_re_t.shape[1]
    return pl.pallas_call(
        _s5_prep_kernel,
        out_shape=[jax.ShapeDtypeStruct((SUBLANES, g, p), F32), jax.ShapeDtypeStruct((SUBLANES, g, p), F32),
                   jax.ShapeDtypeStruct((g, n, p), F32), jax.ShapeDtypeStruct((g, n, p), F32)],
        name="s5_prep",
    )(lam_re, lam_im, log_dt.reshape(g, 1), b_re_t, b_im_t)


def _s5_block_diag(w, sgroups):
    g, a, b = w.shape
    w4 = w.reshape(g // sgroups, sgroups, a, b)
    eye = jnp.eye(sgroups, dtype=w.dtype)
    return jnp.einsum('sgab,gh->sgahb', w4, eye).reshape(g // sgroups, sgroups * a, sgroups * b)


def _s5_tables(lam_re, lam_im, log_dt, b_re, b_im, c_re, c_im):
    g, p = lam_re.shape
    pwr, pwi, bbr, bbi = _s5_prep(lam_re, lam_im, log_dt,
                                  jnp.swapaxes(b_re, 1, 2), jnp.swapaxes(b_im, 1, 2))
    sg = MXU_DIM // S5_GROUP
    w_bu = jnp.concatenate([_s5_block_diag(bbr, sg), _s5_block_diag(bbi, sg)], axis=-1).astype(BF16)
    w_cr = _s5_block_diag(jnp.swapaxes(c_re, 1, 2), sg).astype(BF16)
    w_ci = _s5_block_diag(jnp.swapaxes(c_im, 1, 2), sg).astype(BF16)
    pwr = pwr.reshape(SUBLANES, g * p)
    pwi = pwi.reshape(SUBLANES, g * p)
    rows = jnp.arange(SUBLANES)[:, None]
    steps = []
    for s in (1, 2, 4):
        steps.append(jnp.where(rows >= s, pwr[s - 1][None, :], 0.0))
        steps.append(jnp.where(rows >= s, pwi[s - 1][None, :], 0.0))
    coef = jnp.stack(steps + [pwr, pwi])
    return coef, w_bu, w_cr, w_ci


def _cmul_add(xr, xi, ar, ai, sr, si):
    return xr + ar * sr - ai * si, xi + ar * si + ai * sr


def _s5_prompt_kernel(x_ref, gpre_ref, gpost_ref, win_ref, coef_ref, wbu_ref, wcr_ref, wci_ref,
                      dsk_ref, wa_ref, wb_ref, h0r_ref, h0i_ref,
                      y_ref, sr_ref, si_ref, br_ref, bi_ref, cr_ref, ci_ref):
    t = pl.program_id(1)
    tm = x_ref.shape[1]
    nsg = wbu_ref.shape[0]
    cw = wbu_ref.shape[1]
    sw = wbu_ref.shape[2] // 2

    @pl.when(t == 0)
    def _():
        cr_ref[...] = jnp.broadcast_to(h0r_ref[0], cr_ref.shape)
        ci_ref[...] = jnp.broadcast_to(h0i_ref[0], ci_ref.shape)

    x = x_ref[0]
    h = _rms(x, gpre_ref[...]).astype(BF16)
    u = _dot(h, win_ref[...])
    ub = u.astype(BF16)
    for s in range(nsg):
        bu = _dot(ub[:, s * cw:(s + 1) * cw], wbu_ref[s])
        br_ref[:, s * sw:(s + 1) * sw] = bu[:, :sw]
        bi_ref[:, s * sw:(s + 1) * sw] = bu[:, sw:]

    ns = br_ref.shape[1]
    lc = min(SCAN_LANES, ns)

    def scan_block(rb, carry):
        r0 = pl.multiple_of(rb * SUBLANES, SUBLANES)
        for c in range(ns // lc):
            cs = slice(c * lc, (c + 1) * lc)
            xr = br_ref[pl.ds(r0, SUBLANES), cs]
            xi = bi_ref[pl.ds(r0, SUBLANES), cs]
            for j, sh in enumerate((1, 2, 4)):
                xr, xi = _cmul_add(xr, xi, coef_ref[2 * j, :, cs], coef_ref[2 * j + 1, :, cs],
                                   pltpu.roll(xr, sh, axis=0), pltpu.roll(xi, sh, axis=0))
            xr, xi = _cmul_add(xr, xi, coef_ref[6, :, cs], coef_ref[7, :, cs], cr_ref[:, cs], ci_ref[:, cs])
            br_ref[pl.ds(r0, SUBLANES), cs] = xr
            bi_ref[pl.ds(r0, SUBLANES), cs] = xi
            cr_ref[:, cs] = jnp.broadcast_to(xr[SUBLANES - 1:, :], (SUBLANES, lc))
            ci_ref[:, cs] = jnp.broadcast_to(xi[SUBLANES - 1:, :], (SUBLANES, lc))
        return carry

    lax.fori_loop(0, tm // SUBLANES, scan_block, 0)

    ys = []
    for s in range(nsg):
        hr = br_ref[:, s * sw:(s + 1) * sw].astype(BF16)
        hi = bi_ref[:, s * sw:(s + 1) * sw].astype(BF16)
        ys.append(_dot(hr, wcr_ref[s]) - _dot(hi, wci_ref[s]))
    yy = jnp.concatenate(ys, axis=1) + dsk_ref[...] * u
    z = jax.nn.gelu(yy).astype(BF16)
    out = _dot(z, wa_ref[...]) * jax.nn.sigmoid(_dot(z, wb_ref[...]))
    y_ref[0] = x + _rms(out, gpost_ref[...])
    sr_ref[0] = cr_ref[0:1, :]
    si_ref[0] = ci_ref[0:1, :]


def _s5_prompt(x, gpre, gpost, w_in, coef, w_bu, w_cr, w_ci, d_skip, wa, wb, h0r, h0i, tm):
    bsz, n, d = x.shape
    ns = coef.shape[-1]
    return pl.pallas_call(
        _s5_prompt_kernel,
        grid=(bsz, n // tm),
        in_specs=[
            pl.BlockSpec((1, tm, d), lambda b, t: (b, t, 0)),
            _const_spec((1, d)), _const_spec((1, d)), _const_spec((d, d)),
            _const_spec(coef.shape), _const_spec(w_bu.shape), _const_spec(w_cr.shape), _const_spec(w_ci.shape),
            _const_spec((1, d)), _const_spec((d, d)), _const_spec((d, d)),
            pl.BlockSpec((1, 1, ns), lambda b, t: (b, 0, 0)),
            pl.BlockSpec((1, 1, ns), lambda b, t: (b, 0, 0)),
        ],
        out_specs=[
            pl.BlockSpec((1, tm, d), lambda b, t: (b, t, 0)),
            pl.BlockSpec((1, 1, ns), lambda b, t: (b, 0, 0)),
            pl.BlockSpec((1, 1, ns), lambda b, t: (b, 0, 0)),
        ],
        out_shape=[jax.ShapeDtypeStruct((bsz, n, d), F32),
                   jax.ShapeDtypeStruct((bsz, 1, ns), F32), jax.ShapeDtypeStruct((bsz, 1, ns), F32)],
        scratch_shapes=[pltpu.VMEM((tm, ns), F32), pltpu.VMEM((tm, ns), F32),
                        pltpu.VMEM((SUBLANES, ns), F32), pltpu.VMEM((SUBLANES, ns), F32)],
        compiler_params=_cparams("arbitrary", "arbitrary"),
        name="s5_prompt",
    )(x, gpre, gpost, w_in, coef, w_bu, w_cr, w_ci, d_skip, wa, wb, h0r, h0i)


def _log_sigmoid(x):
    return -(jnp.maximum(-x, 0.0) + jnp.log1p(jnp.exp(-jnp.abs(x))))


def _split3(x):
    hi = x.astype(BF16).astype(F32)
    r = x - hi
    mid = r.astype(BF16).astype(F32)
    lo = r - mid
    return hi, mid, lo


def _fox_placement(nh):
    hd = LANES // 2
    pq = [[0.0] * (nh * LANES) for _ in range(LANES)]
    pk = [[0.0] * (nh * LANES) for _ in range(LANES)]
    one = 3 * nh
    for h in range(nh):
        base = h * LANES + hd
        for j in range(3):
            pq[j * nh + h][base + j] = 1.0
            pq[one][base + 3 + j] = 1.0
            pk[one][base + j] = 1.0
            pk[j * nh + h][base + 3 + j] = -1.0
    return jnp.array(pq, BF16), jnp.array(pk, BF16)


def _qkv_prompt_kernel(x_ref, gpre_ref, w_ref, bf_ref, pq_ref, pk_ref,
                       qa_ref, ka_ref, va_ref, k_ref, v_ref, lf_ref, carry_ref):
    t = pl.program_id(1)
    tm = x_ref.shape[1]
    d = x_ref.shape[-1]
    nh = qa_ref.shape[1]
    hd = d // nh

    @pl.when(t == 0)
    def _():
        carry_ref[...] = jnp.zeros_like(carry_ref)

    x = x_ref[0]
    h = _rms(x, gpre_ref[...]).astype(BF16)
    proj = _dot(h, w_ref[...])
    lane = lax.broadcasted_iota(jnp.int32, (tm, LANES), 1)
    logf = jnp.where(lane < nh, _log_sigmoid(proj[:, 3 * d:] + bf_ref[...]), 0.0)
    ri = lax.broadcasted_iota(jnp.int32, (tm, tm), 0)
    ci = lax.broadcasted_iota(jnp.int32, (tm, tm), 1)
    tri = (ci <= ri).astype(F32)
    cum = jnp.dot(tri, logf, preferred_element_type=F32, precision=lax.Precision.HIGHEST) + carry_ref[0:1, :]
    carry_ref[...] = jnp.broadcast_to(cum[tm - 1:, :], carry_ref.shape)
    hi, mid, lo = _split3(cum)
    src = hi + pltpu.roll(mid, nh, axis=1) + pltpu.roll(lo, 2 * nh, axis=1) + (lane == 3 * nh).astype(F32)
    src = src.astype(BF16)
    aug_q = _dot(src, pq_ref[...])
    aug_k = _dot(src, pk_ref[...])
    low = lane < hd
    for c in range(d // LANES):
        for o, (dst, aug) in enumerate(((qa_ref, aug_q), (ka_ref, aug_k), (va_ref, None))):
            blk = proj[:, o * d + c * LANES:o * d + (c + 1) * LANES]
            for half in range(LANES // hd):
                hh = c * (LANES // hd) + half
                v = blk if half == 0 else pltpu.roll(blk, LANES - half * hd, axis=1)
                fill = 0.0 if aug is None else aug[:, hh * LANES:(hh + 1) * LANES]
                dst[0, hh] = jnp.where(low, v, fill).astype(BF16)
    k_ref[0] = proj[:, d:2 * d]
    v_ref[0] = proj[:, 2 * d:3 * d]
    lf_ref[0] = logf[:, :nh]


def _qkv_prompt(x, gpre, w_all, bf_pad, pq, pk, nh, tm):
    bsz, n, d = x.shape
    head_spec = pl.BlockSpec((1, nh, tm, LANES), lambda b, t: (b, 0, t, 0))
    row_spec = pl.BlockSpec((1, tm, d), lambda b, t: (b, t, 0))
    head_shape = jax.ShapeDtypeStruct((bsz, nh, n, LANES), BF16)
    return pl.pallas_call(
        _qkv_prompt_kernel,
        grid=(bsz, n // tm),
        in_specs=[row_spec, _const_spec((1, d)), _const_spec(w_all.shape), _const_spec((1, LANES)),
                  _const_spec(pq.shape), _const_spec(pk.shape)],
        out_specs=[head_spec, head_spec, head_spec, row_spec, row_spec,
                   pl.BlockSpec((1, tm, nh), lambda b, t: (b, t, 0))],
        out_shape=[head_shape, head_shape, head_shape,
                   jax.ShapeDtypeStruct((bsz, n, d), F32), jax.ShapeDtypeStruct((bsz, n, d), F32),
                   jax.ShapeDtypeStruct((bsz, n, nh), F32)],
        scratch_shapes=[pltpu.VMEM((SUBLANES, LANES), F32)],
        compiler_params=_cparams("arbitrary", "arbitrary"),
        name="qkv_prompt",
    )(x, gpre, w_all, bf_pad, pq, pk)


def _attn_prompt_kernel(qi_ref, ki_ref, q_ref, k_ref, v_ref, o_ref, m_ref, l_ref, acc_ref, *, hd):
    p = pl.program_id(2)
    qi = qi_ref[p]
    ki = ki_ref[p]
    tq = q_ref.shape[2]
    tk = k_ref.shape[2]
    nrep = tk // LANES

    @pl.when(ki == 0)
    def _():
        m_ref[...] = jnp.full_like(m_ref, NEG_BIG)
        l_ref[...] = jnp.zeros_like(l_ref)
        acc_ref[...] = jnp.zeros_like(acc_ref)

    def step(masked):
        for j in range(q_ref.shape[1]):
            s = lax.dot_general(q_ref[0, j], k_ref[0, j], (((1,), (1,)), ((), ())),
                                preferred_element_type=F32)
            if masked:
                row = lax.broadcasted_iota(jnp.int32, (tq, tk), 0)
                col = lax.broadcasted_iota(jnp.int32, (tq, tk), 1)
                s = jnp.where(col <= row, s, NEG_BIG)
            m_prev = m_ref[j]
            m_new = jnp.maximum(m_prev, jnp.max(s, axis=1, keepdims=True))
            pr = jnp.exp(s - jnp.concatenate([m_new] * nrep, axis=1))
            alpha = jnp.exp(m_prev - m_new)
            l_ref[j] = alpha * l_ref[j] + jnp.sum(pr, axis=1, keepdims=True)
            acc_ref[j] = alpha * acc_ref[j] + _dot(pr.astype(BF16), v_ref[0, j])
            m_ref[j] = m_new

    @pl.when(ki < qi)
    def _():
        step(False)

    @pl.when(ki == qi)
    def _():
        step(True)
        lane = lax.broadcasted_iota(jnp.int32, (tq, LANES), 1)
        out = acc_ref[0] / l_ref[0]
        for j in range(1, q_ref.shape[1]):
            oj = pltpu.roll(acc_ref[j] / l_ref[j], j * hd, axis=1)
            out = jnp.where(lane < j * hd, out, oj)
        o_ref[0] = out


def _attn_prompt(qa, ka, va, hd, tq):
    bsz, nh, n, _ = qa.shape
    hpb = LANES // hd
    nq = n // tq
    pairs = [(q, k) for q in range(nq) for k in range(q + 1)]
    qi_tab = jnp.array([q for q, _ in pairs], jnp.int32)
    ki_tab = jnp.array([k for _, k in pairs], jnp.int32)
    grid_spec = pltpu.PrefetchScalarGridSpec(
        num_scalar_prefetch=2,
        grid=(bsz, nh // hpb, len(pairs)),
        in_specs=[
            pl.BlockSpec((1, hpb, tq, LANES), lambda b, h, p, qt, kt: (b, h, qt[p], 0)),
            pl.BlockSpec((1, hpb, tq, LANES), lambda b, h, p, qt, kt: (b, h, kt[p], 0)),
            pl.BlockSpec((1, hpb, tq, LANES), lambda b, h, p, qt, kt: (b, h, kt[p], 0)),
        ],
        out_specs=pl.BlockSpec((1, tq, LANES), lambda b, h, p, qt, kt: (b, qt[p], h)),
        scratch_shapes=[pltpu.VMEM((hpb, tq, LANES), F32), pltpu.VMEM((hpb, tq, LANES), F32),
                        pltpu.VMEM((hpb, tq, LANES), F32)],
    )
    return pl.pallas_call(
        functools.partial(_attn_prompt_kernel, hd=hd),
        grid_spec=grid_spec,
        out_shape=jax.ShapeDtypeStruct((bsz, n, nh * hd), F32),
        compiler_params=_cparams("arbitrary", "arbitrary", "arbitrary"),
        name="attn_prompt",
    )(qi_tab, ki_tab, qa, ka, va)


def _oproj_kernel(o_ref, x_ref, w_ref, g_ref, y_ref):
    y = _dot(o_ref[...].astype(BF16), w_ref[...])
    y_ref[...] = x_ref[...] + _rms(y, g_ref[...])


def _oproj(o, x, w, g, tm):
    m, d = x.shape
    row = pl.BlockSpec((tm, d), lambda t: (t, 0))
    return pl.pallas_call(
        _oproj_kernel,
        grid=(m // tm,),
        in_specs=[row, row, _const_spec((d, d)), _const_spec((1, d))],
        out_specs=row,
        out_shape=jax.ShapeDtypeStruct((m, d), F32),
        compiler_params=_cparams("arbitrary"),
        name="oproj",
    )(o, x, w, g)


def _sconv_sample_kernel(x_ref, st_ref, gpre_ref, gpost_ref, win_ref, cw_ref, wout_ref, y_ref, ns_ref):
    d = x_ref.shape[-1]
    x = x_ref[...]
    h = _rms(x, gpre_ref[...]).astype(BF16)
    proj = _dot(h, win_ref[...])
    b = proj[:, :d]
    ch = proj[:, d:2 * d] * proj[:, 2 * d:]
    s0 = st_ref[0]
    s1 = st_ref[1]
    w = cw_ref[...]
    z = s0 * w[0:1, :] + s1 * w[1:2, :] + ch * w[2:3, :]
    y = _dot((b * z).astype(BF16), wout_ref[...])
    y_ref[...] = x + _rms(y, gpost_ref[...])
    ns_ref[0] = s1
    ns_ref[1] = ch


def _sconv_sample(x, st, gpre, gpost, w_in, conv_w, w_out):
    m, d = x.shape
    return pl.pallas_call(
        _sconv_sample_kernel,
        out_shape=[jax.ShapeDtypeStruct((m, d), F32), jax.ShapeDtypeStruct((CONV_W - 1, m, d), F32)],
        compiler_params=pltpu.CompilerParams(vmem_limit_bytes=VMEM_LIMIT_BYTES),
        name="sconv_sample",
    )(x, st, gpre, gpost, w_in, conv_w, w_out)


def _ffn_sample_kernel(x_ref, st_ref, gpre_ref, gpost_ref, wg_ref, wu_ref, cw_ref, wd_ref,
                       y_ref, ns_ref, acc_ref):
    c = pl.program_id(0)

    @pl.when(c == 0)
    def _():
        acc_ref[...] = jnp.zeros_like(acc_ref)

    x = x_ref[...]
    h = _rms(x, gpre_ref[...]).astype(BF16)
    g = _dot(h, wg_ref[...])
    u = _dot(h, wu_ref[...])
    s0 = st_ref[0]
    s1 = st_ref[1]
    w = cw_ref[...]
    gc = s0 * w[0:1, :] + s1 * w[1:2, :] + g * w[2:3, :]
    act = (gc * jax.nn.sigmoid(gc) * u).astype(BF16)
    acc_ref[...] += _dot(act, wd_ref[...])
    ns_ref[0] = s1
    ns_ref[1] = g

    @pl.when(c == pl.num_programs(0) - 1)
    def _():
        y_ref[...] = x + _rms(acc_ref[...], gpost_ref[...])


def _ffn_sample(x, st, gpre, gpost, wg, wu, conv_w, wd, fc):
    m, d = x.shape
    f = wg.shape[1]
    full = pl.BlockSpec((m, d), lambda c: (0, 0))
    vec = pl.BlockSpec((1, d), lambda c: (0, 0))
    return pl.pallas_call(
        _ffn_sample_kernel,
        grid=(f // fc,),
        in_specs=[full, pl.BlockSpec((CONV_W - 1, m, fc), lambda c: (0, 0, c)), vec, vec,
                  pl.BlockSpec((d, fc), lambda c: (0, c)), pl.BlockSpec((d, fc), lambda c: (0, c)),
                  pl.BlockSpec((CONV_W, fc), lambda c: (0, c)), pl.BlockSpec((fc, d), lambda c: (c, 0))],
        out_specs=[full, pl.BlockSpec((CONV_W - 1, m, fc), lambda c: (0, 0, c))],
        out_shape=[jax.ShapeDtypeStruct((m, d), F32), jax.ShapeDtypeStruct((CONV_W - 1, m, f), F32)],
        scratch_shapes=[pltpu.VMEM((m, d), F32)],
        compiler_params=_cparams("arbitrary"),
        name="ffn_sample",
    )(x, st, gpre, gpost, wg, wu, conv_w, wd)


def _s5_sample_kernel(x_ref, h0r_ref, h0i_ref, gpre_ref, gpost_ref, win_ref, coef_ref, wbu_ref,
                      wcr_ref, wci_ref, dsk_ref, wa_ref, wb_ref, y_ref, sr_ref, si_ref):
    nsg = wbu_ref.shape[0]
    cw = wbu_ref.shape[1]
    sw = wbu_ref.shape[2] // 2
    x = x_ref[...]
    h = _rms(x, gpre_ref[...]).astype(BF16)
    u = _dot(h, win_ref[...])
    ub = u.astype(BF16)
    ar = coef_ref[6][0:1, :]
    ai = coef_ref[7][0:1, :]
    ys = []
    for s in range(nsg):
        cs = slice(s * sw, (s + 1) * sw)
        bu = _dot(ub[:, s * cw:(s + 1) * cw], wbu_ref[s])
        hr, hi = _cmul_add(bu[:, :sw], bu[:, sw:], ar[:, cs], ai[:, cs], h0r_ref[:, cs], h0i_ref[:, cs])
        sr_ref[:, cs] = hr
        si_ref[:, cs] = hi
        ys.append(_dot(hr.astype(BF16), wcr_ref[s]) - _dot(hi.astype(BF16), wci_ref[s]))
    yy = jnp.concatenate(ys, axis=1) + dsk_ref[...] * u
    z = jax.nn.gelu(yy).astype(BF16)
    out = _dot(z, wa_ref[...]) * jax.nn.sigmoid(_dot(z, wb_ref[...]))
    y_ref[...] = x + _rms(out, gpost_ref[...])


def _s5_sample(x, h0r, h0i, gpre, gpost, w_in, coef, w_bu, w_cr, w_ci, d_skip, wa, wb):
    m, d = x.shape
    ns = h0r.shape[1]
    return pl.pallas_call(
        _s5_sample_kernel,
        out_shape=[jax.ShapeDtypeStruct((m, d), F32),
                   jax.ShapeDtypeStruct((m, ns), F32), jax.ShapeDtypeStruct((m, ns), F32)],
        compiler_params=pltpu.CompilerParams(vmem_limit_bytes=VMEM_LIMIT_BYTES),
        name="s5_sample",
    )(x, h0r, h0i, gpre, gpost, w_in, coef, w_bu, w_cr, w_ci, d_skip, wa, wb)


def _qkv_sample_kernel(x_ref, gpre_ref, w_ref, bf_ref, q_ref, k_ref, v_ref, lf_ref):
    d = x_ref.shape[-1]
    h = _rms(x_ref[...], gpre_ref[...]).astype(BF16)
    proj = _dot(h, w_ref[...])
    q_ref[...] = proj[:, :d]
    k_ref[...] = proj[:, d:2 * d]
    v_ref[...] = proj[:, 2 * d:3 * d]
    lf_ref[...] = _log_sigmoid(proj[:, 3 * d:] + bf_ref[...])


def _qkv_sample(x, gpre, w_all, bf_pad):
    m, d = x.shape
    row = jax.ShapeDtypeStruct((m, d), F32)
    return pl.pallas_call(
        _qkv_sample_kernel,
        out_shape=[row, row, row, jax.ShapeDtypeStruct((m, LANES), F32)],
        compiler_params=pltpu.CompilerParams(vmem_limit_bytes=VMEM_LIMIT_BYTES),
        name="qkv_sample",
    )(x, gpre, w_all, bf_pad)


def _attn_sample_kernel(pt_ref, q_ref, kn_ref, vn_ref, lfn_ref, kc_ref, vc_ref, lfc_ref, plc_ref,
                        o_ref, m_ref, l_ref, acc_ref, suf_ref):
    del pt_ref
    pg = pl.program_id(1)
    npg = pl.num_programs(1)
    rows, nh, hd = kc_ref.shape[1], kc_ref.shape[2], kc_ref.shape[3]
    ones = jnp.ones((LANES, LANES), BF16)
    lane3 = lax.broadcasted_iota(jnp.int32, (1, nh, LANES), 2)
    q = jnp.concatenate([q_ref[0], jnp.zeros((nh, LANES - hd), F32)], axis=1)

    @pl.when(pg == 0)
    def _():
        kn = jnp.concatenate([kn_ref[0], jnp.zeros((nh, LANES - hd), F32)], axis=1)
        s_self = _dot((q * kn).astype(BF16), ones)
        m_ref[...] = s_self
        l_ref[...] = jnp.ones_like(l_ref)
        acc_ref[...] = jnp.concatenate([vn_ref[0], jnp.zeros((nh, LANES - hd), F32)], axis=1)
        suf_ref[...] = jnp.broadcast_to(lfn_ref[0], suf_ref.shape)

    lane2 = lax.broadcasted_iota(jnp.int32, (rows, LANES), 1)
    lf = jnp.where(lane2 < nh, jnp.concatenate(
        [lfc_ref[0], jnp.zeros((rows, LANES - nh), F32)], axis=1), 0.0)
    ri = lax.broadcasted_iota(jnp.int32, (rows, rows), 0)
    ci = lax.broadcasted_iota(jnp.int32, (rows, rows), 1)
    upper = (ci > ri).astype(F32)
    bias = jnp.dot(upper, lf, preferred_element_type=F32, precision=lax.Precision.HIGHEST) + suf_ref[0:1, :]
    suf_ref[...] = jnp.broadcast_to(bias[0:1, :] + lf[0:1, :], suf_ref.shape)
    hi, mid, lo = _split3(bias)
    placed = _dot((hi.astype(BF16)), plc_ref[0]) + _dot(mid.astype(BF16), plc_ref[1]) + _dot(lo.astype(BF16), plc_ref[2])
    hsel = lax.broadcasted_iota(jnp.int32, (1, nh, LANES), 1)
    own = (lane3 >= hd) & (((lane3 - hd) % nh) == hsel)
    kq = jnp.concatenate([kc_ref[0], jnp.zeros((rows, nh, LANES - hd), F32)], axis=2) * q[None]
    lhs = jnp.where(lane3 < hd, kq, jnp.where(own, placed[:, None, :], 0.0))
    s = _dot(lhs.astype(BF16).reshape(rows * nh, LANES), ones).reshape(rows, nh, LANES)
    m_prev = m_ref[...]
    m_new = jnp.maximum(m_prev, jnp.max(s, axis=0))
    pr = jnp.exp(s - m_new[None])
    alpha = jnp.exp(m_prev - m_new)
    l_ref[...] = alpha * l_ref[...] + jnp.sum(pr, axis=0)
    vv = jnp.concatenate([vc_ref[0], jnp.zeros((rows, nh, LANES - hd), F32)], axis=2)
    pv = jnp.sum(pr.astype(BF16).astype(F32) * vv, axis=0)
    acc_ref[...] = alpha * acc_ref[...] + pv
    m_ref[...] = m_new

    @pl.when(pg == npg - 1)
    def _():
        o_ref[0] = (acc_ref[...] / l_ref[...])[:, :hd]


def _attn_sample(page_table, q, k_new, v_new, lf_new, cache_k, cache_v, cache_lf, plc):
    m, nh, hd = q.shape
    npg = page_table.shape[1]
    rows = cache_k.shape[1]
    tok = pl.BlockSpec((1, nh, hd), lambda b, p, pt: (b, 0, 0))
    page = pl.BlockSpec((1, rows, nh, hd), lambda b, p, pt: (pt[b, npg - 1 - p], 0, 0, 0))
    grid_spec = pltpu.PrefetchScalarGridSpec(
        num_scalar_prefetch=1,
        grid=(m, npg),
        in_specs=[tok, tok, tok,
                  pl.BlockSpec((1, 1, LANES), lambda b, p, pt: (b, 0, 0)),
                  page, page,
                  pl.BlockSpec((1, rows, nh), lambda b, p, pt: (pt[b, npg - 1 - p], 0, 0)),
                  pl.BlockSpec(plc.shape, lambda b, p, pt: (0, 0, 0))],
        out_specs=tok,
        scratch_shapes=[pltpu.VMEM((nh, LANES), F32), pltpu.VMEM((nh, LANES), F32),
                        pltpu.VMEM((nh, LANES), F32), pltpu.VMEM((SUBLANES, LANES), F32)],
    )
    return pl.pallas_call(
        _attn_sample_kernel,
        grid_spec=grid_spec,
        out_shape=jax.ShapeDtypeStruct((m, nh, hd), F32),
        compiler_params=_cparams("arbitrary", "arbitrary"),
        name="attn_sample",
    )(page_table, q, k_new, v_new, lf_new, cache_k, cache_v, cache_lf, plc)


def _sample_placement(nh, hd):
    mats = []
    for j in range(3):
        mat = [[0.0] * LANES for _ in range(LANES)]
        for h in range(nh):
            mat[h][hd + j * nh + h] = 1.0
        mats.append(mat)
    return jnp.array(mats, BF16)


TM_MIX = 256
TM_FFN = 256
FC_FFN = 256
TQ_ATTN = 512


def kernel(x_prompt, x_sample, state_sconv_l0, state_ssm_re_l1, state_ssm_im_l1, cache_k_l2, cache_v_l2, cache_logf_l2, state_sconv_l3, state_ffn_conv, page_table, sc_w_in_l0, sc_conv_w_l0, sc_w_out_l0, s5_w_in_l1, s5_lambda_re_l1, s5_lambda_im_l1, s5_log_dt_l1, s5_b_re_l1, s5_b_im_l1, s5_c_re_l1, s5_c_im_l1, s5_d_l1, s5_glu_wa_l1, s5_glu_wb_l1, fox_w_qkvf_l2, fox_b_f_l2, fox_w_o_l2, sc_w_in_l3, sc_conv_w_l3, sc_w_out_l3, norm_mix_pre, norm_mix_post, norm_ffn_pre, norm_ffn_post, ffn_w_gate, ffn_w_up, ffn_conv_w, ffn_w_down):
    bp, n, d = x_prompt.shape
    m = x_sample.shape[0]
    nh = fox_b_f_l2.shape[0]
    hd = d // nh
    f = ffn_w_gate.shape[-1]
    g, p = s5_lambda_re_l1.shape
    ns = g * p
    bf = lambda w: w.astype(BF16)
    row = lambda v: v.reshape(1, -1)

    tm_mix = min(TM_MIX, n)
    tm_ffn = min(TM_FFN, n)
    tq = min(TQ_ATTN, n)
    fc = min(FC_FFN, f)

    coef, w_bu, w_cr, w_ci = _s5_tables(s5_lambda_re_l1, s5_lambda_im_l1, s5_log_dt_l1,
                                        s5_b_re_l1, s5_b_im_l1, s5_c_re_l1, s5_c_im_l1)
    scale = hd ** -0.5
    w_qkvf = jnp.concatenate([fox_w_qkvf_l2[:, :d] * scale, fox_w_qkvf_l2[:, d:],
                              jnp.zeros((d, LANES - nh), F32)], axis=1).astype(BF16)
    bf_pad = jnp.concatenate([fox_b_f_l2, jnp.zeros((LANES - nh,), F32)]).reshape(1, LANES)
    pq, pk = _fox_placement(nh)
    plc = _sample_placement(nh, hd)
    sc_params = {0: (bf(sc_w_in_l0), sc_conv_w_l0, bf(sc_w_out_l0)),
                 3: (bf(sc_w_in_l3), sc_conv_w_l3, bf(sc_w_out_l3))}
    s5_w = (bf(s5_w_in_l1), coef, w_bu, w_cr, w_ci, row(s5_d_l1), bf(s5_glu_wa_l1), bf(s5_glu_wb_l1))
    w_o = bf(fox_w_o_l2)
    wg, wu, wd = bf(ffn_w_gate), bf(ffn_w_up), bf(ffn_w_down)

    xp = x_prompt
    ffn_p = []
    xp, sc0_p = _sconv_prompt(xp, row(norm_mix_pre[0]), row(norm_mix_post[0]), *sc_params[0], tm_mix)
    xp, fb = _ffn_prompt(xp, row(norm_ffn_pre[0]), row(norm_ffn_post[0]), wg[0], wu[0], ffn_conv_w[0], wd[0], tm_ffn, fc)
    ffn_p.append(fb)
    zeros_state = jnp.zeros((bp, 1, ns), F32)
    xp, sr_p, si_p = _s5_prompt(xp, row(norm_mix_pre[1]), row(norm_mix_post[1]), *s5_w,
                                zeros_state, zeros_state, tm_mix)
    xp, fb = _ffn_prompt(xp, row(norm_ffn_pre[1]), row(norm_ffn_post[1]), wg[1], wu[1], ffn_conv_w[1], wd[1], tm_ffn, fc)
    ffn_p.append(fb)
    qa, ka, va, k_p, v_p, lf_p = _qkv_prompt(xp, row(norm_mix_pre[2]), w_qkvf, bf_pad, pq, pk, nh, tm_mix)
    o_p = _attn_prompt(qa, ka, va, hd, tq)
    xp = _oproj(o_p.reshape(bp * n, d), xp.reshape(bp * n, d), w_o, row(norm_mix_post[2]),
                min(512, bp * n)).reshape(bp, n, d)
    xp, fb = _ffn_prompt(xp, row(norm_ffn_pre[2]), row(norm_ffn_post[2]), wg[2], wu[2], ffn_conv_w[2], wd[2], tm_ffn, fc)
    ffn_p.append(fb)
    xp, sc3_p = _sconv_prompt(xp, row(norm_mix_pre[3]), row(norm_mix_post[3]), *sc_params[3], tm_mix)
    xp, fb = _ffn_prompt(xp, row(norm_ffn_pre[3]), row(norm_ffn_post[3]), wg[3], wu[3], ffn_conv_w[3], wd[3], tm_ffn, fc)
    ffn_p.append(fb)

    xs = x_sample.reshape(m, d)
    ffn_s = []
    st_ffn = jnp.swapaxes(state_ffn_conv, 1, 2)

    def ffn_s_layer(xs, i):
        y, ns_ = _ffn_sample(xs, st_ffn[i], row(norm_ffn_pre[i]), row(norm_ffn_post[i]),
                             wg[i], wu[i], ffn_conv_w[i], wd[i], fc)
        ffn_s.append(jnp.swapaxes(ns_, 0, 1))
        return y

    xs, sc0_s = _sconv_sample(xs, jnp.swapaxes(state_sconv_l0, 0, 1), row(norm_mix_pre[0]),
                              row(norm_mix_post[0]), *sc_params[0])
    xs = ffn_s_layer(xs, 0)
    xs, sr_s, si_s = _s5_sample(xs, state_ssm_re_l1.reshape(m, ns), state_ssm_im_l1.reshape(m, ns),
                                row(norm_mix_pre[1]), row(norm_mix_post[1]), *s5_w)
    xs = ffn_s_layer(xs, 1)
    q_s, k_s, v_s, lf_s = _qkv_sample(xs, row(norm_mix_pre[2]), w_qkvf, bf_pad)
    o_s = _attn_sample(page_table, q_s.reshape(m, nh, hd), k_s.reshape(m, nh, hd), v_s.reshape(m, nh, hd),
                       lf_s.reshape(m, 1, LANES), cache_k_l2, cache_v_l2, cache_logf_l2, plc)
    xs = _oproj(o_s.reshape(m, d), xs, w_o, row(norm_mix_post[2]), m)
    xs = ffn_s_layer(xs, 2)
    xs, sc3_s = _sconv_sample(xs, jnp.swapaxes(state_sconv_l3, 0, 1), row(norm_mix_pre[3]),
                              row(norm_mix_post[3]), *sc_params[3])
    xs = ffn_s_layer(xs, 3)

    return (xp, xs.reshape(m, 1, d),
            sc0_p, jnp.swapaxes(sc0_s, 0, 1),
            sr_p.reshape(bp, g, p), sr_s.reshape(m, g, p), si_p.reshape(bp, g, p), si_s.reshape(m, g, p),
            k_p.reshape(bp, n, nh, hd), k_s.reshape(m, 1, nh, hd),
            v_p.reshape(bp, n, nh, hd), v_s.reshape(m, 1, nh, hd),
            lf_p, lf_s[:, :nh].reshape(m, 1, nh),
            sc3_p, jnp.swapaxes(sc3_s, 0, 1),
            jnp.stack(ffn_p), jnp.stack(ffn_s))
```

```python
import functools
import math

import jax
import jax.numpy as jnp
from jax import lax
from jax.experimental import pallas as pl
from jax.experimental.pallas import tpu as pltpu

F32 = jnp.float32
BF16 = jnp.bfloat16

RMS_EPS = 1e-6
CONV_W = 3
S5_GROUP = 16
S5_STATE = 64
FOX_HEADS = 16
PAGE_SIZE = 128

LANES = 128
SUBLANES = 8
MXU_DIM = 256
VMEM_LIMIT_BYTES = 56 * 1024 * 1024

NEG_BIG = -1e30
SCAN_LANES = 512


def _cparams(*sem):
    return pltpu.CompilerParams(dimension_semantics=sem, vmem_limit_bytes=VMEM_LIMIT_BYTES)


def _const_spec(shape):
    nd = len(shape)
    return pl.BlockSpec(shape, lambda *_: (0,) * nd, pipeline_mode=pl.Buffered(1))


def _rms(x, g):
    ms = jnp.mean(x * x, axis=-1, keepdims=True)
    return x * lax.rsqrt(ms + RMS_EPS) * g


def _dot(a, b):
    return jnp.dot(a, b, preferred_element_type=F32)


def _shift_rows(cur, prev_tail, k):
    rolled = pltpu.roll(cur, k, axis=0)
    rid = lax.broadcasted_iota(jnp.int32, (SUBLANES, cur.shape[1]), 0)
    head = jnp.where(rid < k, pltpu.roll(prev_tail, k, axis=0), rolled[:SUBLANES, :])
    return jnp.concatenate([head, rolled[SUBLANES:, :]], axis=0)


def _causal_conv3(cur, prev_tail, w):
    x1 = _shift_rows(cur, prev_tail, 1)
    x2 = _shift_rows(cur, prev_tail, 2)
    return x2 * w[0:1, :] + x1 * w[1:2, :] + cur * w[2:3, :]


def _sconv_prompt_kernel(x_ref, gpre_ref, gpost_ref, win_ref, cw_ref, wout_ref,
                         y_ref, st_ref, tail_ref):
    t = pl.program_id(1)
    nt = pl.num_programs(1)
    d = x_ref.shape[-1]
    tm = x_ref.shape[1]

    @pl.when(t == 0)
    def _():
        tail_ref[...] = jnp.zeros_like(tail_ref)

    x = x_ref[0]
    h = _rms(x, gpre_ref[...]).astype(BF16)
    proj = _dot(h, win_ref[...])
    b = proj[:, :d]
    ch = proj[:, d:2 * d] * proj[:, 2 * d:]
    z = _causal_conv3(ch, tail_ref[...], cw_ref[...])
    y = _dot((b * z).astype(BF16), wout_ref[...])
    y_ref[0] = x + _rms(y, gpost_ref[...])
    tail_ref[...] = ch[tm - SUBLANES:, :]

    @pl.when(t == nt - 1)
    def _():
        st_ref[0] = ch[tm - (CONV_W - 1):, :]


def _sconv_prompt(x, gpre, gpost, w_in, conv_w, w_out, tm):
    bsz, n, d = x.shape
    return pl.pallas_call(
        _sconv_prompt_kernel,
        grid=(bsz, n // tm),
        in_specs=[
            pl.BlockSpec((1, tm, d), lambda b, t: (b, t, 0)),
            _const_spec((1, d)), _const_spec((1, d)),
            _const_spec((d, 3 * d)), _const_spec((CONV_W, d)), _const_spec((d, d)),
        ],
        out_specs=[
            pl.BlockSpec((1, tm, d), lambda b, t: (b, t, 0)),
            pl.BlockSpec((1, CONV_W - 1, d), lambda b, t: (b, 0, 0)),
        ],
        out_shape=[jax.ShapeDtypeStruct((bsz, n, d), F32),
                   jax.ShapeDtypeStruct((bsz, CONV_W - 1, d), F32)],
        scratch_shapes=[pltpu.VMEM((SUBLANES, d), F32)],
        compiler_params=_cparams("arbitrary", "arbitrary"),
        name="sconv_prompt",
    )(x, gpre, gpost, w_in, conv_w, w_out)


def _ffn_prompt_kernel(x_ref, gpre_ref, gpost_ref, wg_ref, wu_ref, cw_ref, wd_ref,
                       y_ref, st_ref, tail_ref, act_ref, *, fc):
    t = pl.program_id(1)
    nt = pl.num_programs(1)
    tm = x_ref.shape[1]
    f = wg_ref.shape[1]

    @pl.when(t == 0)
    def _():
        tail_ref[...] = jnp.zeros_like(tail_ref)

    x = x_ref[0]
    h = _rms(x, gpre_ref[...]).astype(BF16)
    for c in range(f // fc):
        cs = slice(c * fc, (c + 1) * fc)
        g = _dot(h, wg_ref[:, cs])
        u = _dot(h, wu_ref[:, cs])
        gc = _causal_conv3(g, tail_ref[:, cs], cw_ref[:, cs])
        act_ref[:, cs] = (gc * jax.nn.sigmoid(gc) * u).astype(BF16)
        tail_ref[:, cs] = g[tm - SUBLANES:, :]
    y = _dot(act_ref[...], wd_ref[...])
    y_ref[0] = x + _rms(y, gpost_ref[...])

    @pl.when(t == nt - 1)
    def _():
        st_ref[0] = tail_ref[SUBLANES - (CONV_W - 1):, :]


def _ffn_prompt(x, gpre, gpost, wg, wu, conv_w, wd, tm, fc):
    bsz, n, d = x.shape
    f = wg.shape[1]
    return pl.pallas_call(
        functools.partial(_ffn_prompt_kernel, fc=fc),
        grid=(bsz, n // tm),
        in_specs=[
            pl.BlockSpec((1, tm, d), lambda b, t: (b, t, 0)),
            _const_spec((1, d)), _const_spec((1, d)),
            _const_spec((d, f)), _const_spec((d, f)), _const_spec((CONV_W, f)), _const_spec((f, d)),
        ],
        out_specs=[
            pl.BlockSpec((1, tm, d), lambda b, t: (b, t, 0)),
            pl.BlockSpec((1, CONV_W - 1, f), lambda b, t: (b, 0, 0)),
        ],
        out_shape=[jax.ShapeDtypeStruct((bsz, n, d), F32),
                   jax.ShapeDtypeStruct((bsz, CONV_W - 1, f), F32)],
        scratch_shapes=[pltpu.VMEM((SUBLANES, f), F32), pltpu.VMEM((tm, f), BF16)],
        compiler_params=_cparams("arbitrary", "arbitrary"),
        name="ffn_prompt",
    )(x, gpre, gpost, wg, wu, conv_w, wd)


def _s5_prep_kernel(lr_ref, li_ref, ldt_ref, bre_ref, bim_ref, pwr_ref, pwi_ref, bbr_ref, bbi_ref):
    lr = lr_ref[...]
    li = li_ref[...]
    dt = jnp.exp(ldt_ref[...])
    for k in range(SUBLANES):
        mag = jnp.exp(lr * dt * (k + 1.0))
        ang = li * dt * (k + 1.0)
        pwr_ref[k] = mag * jnp.cos(ang)
        pwi_ref[k] = mag * jnp.sin(ang)
    abar_r = pwr_ref[0]
    abar_i = pwi_ref[0]
    den = lr * lr + li * li
    nr = abar_r - 1.0
    kr = (nr * lr + abar_i * li) / den
    ki = (abar_i * lr - nr * li) / den
    bre = bre_ref[...]
    bim = bim_ref[...]
    bbr_ref[...] = kr[:, None, :] * bre - ki[:, None, :] * bim
    bbi_ref[...] = kr[:, None, :] * bim + ki[:, None, :] * bre


def _s5_prep(lam_re, lam_im, log_dt, b_re_t, b_im_t):
    g, p = lam_re.shape
    n = b_re_t.shape[1]
    return pl.pallas_call(
        _s5_prep_kernel,
        out_shape=[jax.ShapeDtypeStruct((SUBLANES, g, p), F32), jax.ShapeDtypeStruct((SUBLANES, g, p), F32),
                   jax.ShapeDtypeStruct((g, n, p), F32), jax.ShapeDtypeStruct((g, n, p), F32)],
        name="s5_prep",
    )(lam_re, lam_im, log_dt.reshape(g, 1), b_re_t, b_im_t)


def _s5_block_diag(w, sgroups):
    g, a, b = w.shape
    w4 = w.reshape(g // sgroups, sgroups, a, b)
    eye = jnp.eye(sgroups, dtype=w.dtype)
    return jnp.einsum('sgab,gh->sgahb', w4, eye).reshape(g // sgroups, sgroups * a, sgroups * b)


def _s5_tables(lam_re, lam_im, log_dt, b_re, b_im, c_re, c_im):
    g, p = lam_re.shape
    pwr, pwi, bbr, bbi = _s5_prep(lam_re, lam_im, log_dt,
                                  jnp.swapaxes(b_re, 1, 2), jnp.swapaxes(b_im, 1, 2))
    sg = MXU_DIM // S5_GROUP
    w_bu = jnp.concatenate([_s5_block_diag(bbr, sg), _s5_block_diag(bbi, sg)], axis=-1).astype(BF16)
    w_cr = _s5_block_diag(jnp.swapaxes(c_re, 1, 2), sg).astype(BF16)
    w_ci = _s5_block_diag(jnp.swapaxes(c_im, 1, 2), sg).astype(BF16)
    pwr = pwr.reshape(SUBLANES, g * p)
    pwi = pwi.reshape(SUBLANES, g * p)
    rows = jnp.arange(SUBLANES)[:, None]
    steps = []
    for s in (1, 2, 4):
        steps.append(jnp.where(rows >= s, pwr[s - 1][None, :], 0.0))
        steps.append(jnp.where(rows >= s, pwi[s - 1][None, :], 0.0))
    coef = jnp.stack(steps + [pwr, pwi])
    return coef, w_bu, w_cr, w_ci


def _cmul_add(xr, xi, ar, ai, sr, si):
    return xr + ar * sr - ai * si, xi + ar * si + ai * sr


def _s5_prompt_kernel(x_ref, gpre_ref, gpost_ref, win_ref, coef_ref, wbu_ref, wcr_ref, wci_ref,
                      dsk_ref, wa_ref, wb_ref, h0r_ref, h0i_ref,
                      y_ref, sr_ref, si_ref, br_ref, bi_ref, cr_ref, ci_ref):
    t = pl.program_id(1)
    tm = x_ref.shape[1]
    nsg = wbu_ref.shape[0]
    cw = wbu_ref.shape[1]
    sw = wbu_ref.shape[2] // 2

    @pl.when(t == 0)
    def _():
        cr_ref[...] = jnp.broadcast_to(h0r_ref[0], cr_ref.shape)
        ci_ref[...] = jnp.broadcast_to(h0i_ref[0], ci_ref.shape)

    x = x_ref[0]
    h = _rms(x, gpre_ref[...]).astype(BF16)
    u = _dot(h, win_ref[...])
    ub = u.astype(BF16)
    for s in range(nsg):
        bu = _dot(ub[:, s * cw:(s + 1) * cw], wbu_ref[s])
        br_ref[:, s * sw:(s + 1) * sw] = bu[:, :sw]
        bi_ref[:, s * sw:(s + 1) * sw] = bu[:, sw:]

    ns = br_ref.shape[1]
    lc = min(SCAN_LANES, ns)

    def scan_block(rb, carry):
        r0 = pl.multiple_of(rb * SUBLANES, SUBLANES)
        for c in range(ns // lc):
            cs = slice(c * lc, (c + 1) * lc)
            xr = br_ref[pl.ds(r0, SUBLANES), cs]
            xi = bi_ref[pl.ds(r0, SUBLANES), cs]
            for j, sh in enumerate((1, 2, 4)):
                xr, xi = _cmul_add(xr, xi, coef_ref[2 * j, :, cs], coef_ref[2 * j + 1, :, cs],
                                   pltpu.roll(xr, sh, axis=0), pltpu.roll(xi, sh, axis=0))
            xr, xi = _cmul_add(xr, xi, coef_ref[6, :, cs], coef_ref[7, :, cs], cr_ref[:, cs], ci_ref[:, cs])
            br_ref[pl.ds(r0, SUBLANES), cs] = xr
            bi_ref[pl.ds(r0, SUBLANES), cs] = xi
            cr_ref[:, cs] = jnp.broadcast_to(xr[SUBLANES - 1:, :], (SUBLANES, lc))
            ci_ref[:, cs] = jnp.broadcast_to(xi[SUBLANES - 1:, :], (SUBLANES, lc))
        return carry

    lax.fori_loop(0, tm // SUBLANES, scan_block, 0)

    ys = []
    for s in range(nsg):
        hr = br_ref[:, s * sw:(s + 1) * sw].astype(BF16)
        hi = bi_ref[:, s * sw:(s + 1) * sw].astype(BF16)
        ys.append(_dot(hr, wcr_ref[s]) - _dot(hi, wci_ref[s]))
    yy = jnp.concatenate(ys, axis=1) + dsk_ref[...] * u
    z = jax.nn.gelu(yy).astype(BF16)
    out = _dot(z, wa_ref[...]) * jax.nn.sigmoid(_dot(z, wb_ref[...]))
    y_ref[0] = x + _rms(out, gpost_ref[...])
    sr_ref[0] = cr_ref[0:1, :]
    si_ref[0] = ci_ref[0:1, :]


def _s5_prompt(x, gpre, gpost, w_in, coef, w_bu, w_cr, w_ci, d_skip, wa, wb, h0r, h0i, tm):
    bsz, n, d = x.shape
    ns = coef.shape[-1]
    return pl.pallas_call(
        _s5_prompt_kernel,
        grid=(bsz, n // tm),
        in_specs=[
            pl.BlockSpec((1, tm, d), lambda b, t: (b, t, 0)),
            _const_spec((1, d)), _const_spec((1, d)), _const_spec((d, d)),
            _const_spec(coef.shape), _const_spec(w_bu.shape), _const_spec(w_cr.shape), _const_spec(w_ci.shape),
            _const_spec((1, d)), _const_spec((d, d)), _const_spec((d, d)),
            pl.BlockSpec((1, 1, ns), lambda b, t: (b, 0, 0)),
            pl.BlockSpec((1, 1, ns), lambda b, t: (b, 0, 0)),
        ],
        out_specs=[
            pl.BlockSpec((1, tm, d), lambda b, t: (b, t, 0)),
            pl.BlockSpec((1, 1, ns), lambda b, t: (b, 0, 0)),
            pl.BlockSpec((1, 1, ns), lambda b, t: (b, 0, 0)),
        ],
        out_shape=[jax.ShapeDtypeStruct((bsz, n, d), F32),
                   jax.ShapeDtypeStruct((bsz, 1, ns), F32), jax.ShapeDtypeStruct((bsz, 1, ns), F32)],
        scratch_shapes=[pltpu.VMEM((tm, ns), F32), pltpu.VMEM((tm, ns), F32),
                        pltpu.VMEM((SUBLANES, ns), F32), pltpu.VMEM((SUBLANES, ns), F32)],
        compiler_params=_cparams("arbitrary", "arbitrary"),
        name="s5_prompt",
    )(x, gpre, gpost, w_in, coef, w_bu, w_cr, w_ci, d_skip, wa, wb, h0r, h0i)


def _log_sigmoid(x):
    return -(jnp.maximum(-x, 0.0) + jnp.log1p(jnp.exp(-jnp.abs(x))))


def _split3(x):
    hi = x.astype(BF16).astype(F32)
    r = x - hi
    mid = r.astype(BF16).astype(F32)
    lo = r - mid
    return hi, mid, lo


def _fox_placement(nh):
    hd = LANES // 2
    pq = [[0.0] * (nh * LANES) for _ in range(LANES)]
    pk = [[0.0] * (nh * LANES) for _ in range(LANES)]
    one = 3 * nh
    for h in range(nh):
        base = h * LANES + hd
        for j in range(3):
            pq[j * nh + h][base + j] = 1.0
            pq[one][base + 3 + j] = 1.0
            pk[one][base + j] = 1.0
            pk[j * nh + h][base + 3 + j] = -1.0
    return jnp.array(pq, BF16), jnp.array(pk, BF16)


def _qkv_prompt_kernel(x_ref, gpre_ref, w_ref, bf_ref, pq_ref, pk_ref,
                       qa_ref, ka_ref, va_ref, k_ref, v_ref, lf_ref, carry_ref):
    t = pl.program_id(1)
    tm = x_ref.shape[1]
    d = x_ref.shape[-1]
    nh = qa_ref.shape[1]
    hd = d // nh

    @pl.when(t == 0)
    def _():
        carry_ref[...] = jnp.zeros_like(carry_ref)

    x = x_ref[0]
    h = _rms(x, gpre_ref[...]).astype(BF16)
    proj = _dot(h, w_ref[...])
    lane = lax.broadcasted_iota(jnp.int32, (tm, LANES), 1)
    logf = jnp.where(lane < nh, _log_sigmoid(proj[:, 3 * d:] + bf_ref[...]), 0.0)
    ri = lax.broadcasted_iota(jnp.int32, (tm, tm), 0)
    ci = lax.broadcasted_iota(jnp.int32, (tm, tm), 1)
    tri = (ci <= ri).astype(F32)
    cum = jnp.dot(tri, logf, preferred_element_type=F32, precision=lax.Precision.HIGHEST) + carry_ref[0:1, :]
    carry_ref[...] = jnp.broadcast_to(cum[tm - 1:, :], carry_ref.shape)
    hi, mid, lo = _split3(cum)
    src = hi + pltpu.roll(mid, nh, axis=1) + pltpu.roll(lo, 2 * nh, axis=1) + (lane == 3 * nh).astype(F32)
    src = src.astype(BF16)
    aug_q = _dot(src, pq_ref[...])
    aug_k = _dot(src, pk_ref[...])
    low = lane < hd
    for c in range(d // LANES):
        for o, (dst, aug) in enumerate(((qa_ref, aug_q), (ka_ref, aug_k), (va_ref, None))):
            blk = proj[:, o * d + c * LANES:o * d + (c + 1) * LANES]
            for half in range(LANES // hd):
                hh = c * (LANES // hd) + half
                v = blk if half == 0 else pltpu.roll(blk, LANES - half * hd, axis=1)
                fill = 0.0 if aug is None else aug[:, hh * LANES:(hh + 1) * LANES]
                dst[0, hh] = jnp.where(low, v, fill).astype(BF16)
    k_ref[0] = proj[:, d:2 * d]
    v_ref[0] = proj[:, 2 * d:3 * d]
    lf_ref[0] = logf[:, :nh]


def _qkv_prompt(x, gpre, w_all, bf_pad, pq, pk, nh, tm):
    bsz, n, d = x.shape
    head_spec = pl.BlockSpec((1, nh, tm, LANES), lambda b, t: (b, 0, t, 0))
    row_spec = pl.BlockSpec((1, tm, d), lambda b, t: (b, t, 0))
    head_shape = jax.ShapeDtypeStruct((bsz, nh, n, LANES), BF16)
    return pl.pallas_call(
        _qkv_prompt_kernel,
        grid=(bsz, n // tm),
        in_specs=[row_spec, _const_spec((1, d)), _const_spec(w_all.shape), _const_spec((1, LANES)),
                  _const_spec(pq.shape), _const_spec(pk.shape)],
        out_specs=[head_spec, head_spec, head_spec, row_spec, row_spec,
                   pl.BlockSpec((1, tm, nh), lambda b, t: (b, t, 0))],
        out_shape=[head_shape, head_shape, head_shape,
                   jax.ShapeDtypeStruct((bsz, n, d), F32), jax.ShapeDtypeStruct((bsz, n, d), F32),
                   jax.ShapeDtypeStruct((bsz, n, nh), F32)],
        scratch_shapes=[pltpu.VMEM((SUBLANES, LANES), F32)],
        compiler_params=_cparams("arbitrary", "arbitrary"),
        name="qkv_prompt",
    )(x, gpre, w_all, bf_pad, pq, pk)


def _attn_prompt_kernel(qi_ref, ki_ref, q_ref, k_ref, v_ref, o_ref, m_ref, l_ref, acc_ref, *, hd):
    p = pl.program_id(2)
    qi = qi_ref[p]
    ki = ki_ref[p]
    tq = q_ref.shape[2]
    tk = k_ref.shape[2]
    nrep = tk // LANES

    @pl.when(ki == 0)
    def _():
        m_ref[...] = jnp.full_like(m_ref, NEG_BIG)
        l_ref[...] = jnp.zeros_like(l_ref)
        acc_ref[...] = jnp.zeros_like(acc_ref)

    def step(masked):
        for j in range(q_ref.shape[1]):
            s = lax.dot_general(q_ref[0, j], k_ref[0, j], (((1,), (1,)), ((), ())),
                                preferred_element_type=F32)
            if masked:
                row = lax.broadcasted_iota(jnp.int32, (tq, tk), 0)
                col = lax.broadcasted_iota(jnp.int32, (tq, tk), 1)
                s = jnp.where(col <= row, s, NEG_BIG)
            m_prev = m_ref[j]
            m_new = jnp.maximum(m_prev, jnp.max(s, axis=1, keepdims=True))
            pr = jnp.exp(s - jnp.concatenate([m_new] * nrep, axis=1))
            alpha = jnp.exp(m_prev - m_new)
            l_ref[j] = alpha * l_ref[j] + jnp.sum(pr, axis=1, keepdims=True)
            acc_ref[j] = alpha * acc_ref[j] + _dot(pr.astype(BF16), v_ref[0, j])
            m_ref[j] = m_new

    @pl.when(ki < qi)
    def _():
        step(False)

    @pl.when(ki == qi)
    def _():
        step(True)
        lane = lax.broadcasted_iota(jnp.int32, (tq, LANES), 1)
        out = acc_ref[0] / l_ref[0]
        for j in range(1, q_ref.shape[1]):
            oj = pltpu.roll(acc_ref[j] / l_ref[j], j * hd, axis=1)
            out = jnp.where(lane < j * hd, out, oj)
        o_ref[0] = out


def _attn_prompt(qa, ka, va, hd, tq):
    bsz, nh, n, _ = qa.shape
    hpb = LANES // hd
    nq = n // tq
    pairs = [(q, k) for q in range(nq) for k in range(q + 1)]
    qi_tab = jnp.array([q for q, _ in pairs], jnp.int32)
    ki_tab = jnp.array([k for _, k in pairs], jnp.int32)
    grid_spec = pltpu.PrefetchScalarGridSpec(
        num_scalar_prefetch=2,
        grid=(bsz, nh // hpb, len(pairs)),
        in_specs=[
            pl.BlockSpec((1, hpb, tq, LANES), lambda b, h, p, qt, kt: (b, h, qt[p], 0)),
            pl.BlockSpec((1, hpb, tq, LANES), lambda b, h, p, qt, kt: (b, h, kt[p], 0)),
            pl.BlockSpec((1, hpb, tq, LANES), lambda b, h, p, qt, kt: (b, h, kt[p], 0)),
        ],
        out_specs=pl.BlockSpec((1, tq, LANES), lambda b, h, p, qt, kt: (b, qt[p], h)),
        scratch_shapes=[pltpu.VMEM((hpb, tq, LANES), F32), pltpu.VMEM((hpb, tq, LANES), F32),
                        pltpu.VMEM((hpb, tq, LANES), F32)],
    )
    return pl.pallas_call(
        functools.partial(_attn_prompt_kernel, hd=hd),
        grid_spec=grid_spec,
        out_shape=jax.ShapeDtypeStruct((bsz, n, nh * hd), F32),
        compiler_params=_cparams("arbitrary", "arbitrary", "arbitrary"),
        name="attn_prompt",
    )(qi_tab, ki_tab, qa, ka, va)


def _oproj_kernel(o_ref, x_ref, w_ref, g_ref, y_ref):
    y = _dot(o_ref[...].astype(BF16), w_ref[...])
    y_ref[...] = x_ref[...] + _rms(y, g_ref[...])


def _oproj(o, x, w, g, tm):
    m, d = x.shape
    row = pl.BlockSpec((tm, d), lambda t: (t, 0))
    return pl.pallas_call(
        _oproj_kernel,
        grid=(m // tm,),
        in_specs=[row, row, _const_spec((d, d)), _const_spec((1, d))],
        out_specs=row,
        out_shape=jax.ShapeDtypeStruct((m, d), F32),
        compiler_params=_cparams("arbitrary"),
        name="oproj",
    )(o, x, w, g)


def _sconv_sample_kernel(x_ref, st_ref, gpre_ref, gpost_ref, win_ref, cw_ref, wout_ref, y_ref, ns_ref):
    d = x_ref.shape[-1]
    x = x_ref[...]
    h = _rms(x, gpre_ref[...]).astype(BF16)
    proj = _dot(h, win_ref[...])
    b = proj[:, :d]
    ch = proj[:, d:2 * d] * proj[:, 2 * d:]
    s0 = st_ref[0]
    s1 = st_ref[1]
    w = cw_ref[...]
    z = s0 * w[0:1, :] + s1 * w[1:2, :] + ch * w[2:3, :]
    y = _dot((b * z).astype(BF16), wout_ref[...])
    y_ref[...] = x + _rms(y, gpost_ref[...])
    ns_ref[0] = s1
    ns_ref[1] = ch


def _sconv_sample(x, st, gpre, gpost, w_in, conv_w, w_out):
    m, d = x.shape
    return pl.pallas_call(
        _sconv_sample_kernel,
        out_shape=[jax.ShapeDtypeStruct((m, d), F32), jax.ShapeDtypeStruct((CONV_W - 1, m, d), F32)],
        compiler_params=pltpu.CompilerParams(vmem_limit_bytes=VMEM_LIMIT_BYTES),
        name="sconv_sample",
    )(x, st, gpre, gpost, w_in, conv_w, w_out)


def _ffn_sample_kernel(x_ref, st_ref, gpre_ref, gpost_ref, wg_ref, wu_ref, cw_ref, wd_ref,
                       y_ref, ns_ref, acc_ref):
    c = pl.program_id(0)

    @pl.when(c == 0)
    def _():
        acc_ref[...] = jnp.zeros_like(acc_ref)

    x = x_ref[...]
    h = _rms(x, gpre_ref[...]).astype(BF16)
    g = _dot(h, wg_ref[...])
    u = _dot(h, wu_ref[...])
    s0 = st_ref[0]
    s1 = st_ref[1]
    w = cw_ref[...]
    gc = s0 * w[0:1, :] + s1 * w[1:2, :] + g * w[2:3, :]
    act = (gc * jax.nn.sigmoid(gc) * u).astype(BF16)
    acc_ref[...] += _dot(act, wd_ref[...])
    ns_ref[0] = s1
    ns_ref[1] = g

    @pl.when(c == pl.num_programs(0) - 1)
    def _():
        y_ref[...] = x + _rms(acc_ref[...], gpost_ref[...])


def _ffn_sample(x, st, gpre, gpost, wg, wu, conv_w, wd, fc):
    m, d = x.shape
    f = wg.shape[1]
    full = pl.BlockSpec((m, d), lambda c: (0, 0))
    vec = pl.BlockSpec((1, d), lambda c: (0, 0))
    return pl.pallas_call(
        _ffn_sample_kernel,
        grid=(f // fc,),
        in_specs=[full, pl.BlockSpec((CONV_W - 1, m, fc), lambda c: (0, 0, c)), vec, vec,
                  pl.BlockSpec((d, fc), lambda c: (0, c)), pl.BlockSpec((d, fc), lambda c: (0, c)),
                  pl.BlockSpec((CONV_W, fc), lambda c: (0, c)), pl.BlockSpec((fc, d), lambda c: (c, 0))],
        out_specs=[full, pl.BlockSpec((CONV_W - 1, m, fc), lambda c: (0, 0, c))],
        out_shape=[jax.ShapeDtypeStruct((m, d), F32), jax.ShapeDtypeStruct((CONV_W - 1, m, f), F32)],
        scratch_shapes=[pltpu.VMEM((m, d), F32)],
        compiler_params=_cparams("arbitrary"),
        name="ffn_sample",
    )(x, st, gpre, gpost, wg, wu, conv_w, wd)


def _s5_sample_kernel(x_ref, h0r_ref, h0i_ref, gpre_ref, gpost_ref, win_ref, coef_ref, wbu_ref,
                      wcr_ref, wci_ref, dsk_ref, wa_ref, wb_ref, y_ref, sr_ref, si_ref):
    nsg = wbu_ref.shape[0]
    cw = wbu_ref.shape[1]
    sw = wbu_ref.shape[2] // 2
    x = x_ref[...]
    h = _rms(x, gpre_ref[...]).astype(BF16)
    u = _dot(h, win_ref[...])
    ub = u.astype(BF16)
    ar = coef_ref[6][0:1, :]
    ai = coef_ref[7][0:1, :]
    ys = []
    for s in range(nsg):
        cs = slice(s * sw, (s + 1) * sw)
        bu = _dot(ub[:, s * cw:(s + 1) * cw], wbu_ref[s])
        hr, hi = _cmul_add(bu[:, :sw], bu[:, sw:], ar[:, cs], ai[:, cs], h0r_ref[:, cs], h0i_ref[:, cs])
        sr_ref[:, cs] = hr
        si_ref[:, cs] = hi
        ys.append(_dot(hr.astype(BF16), wcr_ref[s]) - _dot(hi.astype(BF16), wci_ref[s]))
    yy = jnp.concatenate(ys, axis=1) + dsk_ref[...] * u
    z = jax.nn.gelu(yy).astype(BF16)
    out = _dot(z, wa_ref[...]) * jax.nn.sigmoid(_dot(z, wb_ref[...]))
    y_ref[...] = x + _rms(out, gpost_ref[...])


def _s5_sample(x, h0r, h0i, gpre, gpost, w_in, coef, w_bu, w_cr, w_ci, d_skip, wa, wb):
    m, d = x.shape
    ns = h0r.shape[1]
    return pl.pallas_call(
        _s5_sample_kernel,
        out_shape=[jax.ShapeDtypeStruct((m, d), F32),
                   jax.ShapeDtypeStruct((m, ns), F32), jax.ShapeDtypeStruct((m, ns), F32)],
        compiler_params=pltpu.CompilerParams(vmem_limit_bytes=VMEM_LIMIT_BYTES),
        name="s5_sample",
    )(x, h0r, h0i, gpre, gpost, w_in, coef, w_bu, w_cr, w_ci, d_skip, wa, wb)


def _qkv_sample_kernel(x_ref, gpre_ref, w_ref, bf_ref, q_ref, k_ref, v_ref, lf_ref):
    d = x_ref.shape[-1]
    h = _rms(x_ref[...], gpre_ref[...]).astype(BF16)
    proj = _dot(h, w_ref[...])
    q_ref[...] = proj[:, :d]
    k_ref[...] = proj[:, d:2 * d]
    v_ref[...] = proj[:, 2 * d:3 * d]
    lf_ref[...] = _log_sigmoid(proj[:, 3 * d:] + bf_ref[...])


def _qkv_sample(x, gpre, w_all, bf_pad):
    m, d = x.shape
    row = jax.ShapeDtypeStruct((m, d), F32)
    return pl.pallas_call(
        _qkv_sample_kernel,
        out_shape=[row, row, row, jax.ShapeDtypeStruct((m, LANES), F32)],
        compiler_params=pltpu.CompilerParams(vmem_limit_bytes=VMEM_LIMIT_BYTES),
        name="qkv_sample",
    )(x, gpre, w_all, bf_pad)


def _attn_sample_kernel(pt_ref, q_ref, kn_ref, vn_ref, lfn_ref, *rest, npar):
    del pt_ref
    kc, vc, lfc = rest[:npar], rest[npar:2 * npar], rest[2 * npar:3 * npar]
    o_ref, m_ref, l_ref, acc_ref, suf_ref = rest[3 * npar:]
    st = pl.program_id(1)
    nst = pl.num_programs(1)
    nh, hd, rows = kc[0].shape[1], kc[0].shape[2], kc[0].shape[3]
    d = nh * hd
    nt = (((1,), (1,)), ((), ()))
    rid = lax.broadcasted_iota(jnp.int32, (nh, d), 0)
    lid = lax.broadcasted_iota(jnp.int32, (nh, d), 1)
    own = (lid >= rid * hd) & (lid < (rid + 1) * hd)
    qbd = jnp.where(own, q_ref[0], 0.0).astype(BF16)

    @pl.when(st == 0)
    def _():
        kb = jnp.broadcast_to(kn_ref[0], (LANES, d)).astype(BF16)
        m_ref[...] = lax.dot_general(qbd, kb, nt, preferred_element_type=F32)
        l_ref[...] = jnp.ones_like(l_ref)
        acc_ref[...] = jnp.broadcast_to(vn_ref[0], (nh, d))
        suf_ref[...] = lfn_ref[0]

    lf_all = jnp.concatenate([r[0] for r in lfc], axis=0)
    ri = lax.broadcasted_iota(jnp.int32, (rows, 2 * rows), 0)
    ci = lax.broadcasted_iota(jnp.int32, (rows, 2 * rows), 1)
    later = ((ri > ci) | (ci >= rows)).astype(F32)
    sums = jnp.dot(lf_all, later, preferred_element_type=F32, precision=lax.Precision.HIGHEST)
    carry = suf_ref[...]
    scores = []
    for g in range(npar):
        bias = sums[g * nh:(g + 1) * nh, :rows] + carry
        carry = carry + sums[g * nh:(g + 1) * nh, rows:]
        scores.append(_dot(qbd, kc[g][0].reshape(d, rows).astype(BF16)) + bias)
    suf_ref[...] = carry
    s = jnp.concatenate(scores, axis=1)
    m_prev = m_ref[...]
    m_new = jnp.maximum(m_prev, jnp.max(s, axis=1, keepdims=True))
    pr = jnp.exp(s - jnp.concatenate([m_new] * (npar * rows // LANES), axis=1))
    alpha = jnp.exp(m_prev - m_new)
    l_ref[...] = alpha * l_ref[...] + jnp.sum(pr, axis=1, keepdims=True)
    pb = pr.astype(BF16)
    pv = None
    for g in range(npar):
        part = lax.dot_general(pb[:, g * rows:(g + 1) * rows], vc[g][0].reshape(d, rows).astype(BF16), nt,
                               preferred_element_type=F32)
        pv = part if pv is None else pv + part
    acc_ref[...] = jnp.concatenate([alpha] * (d // LANES), axis=1) * acc_ref[...] + pv
    m_ref[...] = m_new

    @pl.when(st == nst - 1)
    def _():
        out = acc_ref[...] / jnp.concatenate([l_ref[...]] * (d // LANES), axis=1)
        o_ref[0] = jnp.sum(jnp.where(own, out, 0.0), axis=0, keepdims=True)


def _attn_sample(page_table, q, k_new, v_new, lf_new, cache_kt, cache_vt, cache_lft, npar):
    m, _, d = q.shape
    npg = page_table.shape[1]
    _, nh, hd, rows = cache_kt.shape
    assert rows == LANES and npg % npar == 0

    def page_idx(g):
        return lambda b, s, pt: (pt[b, npg - 1 - (s * npar + g)], 0, 0, 0)

    def lf_idx(g):
        return lambda b, s, pt: (pt[b, npg - 1 - (s * npar + g)], 0, 0)

    tok = pl.BlockSpec((1, 1, d), lambda b, s, pt: (b, 0, 0))
    grid_spec = pltpu.PrefetchScalarGridSpec(
        num_scalar_prefetch=1,
        grid=(m, npg // npar),
        in_specs=([tok, tok, tok, pl.BlockSpec((1, nh, LANES), lambda b, s, pt: (b, 0, 0))]
                  + [pl.BlockSpec((1, nh, hd, rows), page_idx(g)) for g in range(npar)]
                  + [pl.BlockSpec((1, nh, hd, rows), page_idx(g)) for g in range(npar)]
                  + [pl.BlockSpec((1, nh, rows), lf_idx(g)) for g in range(npar)]),
        out_specs=tok,
        scratch_shapes=[pltpu.VMEM((nh, LANES), F32), pltpu.VMEM((nh, LANES), F32),
                        pltpu.VMEM((nh, d), F32), pltpu.VMEM((nh, LANES), F32)],
    )
    return pl.pallas_call(
        functools.partial(_attn_sample_kernel, npar=npar),
        grid_spec=grid_spec,
        out_shape=jax.ShapeDtypeStruct((m, 1, d), F32),
        compiler_params=_cparams("arbitrary", "arbitrary"),
        name="attn_sample",
    )(page_table, q, k_new, v_new, lf_new, *([cache_kt] * npar), *([cache_vt] * npar), *([cache_lft] * npar))


TM_MIX = 256
TM_FFN = 512
FC_FFN = 256
TQ_ATTN = 512
PAGES_PER_STEP = 8


def kernel(x_prompt, x_sample, state_sconv_l0, state_ssm_re_l1, state_ssm_im_l1, cache_k_l2, cache_v_l2, cache_logf_l2, state_sconv_l3, state_ffn_conv, page_table, sc_w_in_l0, sc_conv_w_l0, sc_w_out_l0, s5_w_in_l1, s5_lambda_re_l1, s5_lambda_im_l1, s5_log_dt_l1, s5_b_re_l1, s5_b_im_l1, s5_c_re_l1, s5_c_im_l1, s5_d_l1, s5_glu_wa_l1, s5_glu_wb_l1, fox_w_qkvf_l2, fox_b_f_l2, fox_w_o_l2, sc_w_in_l3, sc_conv_w_l3, sc_w_out_l3, norm_mix_pre, norm_mix_post, norm_ffn_pre, norm_ffn_post, ffn_w_gate, ffn_w_up, ffn_conv_w, ffn_w_down):
    bp, n, d = x_prompt.shape
    m = x_sample.shape[0]
    nh = fox_b_f_l2.shape[0]
    hd = d // nh
    f = ffn_w_gate.shape[-1]
    g, p = s5_lambda_re_l1.shape
    ns = g * p
    bf = lambda w: w.astype(BF16)
    row = lambda v: v.reshape(1, -1)

    tm_mix = min(TM_MIX, n)
    tm_ffn = min(TM_FFN, n)
    tq = min(TQ_ATTN, n)
    fc = min(FC_FFN, f)

    coef, w_bu, w_cr, w_ci = _s5_tables(s5_lambda_re_l1, s5_lambda_im_l1, s5_log_dt_l1,
                                        s5_b_re_l1, s5_b_im_l1, s5_c_re_l1, s5_c_im_l1)
    scale = hd ** -0.5
    w_qkvf = jnp.concatenate([fox_w_qkvf_l2[:, :d] * scale, fox_w_qkvf_l2[:, d:],
                              jnp.zeros((d, LANES - nh), F32)], axis=1).astype(BF16)
    bf_pad = jnp.concatenate([fox_b_f_l2, jnp.zeros((LANES - nh,), F32)]).reshape(1, LANES)
    pq, pk = _fox_placement(nh)
    sc_params = {0: (bf(sc_w_in_l0), sc_conv_w_l0, bf(sc_w_out_l0)),
                 3: (bf(sc_w_in_l3), sc_conv_w_l3, bf(sc_w_out_l3))}
    s5_w = (bf(s5_w_in_l1), coef, w_bu, w_cr, w_ci, row(s5_d_l1), bf(s5_glu_wa_l1), bf(s5_glu_wb_l1))
    w_o = bf(fox_w_o_l2)
    wg, wu, wd = bf(ffn_w_gate), bf(ffn_w_up), bf(ffn_w_down)

    xp = x_prompt
    ffn_p = []
    xp, sc0_p = _sconv_prompt(xp, row(norm_mix_pre[0]), row(norm_mix_post[0]), *sc_params[0], tm_mix)
    xp, fb = _ffn_prompt(xp, row(norm_ffn_pre[0]), row(norm_ffn_post[0]), wg[0], wu[0], ffn_conv_w[0], wd[0], tm_ffn, fc)
    ffn_p.append(fb)
    zeros_state = jnp.zeros((bp, 1, ns), F32)
    xp, sr_p, si_p = _s5_prompt(xp, row(norm_mix_pre[1]), row(norm_mix_post[1]), *s5_w,
                                zeros_state, zeros_state, tm_mix)
    xp, fb = _ffn_prompt(xp, row(norm_ffn_pre[1]), row(norm_ffn_post[1]), wg[1], wu[1], ffn_conv_w[1], wd[1], tm_ffn, fc)
    ffn_p.append(fb)
    qa, ka, va, k_p, v_p, lf_p = _qkv_prompt(xp, row(norm_mix_pre[2]), w_qkvf, bf_pad, pq, pk, nh, tm_mix)
    o_p = _attn_prompt(qa, ka, va, hd, tq)
    xp = _oproj(o_p.reshape(bp * n, d), xp.reshape(bp * n, d), w_o, row(norm_mix_post[2]),
                min(512, bp * n)).reshape(bp, n, d)
    xp, fb = _ffn_prompt(xp, row(norm_ffn_pre[2]), row(norm_ffn_post[2]), wg[2], wu[2], ffn_conv_w[2], wd[2], tm_ffn, fc)
    ffn_p.append(fb)
    xp, sc3_p = _sconv_prompt(xp, row(norm_mix_pre[3]), row(norm_mix_post[3]), *sc_params[3], tm_mix)
    xp, fb = _ffn_prompt(xp, row(norm_ffn_pre[3]), row(norm_ffn_post[3]), wg[3], wu[3], ffn_conv_w[3], wd[3], tm_ffn, fc)
    ffn_p.append(fb)

    xs = x_sample.reshape(m, d)
    ffn_s = []
    st_ffn = jnp.swapaxes(state_ffn_conv, 1, 2)

    def ffn_s_layer(xs, i):
        y, ns_ = _ffn_sample(xs, st_ffn[i], row(norm_ffn_pre[i]), row(norm_ffn_post[i]),
                             wg[i], wu[i], ffn_conv_w[i], wd[i], fc)
        ffn_s.append(jnp.swapaxes(ns_, 0, 1))
        return y

    xs, sc0_s = _sconv_sample(xs, jnp.swapaxes(state_sconv_l0, 0, 1), row(norm_mix_pre[0]),
                              row(norm_mix_post[0]), *sc_params[0])
    xs = ffn_s_layer(xs, 0)
    xs, sr_s, si_s = _s5_sample(xs, state_ssm_re_l1.reshape(m, ns), state_ssm_im_l1.reshape(m, ns),
                                row(norm_mix_pre[1]), row(norm_mix_post[1]), *s5_w)
    xs = ffn_s_layer(xs, 1)
    q_s, k_s, v_s, lf_s = _qkv_sample(xs, row(norm_mix_pre[2]), w_qkvf, bf_pad)
    npg = page_table.shape[1]
    npar = max(c for c in range(1, PAGES_PER_STEP + 1) if npg % c == 0)
    o_s = _attn_sample(page_table, q_s.reshape(m, 1, d), k_s.reshape(m, 1, d), v_s.reshape(m, 1, d),
                       jnp.broadcast_to(lf_s[:, :nh, None], (m, nh, LANES)),
                       jnp.transpose(cache_k_l2, (0, 2, 3, 1)), jnp.transpose(cache_v_l2, (0, 2, 3, 1)),
                       jnp.transpose(cache_logf_l2, (0, 2, 1)), npar)
    xs = _oproj(o_s.reshape(m, d), xs, w_o, row(norm_mix_post[2]), m)
    xs = ffn_s_layer(xs, 2)
    xs, sc3_s = _sconv_sample(xs, jnp.swapaxes(state_sconv_l3, 0, 1), row(norm_mix_pre[3]),
                              row(norm_mix_post[3]), *sc_params[3])
    xs = ffn_s_layer(xs, 3)

    return (xp, xs.reshape(m, 1, d),
            sc0_p, jnp.swapaxes(sc0_s, 0, 1),
            sr_p.reshape(bp, g, p), sr_s.reshape(m, g, p), si_p.reshape(bp, g, p), si_s.reshape(m, g, p),
            k_p.reshape(bp, n, nh, hd), k_s.reshape(m, 1, nh, hd),
            v_p.reshape(bp, n, nh, hd), v_s.reshape(m, 1, nh, hd),
            lf_p, lf_s[:, :nh].reshape(m, 1, nh),
            sc3_p, jnp.swapaxes(sc3_s, 0, 1),
            jnp.stack(ffn_p), jnp.stack(ffn_s))
```

```python
import functools
import math

import jax
import jax.numpy as jnp
from jax import lax
from jax.experimental import pallas as pl
from jax.experimental.pallas import tpu as pltpu

F32 = jnp.float32
BF16 = jnp.bfloat16

RMS_EPS = 1e-6
CONV_W = 3
S5_GROUP = 16
S5_STATE = 64
FOX_HEADS = 16
PAGE_SIZE = 128

LANES = 128
SUBLANES = 8
MXU_DIM = 256
VMEM_LIMIT_BYTES = 56 * 1024 * 1024

NEG_BIG = -1e30
LOG2E = math.log2(math.e)
SCAN_LANES = 1024
SCAN_UNROLL = 8


def _cparams(*sem):
    return pltpu.CompilerParams(dimension_semantics=sem, vmem_limit_bytes=VMEM_LIMIT_BYTES)


def _const_spec(shape):
    nd = len(shape)
    return pl.BlockSpec(shape, lambda *_: (0,) * nd, pipeline_mode=pl.Buffered(1))


def _rms(x, g):
    ms = jnp.mean(x * x, axis=-1, keepdims=True)
    return x * lax.rsqrt(ms + RMS_EPS) * g


def _dot(a, b):
    return jnp.dot(a, b, preferred_element_type=F32)


def _shift_rows(cur, prev_tail, k):
    rolled = pltpu.roll(cur, k, axis=0)
    rid = lax.broadcasted_iota(jnp.int32, (SUBLANES, cur.shape[1]), 0)
    head = jnp.where(rid < k, pltpu.roll(prev_tail, k, axis=0), rolled[:SUBLANES, :])
    return jnp.concatenate([head, rolled[SUBLANES:, :]], axis=0)


def _causal_conv3(cur, prev_tail, w):
    x1 = _shift_rows(cur, prev_tail, 1)
    x2 = _shift_rows(cur, prev_tail, 2)
    return x2 * w[0:1, :] + x1 * w[1:2, :] + cur * w[2:3, :]


def _sconv_prompt_kernel(x_ref, gpre_ref, gpost_ref, win_ref, cw_ref, wout_ref,
                         y_ref, st_ref, tail_ref):
    t = pl.program_id(1)
    nt = pl.num_programs(1)
    d = x_ref.shape[-1]
    tm = x_ref.shape[1]

    @pl.when(t == 0)
    def _():
        tail_ref[...] = jnp.zeros_like(tail_ref)

    x = x_ref[0]
    h = _rms(x, gpre_ref[...]).astype(BF16)
    proj = _dot(h, win_ref[...])
    b = proj[:, :d]
    ch = proj[:, d:2 * d] * proj[:, 2 * d:]
    z = _causal_conv3(ch, tail_ref[...], cw_ref[...])
    y = _dot((b * z).astype(BF16), wout_ref[...])
    y_ref[0] = x + _rms(y, gpost_ref[...])
    tail_ref[...] = ch[tm - SUBLANES:, :]

    @pl.when(t == nt - 1)
    def _():
        st_ref[0] = ch[tm - (CONV_W - 1):, :]


def _sconv_prompt(x, gpre, gpost, w_in, conv_w, w_out, tm):
    bsz, n, d = x.shape
    return pl.pallas_call(
        _sconv_prompt_kernel,
        grid=(bsz, n // tm),
        in_specs=[
            pl.BlockSpec((1, tm, d), lambda b, t: (b, t, 0)),
            _const_spec((1, d)), _const_spec((1, d)),
            _const_spec((d, 3 * d)), _const_spec((CONV_W, d)), _const_spec((d, d)),
        ],
        out_specs=[
            pl.BlockSpec((1, tm, d), lambda b, t: (b, t, 0)),
            pl.BlockSpec((1, CONV_W - 1, d), lambda b, t: (b, 0, 0)),
        ],
        out_shape=[jax.ShapeDtypeStruct((bsz, n, d), F32),
                   jax.ShapeDtypeStruct((bsz, CONV_W - 1, d), F32)],
        scratch_shapes=[pltpu.VMEM((SUBLANES, d), F32)],
        compiler_params=_cparams("arbitrary", "arbitrary"),
        name="sconv_prompt",
    )(x, gpre, gpost, w_in, conv_w, w_out)


def _ffn_prompt_kernel(x_ref, gpre_ref, gpost_ref, wg_ref, wu_ref, cw_ref, wd_ref,
                       y_ref, st_ref, tail_ref, act_ref, *, fc):
    t = pl.program_id(1)
    nt = pl.num_programs(1)
    tm = x_ref.shape[1]
    f = wg_ref.shape[1]

    @pl.when(t == 0)
    def _():
        tail_ref[...] = jnp.zeros_like(tail_ref)

    x = x_ref[0]
    h = _rms(x, gpre_ref[...]).astype(BF16)
    for c in range(f // fc):
        cs = slice(c * fc, (c + 1) * fc)
        g = _dot(h, wg_ref[:, cs])
        u = _dot(h, wu_ref[:, cs])
        gc = _causal_conv3(g, tail_ref[:, cs], cw_ref[:, cs])
        act_ref[:, cs] = (gc * jax.nn.sigmoid(gc) * u).astype(BF16)
        tail_ref[:, cs] = g[tm - SUBLANES:, :]
    y = _dot(act_ref[...], wd_ref[...])
    y_ref[0] = x + _rms(y, gpost_ref[...])

    @pl.when(t == nt - 1)
    def _():
        st_ref[0] = tail_ref[SUBLANES - (CONV_W - 1):, :]


def _ffn_prompt(x, gpre, gpost, wg, wu, conv_w, wd, tm, fc):
    bsz, n, d = x.shape
    f = wg.shape[1]
    return pl.pallas_call(
        functools.partial(_ffn_prompt_kernel, fc=fc),
        grid=(bsz, n // tm),
        in_specs=[
            pl.BlockSpec((1, tm, d), lambda b, t: (b, t, 0)),
            _const_spec((1, d)), _const_spec((1, d)),
            _const_spec((d, f)), _const_spec((d, f)), _const_spec((CONV_W, f)), _const_spec((f, d)),
        ],
        out_specs=[
            pl.BlockSpec((1, tm, d), lambda b, t: (b, t, 0)),
            pl.BlockSpec((1, CONV_W - 1, f), lambda b, t: (b, 0, 0)),
        ],
        out_shape=[jax.ShapeDtypeStruct((bsz, n, d), F32),
                   jax.ShapeDtypeStruct((bsz, CONV_W - 1, f), F32)],
        scratch_shapes=[pltpu.VMEM((SUBLANES, f), F32), pltpu.VMEM((tm, f), BF16)],
        compiler_params=_cparams("arbitrary", "arbitrary"),
        name="ffn_prompt",
    )(x, gpre, gpost, wg, wu, conv_w, wd)


def _s5_prep_kernel(lr_ref, li_ref, ldt_ref, bre_ref, bim_ref,
                    pwr_ref, pwi_ref, sgr_ref, sgi_ref, bbr_ref, bbi_ref):
    lr = lr_ref[...]
    li = li_ref[...]
    dt = jnp.exp(ldt_ref[...])
    mag = jnp.exp(lr * dt)
    abar_r = mag * jnp.cos(li * dt)
    abar_i = mag * jnp.sin(li * dt)

    def powers(ref_r, ref_i, br, bi):
        cr, ci = br, bi
        for k in range(ref_r.shape[0]):
            ref_r[k] = cr
            ref_i[k] = ci
            cr, ci = cr * br - ci * bi, cr * bi + ci * br

    seg = pwr_ref.shape[0]
    powers(pwr_ref, pwi_ref, abar_r, abar_i)
    powers(sgr_ref, sgi_ref, pwr_ref[seg - 1], pwi_ref[seg - 1])
    den = lr * lr + li * li
    nr = abar_r - 1.0
    kr = (nr * lr + abar_i * li) / den
    ki = (abar_i * lr - nr * li) / den
    bre = bre_ref[...]
    bim = bim_ref[...]
    bbr_ref[...] = kr[:, None, :] * bre - ki[:, None, :] * bim
    bbi_ref[...] = kr[:, None, :] * bim + ki[:, None, :] * bre


def _s5_prep(lam_re, lam_im, log_dt, b_re_t, b_im_t, seg):
    g, p = lam_re.shape
    n = b_re_t.shape[1]
    return pl.pallas_call(
        _s5_prep_kernel,
        out_shape=[jax.ShapeDtypeStruct((seg, g, p), F32), jax.ShapeDtypeStruct((seg, g, p), F32),
                   jax.ShapeDtypeStruct((SUBLANES, g, p), F32), jax.ShapeDtypeStruct((SUBLANES, g, p), F32),
                   jax.ShapeDtypeStruct((g, n, p), F32), jax.ShapeDtypeStruct((g, n, p), F32)],
        name="s5_prep",
    )(lam_re, lam_im, log_dt.reshape(g, 1), b_re_t, b_im_t)


def _s5_block_diag(w, sgroups):
    g, a, b = w.shape
    w4 = w.reshape(g // sgroups, sgroups, a, b)
    eye = jnp.eye(sgroups, dtype=w.dtype)
    return jnp.einsum('sgab,gh->sgahb', w4, eye).reshape(g // sgroups, sgroups * a, sgroups * b)


def _s5_tables(lam_re, lam_im, log_dt, b_re, b_im, c_re, c_im, seg):
    g, p = lam_re.shape
    pwr, pwi, sgr, sgi, bbr, bbi = _s5_prep(lam_re, lam_im, log_dt,
                                            jnp.swapaxes(b_re, 1, 2), jnp.swapaxes(b_im, 1, 2), seg)
    sg = MXU_DIM // S5_GROUP
    w_bu = jnp.concatenate([_s5_block_diag(bbr, sg), _s5_block_diag(bbi, sg)], axis=-1).astype(BF16)
    w_cr = _s5_block_diag(jnp.swapaxes(c_re, 1, 2), sg).astype(BF16)
    w_ci = _s5_block_diag(jnp.swapaxes(c_im, 1, 2), sg).astype(BF16)
    ptab = jnp.stack([pwr.reshape(seg, g * p)[:1], pwi.reshape(seg, g * p)[:1]])
    sgr = sgr.reshape(SUBLANES, g * p)
    sgi = sgi.reshape(SUBLANES, g * p)
    rows = jnp.arange(SUBLANES)[:, None]
    steps = []
    for s in (1, 2, 4):
        steps.append(jnp.where(rows >= s, sgr[s - 1][None, :], 0.0))
        steps.append(jnp.where(rows >= s, sgi[s - 1][None, :], 0.0))
    coef = jnp.stack(steps + [sgr, sgi])
    return ptab, coef, w_bu, w_cr, w_ci


def _cmul_add(xr, xi, ar, ai, sr, si):
    return xr + ar * sr - ai * si, xi + ar * si + ai * sr


def _s5_prompt_kernel(x_ref, gpre_ref, gpost_ref, perm_ref, win_ref, ptab_ref, coef_ref, wbu_ref, wcr_ref,
                      wci_ref, dsk_ref, wa_ref, wb_ref, h0r_ref, h0i_ref,
                      y_ref, sr_ref, si_ref, br_ref, bi_ref, cr_ref, ci_ref):
    t = pl.program_id(1)
    tm = x_ref.shape[1]
    seg = tm // SUBLANES
    nsg = wbu_ref.shape[0]
    cw = wbu_ref.shape[1]
    sw = wbu_ref.shape[2] // 2

    @pl.when(t == 0)
    def _():
        cr_ref[...] = jnp.broadcast_to(h0r_ref[0], cr_ref.shape)
        ci_ref[...] = jnp.broadcast_to(h0i_ref[0], ci_ref.shape)

    x = x_ref[0]
    h = _dot(perm_ref[0], _rms(x, gpre_ref[...]).astype(BF16)).astype(BF16)
    u = _dot(h, win_ref[...])
    ub = u.astype(BF16)
    for s in range(nsg):
        bu = _dot(ub[:, s * cw:(s + 1) * cw], wbu_ref[s])
        br_ref[:, s * sw:(s + 1) * sw] = bu[:, :sw]
        bi_ref[:, s * sw:(s + 1) * sw] = bu[:, sw:]

    ns = br_ref.shape[1]
    lc = min(SCAN_LANES, ns)
    rid = lax.broadcasted_iota(jnp.int32, (SUBLANES, lc), 0)
    for c in range(ns // lc):
        cs = slice(c * lc, (c + 1) * lc)
        ar = ptab_ref[0, 0:1, cs]
        ai = ptab_ref[1, 0:1, cs]

        def local_step(j, hh, cs=cs, ar=ar, ai=ai):
            r0 = pl.multiple_of(j * SUBLANES, SUBLANES)
            nr, ni = _cmul_add(br_ref[pl.ds(r0, SUBLANES), cs], bi_ref[pl.ds(r0, SUBLANES), cs],
                               ar, ai, hh[0], hh[1])
            br_ref[pl.ds(r0, SUBLANES), cs] = nr
            bi_ref[pl.ds(r0, SUBLANES), cs] = ni
            return nr, ni

        zero = jnp.zeros((SUBLANES, lc), F32)
        er, ei = lax.fori_loop(0, seg, local_step, (zero, zero), unroll=SCAN_UNROLL)
        for j, sh in enumerate((1, 2, 4)):
            er, ei = _cmul_add(er, ei, coef_ref[2 * j, :, cs], coef_ref[2 * j + 1, :, cs],
                               pltpu.roll(er, sh, axis=0), pltpu.roll(ei, sh, axis=0))
        er, ei = _cmul_add(er, ei, coef_ref[6, :, cs], coef_ref[7, :, cs], cr_ref[:, cs], ci_ref[:, cs])
        inr = jnp.where(rid == 0, cr_ref[:, cs], pltpu.roll(er, 1, axis=0))
        ini = jnp.where(rid == 0, ci_ref[:, cs], pltpu.roll(ei, 1, axis=0))
        cr_ref[:, cs] = jnp.broadcast_to(er[SUBLANES - 1:, :], (SUBLANES, lc))
        ci_ref[:, cs] = jnp.broadcast_to(ei[SUBLANES - 1:, :], (SUBLANES, lc))

        def carry_step(j, cc, cs=cs, ar=ar, ai=ai):
            r0 = pl.multiple_of(j * SUBLANES, SUBLANES)
            nr = ar * cc[0] - ai * cc[1]
            ni = ar * cc[1] + ai * cc[0]
            br_ref[pl.ds(r0, SUBLANES), cs] = br_ref[pl.ds(r0, SUBLANES), cs] + nr
            bi_ref[pl.ds(r0, SUBLANES), cs] = bi_ref[pl.ds(r0, SUBLANES), cs] + ni
            return nr, ni

        lax.fori_loop(0, seg, carry_step, (inr, ini), unroll=SCAN_UNROLL)

    ys = []
    for s in range(nsg):
        hr = br_ref[:, s * sw:(s + 1) * sw].astype(BF16)
        hi = bi_ref[:, s * sw:(s + 1) * sw].astype(BF16)
        ys.append(_dot(hr, wcr_ref[s]) - _dot(hi, wci_ref[s]))
    yy = jnp.concatenate(ys, axis=1) + dsk_ref[...] * u
    z = _dot(perm_ref[1], jax.nn.gelu(yy).astype(BF16)).astype(BF16)
    out = _dot(z, wa_ref[...]) * jax.nn.sigmoid(_dot(z, wb_ref[...]))
    y_ref[0] = x + _rms(out, gpost_ref[...])
    sr_ref[0] = cr_ref[0:1, :]
    si_ref[0] = ci_ref[0:1, :]


def _s5_prompt(x, gpre, gpost, w_in, ptab, coef, w_bu, w_cr, w_ci, d_skip, wa, wb, h0r, h0i, tm):
    bsz, n, d = x.shape
    ns = coef.shape[-1]
    seg = tm // SUBLANES
    src = (jnp.arange(tm) % SUBLANES) * seg + jnp.arange(tm) // SUBLANES
    gather = (src[:, None] == jnp.arange(tm)[None, :])
    perm = jnp.stack([gather, gather.T]).astype(BF16)
    return pl.pallas_call(
        _s5_prompt_kernel,
        grid=(bsz, n // tm),
        in_specs=[
            pl.BlockSpec((1, tm, d), lambda b, t: (b, t, 0)),
            _const_spec((1, d)), _const_spec((1, d)), _const_spec(perm.shape), _const_spec((d, d)),
            _const_spec(ptab.shape), _const_spec(coef.shape),
            _const_spec(w_bu.shape), _const_spec(w_cr.shape), _const_spec(w_ci.shape),
            _const_spec((1, d)), _const_spec((d, d)), _const_spec((d, d)),
            pl.BlockSpec((1, 1, ns), lambda b, t: (b, 0, 0)),
            pl.BlockSpec((1, 1, ns), lambda b, t: (b, 0, 0)),
        ],
        out_specs=[
            pl.BlockSpec((1, tm, d), lambda b, t: (b, t, 0)),
            pl.BlockSpec((1, 1, ns), lambda b, t: (b, 0, 0)),
            pl.BlockSpec((1, 1, ns), lambda b, t: (b, 0, 0)),
        ],
        out_shape=[jax.ShapeDtypeStruct((bsz, n, d), F32),
                   jax.ShapeDtypeStruct((bsz, 1, ns), F32), jax.ShapeDtypeStruct((bsz, 1, ns), F32)],
        scratch_shapes=[pltpu.VMEM((tm, ns), F32), pltpu.VMEM((tm, ns), F32),
                        pltpu.VMEM((SUBLANES, ns), F32), pltpu.VMEM((SUBLANES, ns), F32)],
        compiler_params=_cparams("arbitrary", "arbitrary"),
        name="s5_prompt",
    )(x, gpre, gpost, perm, w_in, ptab, coef, w_bu, w_cr, w_ci, d_skip, wa, wb, h0r, h0i)


def _log_sigmoid(x):
    return -(jnp.maximum(-x, 0.0) + jnp.log1p(jnp.exp(-jnp.abs(x))))


def _split3(x):
    hi = x.astype(BF16).astype(F32)
    r = x - hi
    mid = r.astype(BF16).astype(F32)
    lo = r - mid
    return hi, mid, lo


def _fox_placement(nh):
    hd = LANES // 2
    pq = [[0.0] * (nh * LANES) for _ in range(LANES)]
    pk = [[0.0] * (nh * LANES) for _ in range(LANES)]
    one = 3 * nh
    for h in range(nh):
        base = h * LANES + hd
        for j in range(3):
            pq[j * nh + h][base + j] = 1.0
            pq[one][base + 3 + j] = 1.0
            pk[one][base + j] = 1.0
            pk[j * nh + h][base + 3 + j] = -1.0
    return jnp.array(pq, BF16), jnp.array(pk, BF16)


def _qkv_prompt_kernel(x_ref, gpre_ref, w_ref, bf_ref, pq_ref, pk_ref,
                       qa_ref, ka_ref, va_ref, k_ref, v_ref, lf_ref, carry_ref):
    t = pl.program_id(1)
    tm = x_ref.shape[1]
    d = x_ref.shape[-1]
    nh = qa_ref.shape[1]
    hd = d // nh

    @pl.when(t == 0)
    def _():
        carry_ref[...] = jnp.zeros_like(carry_ref)

    x = x_ref[0]
    h = _rms(x, gpre_ref[...]).astype(BF16)
    proj = _dot(h, w_ref[...])
    lane = lax.broadcasted_iota(jnp.int32, (tm, LANES), 1)
    logf = jnp.where(lane < nh, _log_sigmoid(proj[:, 3 * d:] + bf_ref[...]), 0.0)
    ri = lax.broadcasted_iota(jnp.int32, (tm, tm), 0)
    ci = lax.broadcasted_iota(jnp.int32, (tm, tm), 1)
    tri = (ci <= ri).astype(F32)
    cum = jnp.dot(tri, logf, preferred_element_type=F32, precision=lax.Precision.HIGHEST) + carry_ref[0:1, :]
    carry_ref[...] = jnp.broadcast_to(cum[tm - 1:, :], carry_ref.shape)
    hi, mid, lo = _split3(cum * LOG2E)
    src = hi + pltpu.roll(mid, nh, axis=1) + pltpu.roll(lo, 2 * nh, axis=1) + (lane == 3 * nh).astype(F32)
    src = src.astype(BF16)
    aug_q = _dot(src, pq_ref[...])
    aug_k = _dot(src, pk_ref[...])
    low = lane < hd
    one_at_hd = (lane == hd).astype(F32)
    for c in range(d // LANES):
        for o, dst in enumerate((qa_ref, ka_ref, va_ref)):
            blk = proj[:, o * d + c * LANES:o * d + (c + 1) * LANES]
            if o == 0:
                blk = blk * LOG2E
            for half in range(LANES // hd):
                hh = c * (LANES // hd) + half
                v = blk if half == 0 else pltpu.roll(blk, LANES - half * hd, axis=1)
                fill = (aug_q, aug_k)[o][:, hh * LANES:(hh + 1) * LANES] if o < 2 else one_at_hd
                dst[0, hh] = jnp.where(low, v, fill).astype(BF16)
    k_ref[0] = proj[:, d:2 * d]
    v_ref[0] = proj[:, 2 * d:3 * d]
    lf_ref[0] = logf[:, :nh]


def _qkv_prompt(x, gpre, w_all, bf_pad, pq, pk, nh, tm):
    bsz, n, d = x.shape
    head_spec = pl.BlockSpec((1, nh, tm, LANES), lambda b, t: (b, 0, t, 0))
    row_spec = pl.BlockSpec((1, tm, d), lambda b, t: (b, t, 0))
    head_shape = jax.ShapeDtypeStruct((bsz, nh, n, LANES), BF16)
    return pl.pallas_call(
        _qkv_prompt_kernel,
        grid=(bsz, n // tm),
        in_specs=[row_spec, _const_spec((1, d)), _const_spec(w_all.shape), _const_spec((1, LANES)),
                  _const_spec(pq.shape), _const_spec(pk.shape)],
        out_specs=[head_spec, head_spec, head_spec, row_spec, row_spec,
                   pl.BlockSpec((1, tm, nh), lambda b, t: (b, t, 0))],
        out_shape=[head_shape, head_shape, head_shape,
                   jax.ShapeDtypeStruct((bsz, n, d), F32), jax.ShapeDtypeStruct((bsz, n, d), F32),
                   jax.ShapeDtypeStruct((bsz, n, nh), F32)],
        scratch_shapes=[pltpu.VMEM((SUBLANES, LANES), F32)],
        compiler_params=_cparams("arbitrary", "arbitrary"),
        name="qkv_prompt",
    )(x, gpre, w_all, bf_pad, pq, pk)


def _attn_prompt_kernel(qi_ref, ki_ref, q_ref, k_ref, v_ref, o_ref, m_ref, acc_ref, *, hd):
    p = pl.program_id(2)
    qi = qi_ref[p]
    ki = ki_ref[p]
    tq = q_ref.shape[2]
    tk = k_ref.shape[2]
    ratio = tk // tq
    nt = (((1,), (1,)), ((), ()))

    @pl.when(ki == 0)
    def _():
        m_ref[...] = jnp.full_like(m_ref, NEG_BIG)
        acc_ref[...] = jnp.zeros_like(acc_ref)

    def step(cols, mask_off):
        for j in range(q_ref.shape[1]):
            s = lax.dot_general(q_ref[0, j], k_ref[0, j, :cols, :], nt, preferred_element_type=F32)
            if mask_off is not None:
                row = lax.broadcasted_iota(jnp.int32, (tq, cols), 0)
                col = lax.broadcasted_iota(jnp.int32, (tq, cols), 1)
                s = jnp.where(col <= row + mask_off, s, NEG_BIG)
            m_prev = m_ref[j]
            m_new = jnp.maximum(m_prev, jnp.max(s, axis=1, keepdims=True))
            pb = jnp.concatenate([jnp.exp2(s[:, c * LANES:(c + 1) * LANES] - m_new).astype(BF16)
                                  for c in range(cols // LANES)], axis=1)
            acc_ref[j] = jnp.exp2(m_prev - m_new) * acc_ref[j] + _dot(pb, v_ref[0, j, :cols, :])
            m_ref[j] = m_new

    @pl.when(ki < qi // ratio)
    def _():
        step(tk, None)

    for r in range(ratio):
        @pl.when((ki == qi // ratio) & (qi % ratio == r))
        def _():
            step((r + 1) * tq, r * tq)
            lane = lax.broadcasted_iota(jnp.int32, (tq, LANES), 1)
            out = None
            for j in range(q_ref.shape[1]):
                acc = acc_ref[j]
                oj = acc / jnp.sum(jnp.where(lane == hd, acc, 0.0), axis=1, keepdims=True)
                out = oj if j == 0 else jnp.where(lane < j * hd, out, pltpu.roll(oj, j * hd, axis=1))
            o_ref[0] = out


def _attn_prompt(qa, ka, va, hd, tq, tk):
    bsz, nh, n, _ = qa.shape
    hpb = LANES // hd
    nq = n // tq
    assert tk % tq == 0 and n % tk == 0
    pairs = [(q, k) for q in range(nq) for k in range(q // (tk // tq) + 1)]
    qi_tab = jnp.array([q for q, _ in pairs], jnp.int32)
    ki_tab = jnp.array([k for _, k in pairs], jnp.int32)
    grid_spec = pltpu.PrefetchScalarGridSpec(
        num_scalar_prefetch=2,
        grid=(bsz, nh // hpb, len(pairs)),
        in_specs=[
            pl.BlockSpec((1, hpb, tq, LANES), lambda b, h, p, qt, kt: (b, h, qt[p], 0)),
            pl.BlockSpec((1, hpb, tk, LANES), lambda b, h, p, qt, kt: (b, h, kt[p], 0)),
            pl.BlockSpec((1, hpb, tk, LANES), lambda b, h, p, qt, kt: (b, h, kt[p], 0)),
        ],
        out_specs=pl.BlockSpec((1, tq, LANES), lambda b, h, p, qt, kt: (b, qt[p], h)),
        scratch_shapes=[pltpu.VMEM((hpb, tq, LANES), F32), pltpu.VMEM((hpb, tq, LANES), F32)],
    )
    return pl.pallas_call(
        functools.partial(_attn_prompt_kernel, hd=hd),
        grid_spec=grid_spec,
        out_shape=jax.ShapeDtypeStruct((bsz, n, nh * hd), F32),
        compiler_params=_cparams("arbitrary", "arbitrary", "arbitrary"),
        name="attn_prompt",
    )(qi_tab, ki_tab, qa, ka, va)


def _oproj_kernel(o_ref, x_ref, w_ref, g_ref, y_ref):
    y = _dot(o_ref[...].astype(BF16), w_ref[...])
    y_ref[...] = x_ref[...] + _rms(y, g_ref[...])


def _oproj(o, x, w, g, tm):
    m, d = x.shape
    row = pl.BlockSpec((tm, d), lambda t: (t, 0))
    return pl.pallas_call(
        _oproj_kernel,
        grid=(m // tm,),
        in_specs=[row, row, _const_spec((d, d)), _const_spec((1, d))],
        out_specs=row,
        out_shape=jax.ShapeDtypeStruct((m, d), F32),
        compiler_params=_cparams("arbitrary"),
        name="oproj",
    )(o, x, w, g)


def _sconv_sample_kernel(x_ref, st_ref, gpre_ref, gpost_ref, win_ref, cw_ref, wout_ref, y_ref, ns_ref):
    d = x_ref.shape[-1]
    x = x_ref[...]
    h = _rms(x, gpre_ref[...]).astype(BF16)
    proj = _dot(h, win_ref[...])
    b = proj[:, :d]
    ch = proj[:, d:2 * d] * proj[:, 2 * d:]
    s0 = st_ref[0]
    s1 = st_ref[1]
    w = cw_ref[...]
    z = s0 * w[0:1, :] + s1 * w[1:2, :] + ch * w[2:3, :]
    y = _dot((b * z).astype(BF16), wout_ref[...])
    y_ref[...] = x + _rms(y, gpost_ref[...])
    ns_ref[0] = s1
    ns_ref[1] = ch


def _sconv_sample(x, st, gpre, gpost, w_in, conv_w, w_out):
    m, d = x.shape
    return pl.pallas_call(
        _sconv_sample_kernel,
        out_shape=[jax.ShapeDtypeStruct((m, d), F32), jax.ShapeDtypeStruct((CONV_W - 1, m, d), F32)],
        compiler_params=pltpu.CompilerParams(vmem_limit_bytes=VMEM_LIMIT_BYTES),
        name="sconv_sample",
    )(x, st, gpre, gpost, w_in, conv_w, w_out)


def _ffn_sample_kernel(x_ref, st_ref, gpre_ref, gpost_ref, wg_ref, wu_ref, cw_ref, wd_ref,
                       y_ref, ns_ref, acc_ref):
    c = pl.program_id(0)

    @pl.when(c == 0)
    def _():
        acc_ref[...] = jnp.zeros_like(acc_ref)

    x = x_ref[...]
    h = _rms(x, gpre_ref[...]).astype(BF16)
    g = _dot(h, wg_ref[...])
    u = _dot(h, wu_ref[...])
    s0 = st_ref[0]
    s1 = st_ref[1]
    w = cw_ref[...]
    gc = s0 * w[0:1, :] + s1 * w[1:2, :] + g * w[2:3, :]
    act = (gc * jax.nn.sigmoid(gc) * u).astype(BF16)
    acc_ref[...] += _dot(act, wd_ref[...])
    ns_ref[0] = s1
    ns_ref[1] = g

    @pl.when(c == pl.num_programs(0) - 1)
    def _():
        y_ref[...] = x + _rms(acc_ref[...], gpost_ref[...])


def _ffn_sample(x, st, gpre, gpost, wg, wu, conv_w, wd, fc):
    m, d = x.shape
    f = wg.shape[1]
    full = pl.BlockSpec((m, d), lambda c: (0, 0))
    vec = pl.BlockSpec((1, d), lambda c: (0, 0))
    return pl.pallas_call(
        _ffn_sample_kernel,
        grid=(f // fc,),
        in_specs=[full, pl.BlockSpec((CONV_W - 1, m, fc), lambda c: (0, 0, c)), vec, vec,
                  pl.BlockSpec((d, fc), lambda c: (0, c)), pl.BlockSpec((d, fc), lambda c: (0, c)),
                  pl.BlockSpec((CONV_W, fc), lambda c: (0, c)), pl.BlockSpec((fc, d), lambda c: (c, 0))],
        out_specs=[full, pl.BlockSpec((CONV_W - 1, m, fc), lambda c: (0, 0, c))],
        out_shape=[jax.ShapeDtypeStruct((m, d), F32), jax.ShapeDtypeStruct((CONV_W - 1, m, f), F32)],
        scratch_shapes=[pltpu.VMEM((m, d), F32)],
        compiler_params=_cparams("arbitrary"),
        name="ffn_sample",
    )(x, st, gpre, gpost, wg, wu, conv_w, wd)


def _s5_sample_kernel(x_ref, h0r_ref, h0i_ref, gpre_ref, gpost_ref, win_ref, ptab_ref, wbu_ref,
                      wcr_ref, wci_ref, dsk_ref, wa_ref, wb_ref, y_ref, sr_ref, si_ref):
    nsg = wbu_ref.shape[0]
    cw = wbu_ref.shape[1]
    sw = wbu_ref.shape[2] // 2
    x = x_ref[...]
    h = _rms(x, gpre_ref[...]).astype(BF16)
    u = _dot(h, win_ref[...])
    ub = u.astype(BF16)
    ar = ptab_ref[0, 0:1, :]
    ai = ptab_ref[1, 0:1, :]
    ys = []
    for s in range(nsg):
        cs = slice(s * sw, (s + 1) * sw)
        bu = _dot(ub[:, s * cw:(s + 1) * cw], wbu_ref[s])
        hr, hi = _cmul_add(bu[:, :sw], bu[:, sw:], ar[:, cs], ai[:, cs], h0r_ref[:, cs], h0i_ref[:, cs])
        sr_ref[:, cs] = hr
        si_ref[:, cs] = hi
        ys.append(_dot(hr.astype(BF16), wcr_ref[s]) - _dot(hi.astype(BF16), wci_ref[s]))
    yy = jnp.concatenate(ys, axis=1) + dsk_ref[...] * u
    z = jax.nn.gelu(yy).astype(BF16)
    out = _dot(z, wa_ref[...]) * jax.nn.sigmoid(_dot(z, wb_ref[...]))
    y_ref[...] = x + _rms(out, gpost_ref[...])


def _s5_sample(x, h0r, h0i, gpre, gpost, w_in, ptab, w_bu, w_cr, w_ci, d_skip, wa, wb):
    m, d = x.shape
    ns = h0r.shape[1]
    return pl.pallas_call(
        _s5_sample_kernel,
        out_shape=[jax.ShapeDtypeStruct((m, d), F32),
                   jax.ShapeDtypeStruct((m, ns), F32), jax.ShapeDtypeStruct((m, ns), F32)],
        compiler_params=pltpu.CompilerParams(vmem_limit_bytes=VMEM_LIMIT_BYTES),
        name="s5_sample",
    )(x, h0r, h0i, gpre, gpost, w_in, ptab, w_bu, w_cr, w_ci, d_skip, wa, wb)


def _qkv_sample_kernel(x_ref, gpre_ref, w_ref, bf_ref, q_ref, k_ref, v_ref, lf_ref):
    d = x_ref.shape[-1]
    h = _rms(x_ref[...], gpre_ref[...]).astype(BF16)
    proj = _dot(h, w_ref[...])
    q_ref[...] = proj[:, :d]
    k_ref[...] = proj[:, d:2 * d]
    v_ref[...] = proj[:, 2 * d:3 * d]
    lf_ref[...] = _log_sigmoid(proj[:, 3 * d:] + bf_ref[...])


def _qkv_sample(x, gpre, w_all, bf_pad):
    m, d = x.shape
    row = jax.ShapeDtypeStruct((m, d), F32)
    return pl.pallas_call(
        _qkv_sample_kernel,
        out_shape=[row, row, row, jax.ShapeDtypeStruct((m, LANES), F32)],
        compiler_params=pltpu.CompilerParams(vmem_limit_bytes=VMEM_LIMIT_BYTES),
        name="qkv_sample",
    )(x, gpre, w_all, bf_pad)


def _attn_sample_kernel(pt_ref, q_ref, kn_ref, vn_ref, lfn_ref, *rest, npar):
    del pt_ref
    kc, vc, lfc = rest[:npar], rest[npar:2 * npar], rest[2 * npar:3 * npar]
    o_ref, m_ref, l_ref, acc_ref, suf_ref = rest[3 * npar:]
    st = pl.program_id(1)
    nst = pl.num_programs(1)
    nh, hd, rows = kc[0].shape[1], kc[0].shape[2], kc[0].shape[3]
    d = nh * hd
    nt = (((1,), (1,)), ((), ()))
    rid = lax.broadcasted_iota(jnp.int32, (nh, d), 0)
    lid = lax.broadcasted_iota(jnp.int32, (nh, d), 1)
    own = (lid >= rid * hd) & (lid < (rid + 1) * hd)
    qbd = jnp.where(own, q_ref[0], 0.0).astype(BF16)

    @pl.when(st == 0)
    def _():
        kb = jnp.broadcast_to(kn_ref[0], (LANES, d)).astype(BF16)
        m_ref[...] = lax.dot_general(qbd, kb, nt, preferred_element_type=F32)
        l_ref[...] = jnp.ones_like(l_ref)
        acc_ref[...] = jnp.broadcast_to(vn_ref[0], (nh, d))
        suf_ref[...] = lfn_ref[0]

    lf_all = jnp.concatenate([r[0] for r in lfc], axis=0)
    ri = lax.broadcasted_iota(jnp.int32, (rows, 2 * rows), 0)
    ci = lax.broadcasted_iota(jnp.int32, (rows, 2 * rows), 1)
    later = ((ri > ci) | (ci >= rows)).astype(F32)
    sums = jnp.dot(lf_all, later, preferred_element_type=F32, precision=lax.Precision.HIGHEST)
    carry = suf_ref[...]
    scores = []
    for g in range(npar):
        bias = sums[g * nh:(g + 1) * nh, :rows] + carry
        carry = carry + sums[g * nh:(g + 1) * nh, rows:]
        scores.append(_dot(qbd, kc[g][0].reshape(d, rows).astype(BF16)) + bias)
    suf_ref[...] = carry
    s = jnp.concatenate(scores, axis=1)
    m_prev = m_ref[...]
    m_new = jnp.maximum(m_prev, jnp.max(s, axis=1, keepdims=True))
    pr = jnp.exp(s - jnp.concatenate([m_new] * (npar * rows // LANES), axis=1))
    alpha = jnp.exp(m_prev - m_new)
    l_ref[...] = alpha * l_ref[...] + jnp.sum(pr, axis=1, keepdims=True)
    pb = pr.astype(BF16)
    pv = None
    for g in range(npar):
        part = lax.dot_general(pb[:, g * rows:(g + 1) * rows], vc[g][0].reshape(d, rows).astype(BF16), nt,
                               preferred_element_type=F32)
        pv = part if pv is None else pv + part
    acc_ref[...] = jnp.concatenate([alpha] * (d // LANES), axis=1) * acc_ref[...] + pv
    m_ref[...] = m_new

    @pl.when(st == nst - 1)
    def _():
        out = acc_ref[...] / jnp.concatenate([l_ref[...]] * (d // LANES), axis=1)
        o_ref[0] = jnp.sum(jnp.where(own, out, 0.0), axis=0, keepdims=True)


def _attn_sample(page_table, q, k_new, v_new, lf_new, cache_kt, cache_vt, cache_lft, npar):
    m, _, d = q.shape
    npg = page_table.shape[1]
    _, nh, hd, rows = cache_kt.shape
    assert rows == LANES and npg % npar == 0

    def page_idx(g):
        return lambda b, s, pt: (pt[b, npg - 1 - (s * npar + g)], 0, 0, 0)

    def lf_idx(g):
        return lambda b, s, pt: (pt[b, npg - 1 - (s * npar + g)], 0, 0)

    tok = pl.BlockSpec((1, 1, d), lambda b, s, pt: (b, 0, 0))
    grid_spec = pltpu.PrefetchScalarGridSpec(
        num_scalar_prefetch=1,
        grid=(m, npg // npar),
        in_specs=([tok, tok, tok, pl.BlockSpec((1, nh, LANES), lambda b, s, pt: (b, 0, 0))]
                  + [pl.BlockSpec((1, nh, hd, rows), page_idx(g)) for g in range(npar)]
                  + [pl.BlockSpec((1, nh, hd, rows), page_idx(g)) for g in range(npar)]
                  + [pl.BlockSpec((1, nh, rows), lf_idx(g)) for g in range(npar)]),
        out_specs=tok,
        scratch_shapes=[pltpu.VMEM((nh, LANES), F32), pltpu.VMEM((nh, LANES), F32),
                        pltpu.VMEM((nh, d), F32), pltpu.VMEM((nh, LANES), F32)],
    )
    return pl.pallas_call(
        functools.partial(_attn_sample_kernel, npar=npar),
        grid_spec=grid_spec,
        out_shape=jax.ShapeDtypeStruct((m, 1, d), F32),
        compiler_params=_cparams("arbitrary", "arbitrary"),
        name="attn_sample",
    )(page_table, q, k_new, v_new, lf_new, *([cache_kt] * npar), *([cache_vt] * npar), *([cache_lft] * npar))


TM_MIX = 256
TM_FFN = 512
FC_FFN = 256
TQ_ATTN = 512
TK_ATTN = 1024
PAGES_PER_STEP = 8


def kernel(x_prompt, x_sample, state_sconv_l0, state_ssm_re_l1, state_ssm_im_l1, cache_k_l2, cache_v_l2, cache_logf_l2, state_sconv_l3, state_ffn_conv, page_table, sc_w_in_l0, sc_conv_w_l0, sc_w_out_l0, s5_w_in_l1, s5_lambda_re_l1, s5_lambda_im_l1, s5_log_dt_l1, s5_b_re_l1, s5_b_im_l1, s5_c_re_l1, s5_c_im_l1, s5_d_l1, s5_glu_wa_l1, s5_glu_wb_l1, fox_w_qkvf_l2, fox_b_f_l2, fox_w_o_l2, sc_w_in_l3, sc_conv_w_l3, sc_w_out_l3, norm_mix_pre, norm_mix_post, norm_ffn_pre, norm_ffn_post, ffn_w_gate, ffn_w_up, ffn_conv_w, ffn_w_down):
    bp, n, d = x_prompt.shape
    m = x_sample.shape[0]
    nh = fox_b_f_l2.shape[0]
    hd = d // nh
    f = ffn_w_gate.shape[-1]
    g, p = s5_lambda_re_l1.shape
    ns = g * p
    bf = lambda w: w.astype(BF16)
    row = lambda v: v.reshape(1, -1)

    tm_mix = min(TM_MIX, n)
    tm_ffn = min(TM_FFN, n)
    tq = min(TQ_ATTN, n)
    fc = min(FC_FFN, f)

    ptab, coef, w_bu, w_cr, w_ci = _s5_tables(s5_lambda_re_l1, s5_lambda_im_l1, s5_log_dt_l1,
                                              s5_b_re_l1, s5_b_im_l1, s5_c_re_l1, s5_c_im_l1,
                                              tm_mix // SUBLANES)
    scale = hd ** -0.5
    w_qkvf = jnp.concatenate([fox_w_qkvf_l2[:, :d] * scale, fox_w_qkvf_l2[:, d:],
                              jnp.zeros((d, LANES - nh), F32)], axis=1).astype(BF16)
    bf_pad = jnp.concatenate([fox_b_f_l2, jnp.zeros((LANES - nh,), F32)]).reshape(1, LANES)
    pq, pk = _fox_placement(nh)
    sc_params = {0: (bf(sc_w_in_l0), sc_conv_w_l0, bf(sc_w_out_l0)),
                 3: (bf(sc_w_in_l3), sc_conv_w_l3, bf(sc_w_out_l3))}
    s5_tail = (w_bu, w_cr, w_ci, row(s5_d_l1), bf(s5_glu_wa_l1), bf(s5_glu_wb_l1))
    s5_w_in = bf(s5_w_in_l1)
    w_o = bf(fox_w_o_l2)
    wg, wu, wd = bf(ffn_w_gate), bf(ffn_w_up), bf(ffn_w_down)

    xp = x_prompt
    ffn_p = []
    xp, sc0_p = _sconv_prompt(xp, row(norm_mix_pre[0]), row(norm_mix_post[0]), *sc_params[0], tm_ffn)
    xp, fb = _ffn_prompt(xp, row(norm_ffn_pre[0]), row(norm_ffn_post[0]), wg[0], wu[0], ffn_conv_w[0], wd[0], tm_ffn, fc)
    ffn_p.append(fb)
    zeros_state = jnp.zeros((bp, 1, ns), F32)
    xp, sr_p, si_p = _s5_prompt(xp, row(norm_mix_pre[1]), row(norm_mix_post[1]), s5_w_in, ptab, coef, *s5_tail,
                                zeros_state, zeros_state, tm_mix)
    xp, fb = _ffn_prompt(xp, row(norm_ffn_pre[1]), row(norm_ffn_post[1]), wg[1], wu[1], ffn_conv_w[1], wd[1], tm_ffn, fc)
    ffn_p.append(fb)
    qa, ka, va, k_p, v_p, lf_p = _qkv_prompt(xp, row(norm_mix_pre[2]), w_qkvf, bf_pad, pq, pk, nh, tm_mix)
    o_p = _attn_prompt(qa, ka, va, hd, tq, min(TK_ATTN, n))
    xp = _oproj(o_p.reshape(bp * n, d), xp.reshape(bp * n, d), w_o, row(norm_mix_post[2]),
                min(512, bp * n)).reshape(bp, n, d)
    xp, fb = _ffn_prompt(xp, row(norm_ffn_pre[2]), row(norm_ffn_post[2]), wg[2], wu[2], ffn_conv_w[2], wd[2], tm_ffn, fc)
    ffn_p.append(fb)
    xp, sc3_p = _sconv_prompt(xp, row(norm_mix_pre[3]), row(norm_mix_post[3]), *sc_params[3], tm_ffn)
    xp, fb = _ffn_prompt(xp, row(norm_ffn_pre[3]), row(norm_ffn_post[3]), wg[3], wu[3], ffn_conv_w[3], wd[3], tm_ffn, fc)
    ffn_p.append(fb)

    xs = x_sample.reshape(m, d)
    ffn_s = []
    st_ffn = jnp.swapaxes(state_ffn_conv, 1, 2)

    def ffn_s_layer(xs, i):
        y, ns_ = _ffn_sample(xs, st_ffn[i], row(norm_ffn_pre[i]), row(norm_ffn_post[i]),
                             wg[i], wu[i], ffn_conv_w[i], wd[i], fc)
        ffn_s.append(jnp.swapaxes(ns_, 0, 1))
        return y

    xs, sc0_s = _sconv_sample(xs, jnp.swapaxes(state_sconv_l0, 0, 1), row(norm_mix_pre[0]),
                              row(norm_mix_post[0]), *sc_params[0])
    xs = ffn_s_layer(xs, 0)
    xs, sr_s, si_s = _s5_sample(xs, state_ssm_re_l1.reshape(m, ns), state_ssm_im_l1.reshape(m, ns),
                                row(norm_mix_pre[1]), row(norm_mix_post[1]), s5_w_in, ptab, *s5_tail)
    xs = ffn_s_layer(xs, 1)
    q_s, k_s, v_s, lf_s = _qkv_sample(xs, row(norm_mix_pre[2]), w_qkvf, bf_pad)
    npg = page_table.shape[1]
    npar = max(c for c in range(1, PAGES_PER_STEP + 1) if npg % c == 0)
    o_s = _attn_sample(page_table, q_s.reshape(m, 1, d), k_s.reshape(m, 1, d), v_s.reshape(m, 1, d),
                       jnp.broadcast_to(lf_s[:, :nh, None], (m, nh, LANES)),
                       jnp.transpose(cache_k_l2, (0, 2, 3, 1)), jnp.transpose(cache_v_l2, (0, 2, 3, 1)),
                       jnp.transpose(cache_logf_l2, (0, 2, 1)), npar)
    xs = _oproj(o_s.reshape(m, d), xs, w_o, row(norm_mix_post[2]), m)
    xs = ffn_s_layer(xs, 2)
    xs, sc3_s = _sconv_sample(xs, jnp.swapaxes(state_sconv_l3, 0, 1), row(norm_mix_pre[3]),
                              row(norm_mix_post[3]), *sc_params[3])
    xs = ffn_s_layer(xs, 3)

    return (xp, xs.reshape(m, 1, d),
            sc0_p, jnp.swapaxes(sc0_s, 0, 1),
            sr_p.reshape(bp, g, p), sr_s.reshape(m, g, p), si_p.reshape(bp, g, p), si_s.reshape(m, g, p),
            k_p.reshape(bp, n, nh, hd), k_s.reshape(m, 1, nh, hd),
            v_p.reshape(bp, n, nh, hd), v_s.reshape(m, 1, nh, hd),
            lf_p, lf_s[:, :nh].reshape(m, 1, nh),
            sc3_p, jnp.swapaxes(sc3_s, 0, 1),
            jnp.stack(ffn_p), jnp.stack(ffn_s))
```

```python
import functools
import math

import jax
import jax.numpy as jnp
from jax import lax
from jax.experimental import pallas as pl
from jax.experimental.pallas import tpu as pltpu

F32 = jnp.float32
BF16 = jnp.bfloat16

RMS_EPS = 1e-6
CONV_W = 3
S5_GROUP = 16
S5_STATE = 64
FOX_HEADS = 16
PAGE_SIZE = 128

LANES = 128
SUBLANES = 8
MXU_DIM = 256
VMEM_LIMIT_BYTES = 56 * 1024 * 1024

NEG_BIG = -1e30
LOG2E = math.log2(math.e)
SCAN_UNROLL = True


def _cparams(*sem):
    return pltpu.CompilerParams(dimension_semantics=sem, vmem_limit_bytes=VMEM_LIMIT_BYTES)


def _const_spec(shape):
    nd = len(shape)
    return pl.BlockSpec(shape, lambda *_: (0,) * nd, pipeline_mode=pl.Buffered(1))


def _rms(x, g):
    ms = jnp.mean(x * x, axis=-1, keepdims=True)
    return x * lax.rsqrt(ms + RMS_EPS) * g


def _dot(a, b):
    return jnp.dot(a, b, preferred_element_type=F32)


def _shift_rows(cur, prev_tail, k):
    rolled = pltpu.roll(cur, k, axis=0)
    rid = lax.broadcasted_iota(jnp.int32, (SUBLANES, cur.shape[1]), 0)
    head = jnp.where(rid < k, pltpu.roll(prev_tail, k, axis=0), rolled[:SUBLANES, :])
    return jnp.concatenate([head, rolled[SUBLANES:, :]], axis=0)


def _causal_conv3(cur, prev_tail, w):
    x1 = _shift_rows(cur, prev_tail, 1)
    x2 = _shift_rows(cur, prev_tail, 2)
    return x2 * w[0:1, :] + x1 * w[1:2, :] + cur * w[2:3, :]


def _sconv_prompt_kernel(x_ref, gpre_ref, gpost_ref, win_ref, cw_ref, wout_ref,
                         y_ref, st_ref, tail_ref):
    t = pl.program_id(1)
    nt = pl.num_programs(1)
    d = x_ref.shape[-1]
    tm = x_ref.shape[1]

    @pl.when(t == 0)
    def _():
        tail_ref[...] = jnp.zeros_like(tail_ref)

    x = x_ref[0]
    h = _rms(x, gpre_ref[...]).astype(BF16)
    proj = _dot(h, win_ref[...])
    b = proj[:, :d]
    ch = proj[:, d:2 * d] * proj[:, 2 * d:]
    z = _causal_conv3(ch, tail_ref[...], cw_ref[...])
    y = _dot((b * z).astype(BF16), wout_ref[...])
    y_ref[0] = x + _rms(y, gpost_ref[...])
    tail_ref[...] = ch[tm - SUBLANES:, :]

    @pl.when(t == nt - 1)
    def _():
        st_ref[0] = ch[tm - (CONV_W - 1):, :]


def _sconv_prompt(x, gpre, gpost, w_in, conv_w, w_out, tm):
    bsz, n, d = x.shape
    return pl.pallas_call(
        _sconv_prompt_kernel,
        grid=(bsz, n // tm),
        in_specs=[
            pl.BlockSpec((1, tm, d), lambda b, t: (b, t, 0)),
            _const_spec((1, d)), _const_spec((1, d)),
            _const_spec((d, 3 * d)), _const_spec((CONV_W, d)), _const_spec((d, d)),
        ],
        out_specs=[
            pl.BlockSpec((1, tm, d), lambda b, t: (b, t, 0)),
            pl.BlockSpec((1, CONV_W - 1, d), lambda b, t: (b, 0, 0)),
        ],
        out_shape=[jax.ShapeDtypeStruct((bsz, n, d), F32),
                   jax.ShapeDtypeStruct((bsz, CONV_W - 1, d), F32)],
        scratch_shapes=[pltpu.VMEM((SUBLANES, d), F32)],
        compiler_params=_cparams("arbitrary", "arbitrary"),
        name="sconv_prompt",
    )(x, gpre, gpost, w_in, conv_w, w_out)


def _ffn_prompt_kernel(x_ref, gpre_ref, gpost_ref, wg_ref, wu_ref, cw_ref, wd_ref,
                       y_ref, st_ref, tail_ref, act_ref, *, fc):
    t = pl.program_id(1)
    nt = pl.num_programs(1)
    tm = x_ref.shape[1]
    f = wg_ref.shape[1]

    @pl.when(t == 0)
    def _():
        tail_ref[...] = jnp.zeros_like(tail_ref)

    x = x_ref[0]
    h = _rms(x, gpre_ref[...]).astype(BF16)
    for c in range(f // fc):
        cs = slice(c * fc, (c + 1) * fc)
        g = _dot(h, wg_ref[:, cs])
        u = _dot(h, wu_ref[:, cs])
        gc = _causal_conv3(g, tail_ref[:, cs], cw_ref[:, cs])
        act_ref[:, cs] = (gc * jax.nn.sigmoid(gc) * u).astype(BF16)
        tail_ref[:, cs] = g[tm - SUBLANES:, :]
    y = _dot(act_ref[...], wd_ref[...])
    y_ref[0] = x + _rms(y, gpost_ref[...])

    @pl.when(t == nt - 1)
    def _():
        st_ref[0] = tail_ref[SUBLANES - (CONV_W - 1):, :]


def _layer_spec(shape, layer):
    nd = len(shape)
    return pl.BlockSpec((None,) + tuple(shape), lambda *_: (layer,) + (0,) * nd, pipeline_mode=pl.Buffered(1))


def _ffn_prompt(x, layer, gpre, gpost, wg, wu, conv_w, wd, tm, fc):
    bsz, n, d = x.shape
    f = wg.shape[-1]
    return pl.pallas_call(
        functools.partial(_ffn_prompt_kernel, fc=fc),
        grid=(bsz, n // tm),
        in_specs=[
            pl.BlockSpec((1, tm, d), lambda b, t: (b, t, 0)),
            _layer_spec((1, d), layer), _layer_spec((1, d), layer),
            _layer_spec((d, f), layer), _layer_spec((d, f), layer),
            _layer_spec((CONV_W, f), layer), _layer_spec((f, d), layer),
        ],
        out_specs=[
            pl.BlockSpec((1, tm, d), lambda b, t: (b, t, 0)),
            pl.BlockSpec((1, CONV_W - 1, f), lambda b, t: (b, 0, 0)),
        ],
        out_shape=[jax.ShapeDtypeStruct((bsz, n, d), F32),
                   jax.ShapeDtypeStruct((bsz, CONV_W - 1, f), F32)],
        scratch_shapes=[pltpu.VMEM((SUBLANES, f), F32), pltpu.VMEM((tm, f), BF16)],
        compiler_params=_cparams("arbitrary", "arbitrary"),
        name="ffn_prompt",
    )(x, gpre, gpost, wg, wu, conv_w, wd)


def _s5_prep_kernel(lr_ref, li_ref, ldt_ref, bre_ref, bim_ref,
                    pwr_ref, pwi_ref, sgr_ref, sgi_ref, bbr_ref, bbi_ref):
    lr = lr_ref[...]
    li = li_ref[...]
    dt = jnp.exp(ldt_ref[...])
    mag = jnp.exp(lr * dt)
    abar_r = mag * jnp.cos(li * dt)
    abar_i = mag * jnp.sin(li * dt)

    def powers(ref_r, ref_i, br, bi):
        cr, ci = br, bi
        for k in range(ref_r.shape[0]):
            ref_r[k] = cr
            ref_i[k] = ci
            cr, ci = cr * br - ci * bi, cr * bi + ci * br

    seg = pwr_ref.shape[0]
    powers(pwr_ref, pwi_ref, abar_r, abar_i)
    powers(sgr_ref, sgi_ref, pwr_ref[seg - 1], pwi_ref[seg - 1])
    den = lr * lr + li * li
    nr = abar_r - 1.0
    kr = (nr * lr + abar_i * li) / den
    ki = (abar_i * lr - nr * li) / den
    bre = bre_ref[...]
    bim = bim_ref[...]
    bbr_ref[...] = kr[:, None, :] * bre - ki[:, None, :] * bim
    bbi_ref[...] = kr[:, None, :] * bim + ki[:, None, :] * bre


def _s5_prep(lam_re, lam_im, log_dt, b_re_t, b_im_t, seg):
    g, p = lam_re.shape
    n = b_re_t.shape[1]
    return pl.pallas_call(
        _s5_prep_kernel,
        out_shape=[jax.ShapeDtypeStruct((seg, g, p), F32), jax.ShapeDtypeStruct((seg, g, p), F32),
                   jax.ShapeDtypeStruct((SUBLANES, g, p), F32), jax.ShapeDtypeStruct((SUBLANES, g, p), F32),
                   jax.ShapeDtypeStruct((g, n, p), F32), jax.ShapeDtypeStruct((g, n, p), F32)],
        name="s5_prep",
    )(lam_re, lam_im, log_dt.reshape(g, 1), b_re_t, b_im_t)


def _s5_block_diag(w, sgroups):
    g, a, b = w.shape
    w4 = w.reshape(g // sgroups, sgroups, a, b)
    eye = jnp.eye(sgroups, dtype=w.dtype)
    return jnp.einsum('sgab,gh->sgahb', w4, eye).reshape(g // sgroups, sgroups * a, sgroups * b)


def _s5_tables(lam_re, lam_im, log_dt, b_re, b_im, c_re, c_im, seg):
    g, p = lam_re.shape
    pwr, pwi, sgr, sgi, bbr, bbi = _s5_prep(lam_re, lam_im, log_dt,
                                            jnp.swapaxes(b_re, 1, 2), jnp.swapaxes(b_im, 1, 2), seg)
    sg = MXU_DIM // S5_GROUP
    w_bu = jnp.concatenate([_s5_block_diag(bbr, sg), _s5_block_diag(bbi, sg)], axis=-1).astype(BF16)
    w_cr = _s5_block_diag(jnp.swapaxes(c_re, 1, 2), sg).astype(BF16)
    w_ci = _s5_block_diag(jnp.swapaxes(c_im, 1, 2), sg).astype(BF16)
    ptab = jnp.stack([pwr.reshape(seg, g * p)[:1], pwi.reshape(seg, g * p)[:1]])
    sgr = sgr.reshape(SUBLANES, g * p)
    sgi = sgi.reshape(SUBLANES, g * p)
    rows = jnp.arange(SUBLANES)[:, None]
    steps = []
    for s in (1, 2, 4):
        steps.append(jnp.where(rows >= s, sgr[s - 1][None, :], 0.0))
        steps.append(jnp.where(rows >= s, sgi[s - 1][None, :], 0.0))
    coef = jnp.stack(steps + [sgr, sgi])
    return ptab, coef, w_bu, w_cr, w_ci


def _cmul_add(xr, xi, ar, ai, sr, si):
    return xr + ar * sr - ai * si, xi + ar * si + ai * sr


def _s5_prompt_kernel(x_ref, gpre_ref, gpost_ref, perm_ref, win_ref, ptab_ref, coef_ref, wbu_ref, wcr_ref,
                      wci_ref, dsk_ref, wa_ref, wb_ref, h0r_ref, h0i_ref,
                      y_ref, sr_ref, si_ref, br_ref, bi_ref, cr_ref, ci_ref):
    t = pl.program_id(1)
    tm = x_ref.shape[1]
    seg = tm // SUBLANES
    nsg = wbu_ref.shape[0]
    cw = wbu_ref.shape[1]
    sw = wbu_ref.shape[2] // 2

    @pl.when(t == 0)
    def _():
        cr_ref[...] = jnp.broadcast_to(h0r_ref[0], cr_ref.shape)
        ci_ref[...] = jnp.broadcast_to(h0i_ref[0], ci_ref.shape)

    x = x_ref[0]
    h = _dot(perm_ref[0], _rms(x, gpre_ref[...]).astype(BF16)).astype(BF16)
    u = _dot(h, win_ref[...])
    ub = u.astype(BF16)
    lc = sw
    for c in range(nsg):
        bu = _dot(ub[:, c * cw:(c + 1) * cw], wbu_ref[c])
        br_ref[:, c * lc:(c + 1) * lc] = bu[:, :sw]
        bi_ref[:, c * lc:(c + 1) * lc] = bu[:, sw:]

    rid = lax.broadcasted_iota(jnp.int32, (SUBLANES, lc), 0)
    for c in range(nsg):
        cs = slice(c * lc, (c + 1) * lc)
        ar = ptab_ref[0, 0:1, cs]
        ai = ptab_ref[1, 0:1, cs]

        def local_step(j, hh, cs=cs, ar=ar, ai=ai):
            r0 = pl.multiple_of(j * SUBLANES, SUBLANES)
            nr, ni = _cmul_add(br_ref[pl.ds(r0, SUBLANES), cs], bi_ref[pl.ds(r0, SUBLANES), cs],
                               ar, ai, hh[0], hh[1])
            br_ref[pl.ds(r0, SUBLANES), cs] = nr
            bi_ref[pl.ds(r0, SUBLANES), cs] = ni
            return nr, ni

        zero = jnp.zeros((SUBLANES, lc), F32)
        er, ei = lax.fori_loop(0, seg, local_step, (zero, zero), unroll=SCAN_UNROLL)
        for j, sh in enumerate((1, 2, 4)):
            er, ei = _cmul_add(er, ei, coef_ref[2 * j, :, cs], coef_ref[2 * j + 1, :, cs],
                               pltpu.roll(er, sh, axis=0), pltpu.roll(ei, sh, axis=0))
        er, ei = _cmul_add(er, ei, coef_ref[6, :, cs], coef_ref[7, :, cs], cr_ref[:, cs], ci_ref[:, cs])
        inr = jnp.where(rid == 0, cr_ref[:, cs], pltpu.roll(er, 1, axis=0))
        ini = jnp.where(rid == 0, ci_ref[:, cs], pltpu.roll(ei, 1, axis=0))
        cr_ref[:, cs] = jnp.broadcast_to(er[SUBLANES - 1:, :], (SUBLANES, lc))
        ci_ref[:, cs] = jnp.broadcast_to(ei[SUBLANES - 1:, :], (SUBLANES, lc))

        def carry_step(j, cc, cs=cs, ar=ar, ai=ai):
            r0 = pl.multiple_of(j * SUBLANES, SUBLANES)
            nr = ar * cc[0] - ai * cc[1]
            ni = ar * cc[1] + ai * cc[0]
            br_ref[pl.ds(r0, SUBLANES), cs] = br_ref[pl.ds(r0, SUBLANES), cs] + nr
            bi_ref[pl.ds(r0, SUBLANES), cs] = bi_ref[pl.ds(r0, SUBLANES), cs] + ni
            return nr, ni

        lax.fori_loop(0, seg, carry_step, (inr, ini), unroll=SCAN_UNROLL)

    ys = []
    for c in range(nsg):
        cs = slice(c * lc, (c + 1) * lc)
        ys.append(_dot(br_ref[:, cs].astype(BF16), wcr_ref[c]) - _dot(bi_ref[:, cs].astype(BF16), wci_ref[c]))
    yy = jnp.concatenate(ys, axis=1) + dsk_ref[...] * u
    z = _dot(perm_ref[1], jax.nn.gelu(yy).astype(BF16)).astype(BF16)
    out = _dot(z, wa_ref[...]) * jax.nn.sigmoid(_dot(z, wb_ref[...]))
    y_ref[0] = x + _rms(out, gpost_ref[...])
    sr_ref[0] = cr_ref[0:1, :]
    si_ref[0] = ci_ref[0:1, :]


def _s5_prompt(x, gpre, gpost, w_in, ptab, coef, w_bu, w_cr, w_ci, d_skip, wa, wb, h0r, h0i, tm):
    bsz, n, d = x.shape
    ns = coef.shape[-1]
    seg = tm // SUBLANES
    src = (jnp.arange(tm) % SUBLANES) * seg + jnp.arange(tm) // SUBLANES
    gather = (src[:, None] == jnp.arange(tm)[None, :])
    perm = jnp.stack([gather, gather.T]).astype(BF16)
    return pl.pallas_call(
        _s5_prompt_kernel,
        grid=(bsz, n // tm),
        in_specs=[
            pl.BlockSpec((1, tm, d), lambda b, t: (b, t, 0)),
            _const_spec((1, d)), _const_spec((1, d)), _const_spec(perm.shape), _const_spec((d, d)),
            _const_spec(ptab.shape), _const_spec(coef.shape),
            _const_spec(w_bu.shape), _const_spec(w_cr.shape), _const_spec(w_ci.shape),
            _const_spec((1, d)), _const_spec((d, d)), _const_spec((d, d)),
            pl.BlockSpec((1, 1, ns), lambda b, t: (b, 0, 0)),
            pl.BlockSpec((1, 1, ns), lambda b, t: (b, 0, 0)),
        ],
        out_specs=[
            pl.BlockSpec((1, tm, d), lambda b, t: (b, t, 0)),
            pl.BlockSpec((1, 1, ns), lambda b, t: (b, 0, 0)),
            pl.BlockSpec((1, 1, ns), lambda b, t: (b, 0, 0)),
        ],
        out_shape=[jax.ShapeDtypeStruct((bsz, n, d), F32),
                   jax.ShapeDtypeStruct((bsz, 1, ns), F32), jax.ShapeDtypeStruct((bsz, 1, ns), F32)],
        scratch_shapes=[pltpu.VMEM((tm, ns), F32), pltpu.VMEM((tm, ns), F32),
                        pltpu.VMEM((SUBLANES, ns), F32), pltpu.VMEM((SUBLANES, ns), F32)],
        compiler_params=_cparams("arbitrary", "arbitrary"),
        name="s5_prompt",
    )(x, gpre, gpost, perm, w_in, ptab, coef, w_bu, w_cr, w_ci, d_skip, wa, wb, h0r, h0i)


def _log_sigmoid(x):
    return -(jnp.maximum(-x, 0.0) + jnp.log1p(jnp.exp(-jnp.abs(x))))


def _split3(x):
    hi = x.astype(BF16).astype(F32)
    r = x - hi
    mid = r.astype(BF16).astype(F32)
    lo = r - mid
    return hi, mid, lo


def _fox_placement(nh):
    hd = LANES // 2
    pq = [[0.0] * (nh * LANES) for _ in range(LANES)]
    pk = [[0.0] * (nh * LANES) for _ in range(LANES)]
    one = 3 * nh
    for h in range(nh):
        base = h * LANES + hd
        for j in range(3):
            pq[j * nh + h][base + j] = 1.0
            pq[one][base + 3 + j] = 1.0
            pk[one][base + j] = 1.0
            pk[j * nh + h][base + 3 + j] = -1.0
    return jnp.array(pq, BF16), jnp.array(pk, BF16)


def _qkv_prompt_kernel(x_ref, gpre_ref, w_ref, bf_ref, pq_ref, pk_ref,
                       qa_ref, ka_ref, va_ref, k_ref, v_ref, lf_ref, carry_ref):
    t = pl.program_id(1)
    tm = x_ref.shape[1]
    d = x_ref.shape[-1]
    nh = qa_ref.shape[1]
    hd = d // nh

    @pl.when(t == 0)
    def _():
        carry_ref[...] = jnp.zeros_like(carry_ref)

    x = x_ref[0]
    h = _rms(x, gpre_ref[...]).astype(BF16)
    proj = _dot(h, w_ref[...])
    lane = lax.broadcasted_iota(jnp.int32, (tm, LANES), 1)
    logf = jnp.where(lane < nh, _log_sigmoid(proj[:, 3 * d:] + bf_ref[...]), 0.0)
    ri = lax.broadcasted_iota(jnp.int32, (tm, tm), 0)
    ci = lax.broadcasted_iota(jnp.int32, (tm, tm), 1)
    tri = (ci <= ri).astype(F32)
    cum = jnp.dot(tri, logf, preferred_element_type=F32, precision=lax.Precision.HIGHEST) + carry_ref[0:1, :]
    carry_ref[...] = jnp.broadcast_to(cum[tm - 1:, :], carry_ref.shape)
    hi, mid, lo = _split3(cum * LOG2E)
    src = hi + pltpu.roll(mid, nh, axis=1) + pltpu.roll(lo, 2 * nh, axis=1) + (lane == 3 * nh).astype(F32)
    src = src.astype(BF16)
    aug_q = _dot(src, pq_ref[...])
    aug_k = _dot(src, pk_ref[...])
    low = lane < hd
    one_at_hd = (lane == hd).astype(F32)
    for c in range(d // LANES):
        for o, dst in enumerate((qa_ref, ka_ref, va_ref)):
            blk = proj[:, o * d + c * LANES:o * d + (c + 1) * LANES]
            if o == 0:
                blk = blk * LOG2E
            for half in range(LANES // hd):
                hh = c * (LANES // hd) + half
                v = blk if half == 0 else pltpu.roll(blk, LANES - half * hd, axis=1)
                fill = (aug_q, aug_k)[o][:, hh * LANES:(hh + 1) * LANES] if o < 2 else one_at_hd
                dst[0, hh] = jnp.where(low, v, fill).astype(BF16)
    k_ref[0] = proj[:, d:2 * d]
    v_ref[0] = proj[:, 2 * d:3 * d]
    lf_ref[0] = logf[:, :nh]


def _qkv_prompt(x, gpre, w_all, bf_pad, pq, pk, nh, tm):
    bsz, n, d = x.shape
    head_spec = pl.BlockSpec((1, nh, tm, LANES), lambda b, t: (b, 0, t, 0))
    row_spec = pl.BlockSpec((1, tm, d), lambda b, t: (b, t, 0))
    head_shape = jax.ShapeDtypeStruct((bsz, nh, n, LANES), BF16)
    return pl.pallas_call(
        _qkv_prompt_kernel,
        grid=(bsz, n // tm),
        in_specs=[row_spec, _const_spec((1, d)), _const_spec(w_all.shape), _const_spec((1, LANES)),
                  _const_spec(pq.shape), _const_spec(pk.shape)],
        out_specs=[head_spec, head_spec, head_spec, row_spec, row_spec,
                   pl.BlockSpec((1, tm, nh), lambda b, t: (b, t, 0))],
        out_shape=[head_shape, head_shape, head_shape,
                   jax.ShapeDtypeStruct((bsz, n, d), F32), jax.ShapeDtypeStruct((bsz, n, d), F32),
                   jax.ShapeDtypeStruct((bsz, n, nh), F32)],
        scratch_shapes=[pltpu.VMEM((SUBLANES, LANES), F32)],
        compiler_params=_cparams("arbitrary", "arbitrary"),
        name="qkv_prompt",
    )(x, gpre, w_all, bf_pad, pq, pk)


def _attn_prompt_kernel(q_ref, k_ref, v_ref, o_ref, m_ref, acc_ref, *, hd, tk):
    qi = pl.program_id(2)
    tq = q_ref.shape[2]
    ratio = tk // tq
    nt = (((1,), (1,)), ((), ()))
    m_ref[...] = jnp.full_like(m_ref, NEG_BIG)
    acc_ref[...] = jnp.zeros_like(acc_ref)

    def step(start, cols, mask_off):
        for j in range(q_ref.shape[1]):
            s = lax.dot_general(q_ref[0, j], k_ref[0, j, pl.ds(start, cols), :], nt,
                                preferred_element_type=F32)
            if mask_off is not None:
                row = lax.broadcasted_iota(jnp.int32, (tq, cols), 0)
                col = lax.broadcasted_iota(jnp.int32, (tq, cols), 1)
                s = jnp.where(col <= row + mask_off, s, NEG_BIG)
            m_prev = m_ref[j]
            m_new = jnp.maximum(m_prev, jnp.max(s, axis=1, keepdims=True))
            pb = jnp.concatenate([jnp.exp2(s[:, c * LANES:(c + 1) * LANES] - m_new).astype(BF16)
                                  for c in range(cols // LANES)], axis=1)
            acc_ref[j] = (jnp.exp2(m_prev - m_new) * acc_ref[j]
                          + _dot(pb, v_ref[0, j, pl.ds(start, cols), :]))
            m_ref[j] = m_new

    nfull = qi // ratio

    def full_blocks(i, carry):
        for u in range(2):
            step(pl.multiple_of((2 * i + u) * tk, tk), tk, None)
        return carry

    lax.fori_loop(0, nfull // 2, full_blocks, 0)

    @pl.when(nfull % 2 == 1)
    def _():
        step(pl.multiple_of((nfull - 1) * tk, tk), tk, None)

    diag_start = pl.multiple_of(nfull * tk, tk)
    for r in range(ratio):
        @pl.when(qi % ratio == r)
        def _():
            step(diag_start, (r + 1) * tq, r * tq)

    lane = lax.broadcasted_iota(jnp.int32, (tq, LANES), 1)
    out = None
    for j in range(q_ref.shape[1]):
        acc = acc_ref[j]
        oj = acc / jnp.sum(jnp.where(lane == hd, acc, 0.0), axis=1, keepdims=True)
        out = oj if j == 0 else jnp.where(lane < j * hd, out, pltpu.roll(oj, j * hd, axis=1))
    o_ref[0] = out


def _attn_prompt(qa, ka, va, hd, tq, tk):
    bsz, nh, n, _ = qa.shape
    hpb = LANES // hd
    assert tk % tq == 0 and n % tk == 0
    seq_spec = pl.BlockSpec((1, hpb, n, LANES), lambda b, h, q: (b, h, 0, 0))
    return pl.pallas_call(
        functools.partial(_attn_prompt_kernel, hd=hd, tk=tk),
        grid=(bsz, nh // hpb, n // tq),
        in_specs=[pl.BlockSpec((1, hpb, tq, LANES), lambda b, h, q: (b, h, q, 0)), seq_spec, seq_spec],
        out_specs=pl.BlockSpec((1, tq, LANES), lambda b, h, q: (b, q, h)),
        out_shape=jax.ShapeDtypeStruct((bsz, n, nh * hd), F32),
        scratch_shapes=[pltpu.VMEM((hpb, tq, LANES), F32), pltpu.VMEM((hpb, tq, LANES), F32)],
        compiler_params=_cparams("arbitrary", "arbitrary", "arbitrary"),
        name="attn_prompt",
    )(qa, ka, va)


def _oproj_kernel(o_ref, x_ref, w_ref, g_ref, y_ref):
    y = _dot(o_ref[...].astype(BF16), w_ref[...])
    y_ref[...] = x_ref[...] + _rms(y, g_ref[...])


def _oproj(o, x, w, g, tm):
    m, d = x.shape
    row = pl.BlockSpec((tm, d), lambda t: (t, 0))
    return pl.pallas_call(
        _oproj_kernel,
        grid=(m // tm,),
        in_specs=[row, row, _const_spec((d, d)), _const_spec((1, d))],
        out_specs=row,
        out_shape=jax.ShapeDtypeStruct((m, d), F32),
        compiler_params=_cparams("arbitrary"),
        name="oproj",
    )(o, x, w, g)


def _sconv_sample_kernel(x_ref, st_ref, gpre_ref, gpost_ref, win_ref, cw_ref, wout_ref, y_ref, ns_ref):
    d = x_ref.shape[-1]
    x = x_ref[...]
    h = _rms(x, gpre_ref[...]).astype(BF16)
    proj = _dot(h, win_ref[...])
    b = proj[:, :d]
    ch = proj[:, d:2 * d] * proj[:, 2 * d:]
    s0 = st_ref[0]
    s1 = st_ref[1]
    w = cw_ref[...]
    z = s0 * w[0:1, :] + s1 * w[1:2, :] + ch * w[2:3, :]
    y = _dot((b * z).astype(BF16), wout_ref[...])
    y_ref[...] = x + _rms(y, gpost_ref[...])
    ns_ref[0] = s1
    ns_ref[1] = ch


def _sconv_sample(x, st, gpre, gpost, w_in, conv_w, w_out):
    m, d = x.shape
    return pl.pallas_call(
        _sconv_sample_kernel,
        out_shape=[jax.ShapeDtypeStruct((m, d), F32), jax.ShapeDtypeStruct((CONV_W - 1, m, d), F32)],
        compiler_params=pltpu.CompilerParams(vmem_limit_bytes=VMEM_LIMIT_BYTES),
        name="sconv_sample",
    )(x, st, gpre, gpost, w_in, conv_w, w_out)


def _ffn_sample_kernel(x_ref, st_ref, gpre_ref, gpost_ref, wg_ref, wu_ref, cw_ref, wd_ref,
                       y_ref, ns_ref, acc_ref):
    c = pl.program_id(0)

    @pl.when(c == 0)
    def _():
        acc_ref[...] = jnp.zeros_like(acc_ref)

    x = x_ref[...]
    h = _rms(x, gpre_ref[...]).astype(BF16)
    g = _dot(h, wg_ref[...])
    u = _dot(h, wu_ref[...])
    s0 = st_ref[0]
    s1 = st_ref[1]
    w = cw_ref[...]
    gc = s0 * w[0:1, :] + s1 * w[1:2, :] + g * w[2:3, :]
    act = (gc * jax.nn.sigmoid(gc) * u).astype(BF16)
    acc_ref[...] += _dot(act, wd_ref[...])
    ns_ref[0] = s1
    ns_ref[1] = g

    @pl.when(c == pl.num_programs(0) - 1)
    def _():
        y_ref[...] = x + _rms(acc_ref[...], gpost_ref[...])


def _ffn_sample(x, layer, st, gpre, gpost, wg, wu, conv_w, wd, fc):
    m, d = x.shape
    f = wg.shape[-1]
    full = pl.BlockSpec((m, d), lambda c: (0, 0))
    vec = pl.BlockSpec((None, 1, d), lambda c: (layer, 0, 0))
    return pl.pallas_call(
        _ffn_sample_kernel,
        grid=(f // fc,),
        in_specs=[full, pl.BlockSpec((None, CONV_W - 1, m, fc), lambda c: (layer, 0, 0, c)), vec, vec,
                  pl.BlockSpec((None, d, fc), lambda c: (layer, 0, c)),
                  pl.BlockSpec((None, d, fc), lambda c: (layer, 0, c)),
                  pl.BlockSpec((None, CONV_W, fc), lambda c: (layer, 0, c)),
                  pl.BlockSpec((None, fc, d), lambda c: (layer, c, 0))],
        out_specs=[full, pl.BlockSpec((CONV_W - 1, m, fc), lambda c: (0, 0, c))],
        out_shape=[jax.ShapeDtypeStruct((m, d), F32), jax.ShapeDtypeStruct((CONV_W - 1, m, f), F32)],
        scratch_shapes=[pltpu.VMEM((m, d), F32)],
        compiler_params=_cparams("arbitrary"),
        name="ffn_sample",
    )(x, st, gpre, gpost, wg, wu, conv_w, wd)


def _s5_sample_kernel(x_ref, h0r_ref, h0i_ref, gpre_ref, gpost_ref, win_ref, ptab_ref, wbu_ref,
                      wcr_ref, wci_ref, dsk_ref, wa_ref, wb_ref, y_ref, sr_ref, si_ref):
    nsg = wbu_ref.shape[0]
    cw = wbu_ref.shape[1]
    sw = wbu_ref.shape[2] // 2
    x = x_ref[...]
    h = _rms(x, gpre_ref[...]).astype(BF16)
    u = _dot(h, win_ref[...])
    ub = u.astype(BF16)
    ar = ptab_ref[0, 0:1, :]
    ai = ptab_ref[1, 0:1, :]
    ys = []
    for s in range(nsg):
        cs = slice(s * sw, (s + 1) * sw)
        bu = _dot(ub[:, s * cw:(s + 1) * cw], wbu_ref[s])
        hr, hi = _cmul_add(bu[:, :sw], bu[:, sw:], ar[:, cs], ai[:, cs], h0r_ref[:, cs], h0i_ref[:, cs])
        sr_ref[:, cs] = hr
        si_ref[:, cs] = hi
        ys.append(_dot(hr.astype(BF16), wcr_ref[s]) - _dot(hi.astype(BF16), wci_ref[s]))
    yy = jnp.concatenate(ys, axis=1) + dsk_ref[...] * u
    z = jax.nn.gelu(yy).astype(BF16)
    out = _dot(z, wa_ref[...]) * jax.nn.sigmoid(_dot(z, wb_ref[...]))
    y_ref[...] = x + _rms(out, gpost_ref[...])


def _s5_sample(x, h0r, h0i, gpre, gpost, w_in, ptab, w_bu, w_cr, w_ci, d_skip, wa, wb):
    m, d = x.shape
    ns = h0r.shape[1]
    return pl.pallas_call(
        _s5_sample_kernel,
        out_shape=[jax.ShapeDtypeStruct((m, d), F32),
                   jax.ShapeDtypeStruct((m, ns), F32), jax.ShapeDtypeStruct((m, ns), F32)],
        compiler_params=pltpu.CompilerParams(vmem_limit_bytes=VMEM_LIMIT_BYTES),
        name="s5_sample",
    )(x, h0r, h0i, gpre, gpost, w_in, ptab, w_bu, w_cr, w_ci, d_skip, wa, wb)


def _qkv_sample_kernel(x_ref, gpre_ref, w_ref, bf_ref, q_ref, k_ref, v_ref, lf_ref):
    d = x_ref.shape[-1]
    h = _rms(x_ref[...], gpre_ref[...]).astype(BF16)
    proj = _dot(h, w_ref[...])
    q_ref[...] = proj[:, :d]
    k_ref[...] = proj[:, d:2 * d]
    v_ref[...] = proj[:, 2 * d:3 * d]
    lf_ref[...] = _log_sigmoid(proj[:, 3 * d:] + bf_ref[...])


def _qkv_sample(x, gpre, w_all, bf_pad):
    m, d = x.shape
    row = jax.ShapeDtypeStruct((m, d), F32)
    return pl.pallas_call(
        _qkv_sample_kernel,
        out_shape=[row, row, row, jax.ShapeDtypeStruct((m, LANES), F32)],
        compiler_params=pltpu.CompilerParams(vmem_limit_bytes=VMEM_LIMIT_BYTES),
        name="qkv_sample",
    )(x, gpre, w_all, bf_pad)


def _attn_sample_kernel(pt_ref, q_ref, kn_ref, vn_ref, lfn_ref, *rest, npar):
    del pt_ref
    kc, vc, lfc = rest[:npar], rest[npar:2 * npar], rest[2 * npar:3 * npar]
    o_ref, m_ref, l_ref, acc_ref, suf_ref = rest[3 * npar:]
    st = pl.program_id(1)
    nst = pl.num_programs(1)
    nh, hd, rows = kc[0].shape[1], kc[0].shape[2], kc[0].shape[3]
    d = nh * hd
    nt = (((1,), (1,)), ((), ()))
    rid = lax.broadcasted_iota(jnp.int32, (nh, d), 0)
    lid = lax.broadcasted_iota(jnp.int32, (nh, d), 1)
    own = (lid >= rid * hd) & (lid < (rid + 1) * hd)
    qbd = jnp.where(own, q_ref[0], 0.0).astype(BF16)

    @pl.when(st == 0)
    def _():
        kb = jnp.broadcast_to(kn_ref[0], (LANES, d)).astype(BF16)
        m_ref[...] = lax.dot_general(qbd, kb, nt, preferred_element_type=F32)
        l_ref[...] = jnp.ones_like(l_ref)
        acc_ref[...] = jnp.broadcast_to(vn_ref[0], (nh, d))
        suf_ref[...] = lfn_ref[0]

    lf_all = jnp.concatenate([r[0] for r in lfc], axis=0)
    ri = lax.broadcasted_iota(jnp.int32, (rows, 2 * rows), 0)
    ci = lax.broadcasted_iota(jnp.int32, (rows, 2 * rows), 1)
    later = ((ri > ci) | (ci >= rows)).astype(F32)
    sums = jnp.dot(lf_all, later, preferred_element_type=F32, precision=lax.Precision.HIGHEST)
    carry = suf_ref[...]
    scores = []
    for g in range(npar):
        bias = sums[g * nh:(g + 1) * nh, :rows] + carry
        carry = carry + sums[g * nh:(g + 1) * nh, rows:]
        scores.append(_dot(qbd, kc[g][0].reshape(d, rows).astype(BF16)) + bias)
    suf_ref[...] = carry
    s = jnp.concatenate(scores, axis=1)
    m_prev = m_ref[...]
    m_new = jnp.maximum(m_prev, jnp.max(s, axis=1, keepdims=True))
    pr = jnp.exp(s - jnp.concatenate([m_new] * (npar * rows // LANES), axis=1))
    alpha = jnp.exp(m_prev - m_new)
    l_ref[...] = alpha * l_ref[...] + jnp.sum(pr, axis=1, keepdims=True)
    pb = pr.astype(BF16)
    pv = None
    for g in range(npar):
        part = lax.dot_general(pb[:, g * rows:(g + 1) * rows], vc[g][0].reshape(d, rows).astype(BF16), nt,
                               preferred_element_type=F32)
        pv = part if pv is None else pv + part
    acc_ref[...] = jnp.concatenate([alpha] * (d // LANES), axis=1) * acc_ref[...] + pv
    m_ref[...] = m_new

    @pl.when(st == nst - 1)
    def _():
        out = acc_ref[...] / jnp.concatenate([l_ref[...]] * (d // LANES), axis=1)
        o_ref[0] = jnp.sum(jnp.where(own, out, 0.0), axis=0, keepdims=True)


def _attn_sample(page_table, q, k_new, v_new, lf_new, cache_kt, cache_vt, cache_lft, npar):
    m, _, d = q.shape
    npg = page_table.shape[1]
    _, nh, hd, rows = cache_kt.shape
    assert rows == LANES and npg % npar == 0

    def page_idx(g):
        return lambda b, s, pt: (pt[b, npg - 1 - (s * npar + g)], 0, 0, 0)

    def lf_idx(g):
        return lambda b, s, pt: (pt[b, npg - 1 - (s * npar + g)], 0, 0)

    tok = pl.BlockSpec((1, 1, d), lambda b, s, pt: (b, 0, 0))
    grid_spec = pltpu.PrefetchScalarGridSpec(
        num_scalar_prefetch=1,
        grid=(m, npg // npar),
        in_specs=([tok, tok, tok, pl.BlockSpec((1, nh, LANES), lambda b, s, pt: (b, 0, 0))]
                  + [pl.BlockSpec((1, nh, hd, rows), page_idx(g)) for g in range(npar)]
                  + [pl.BlockSpec((1, nh, hd, rows), page_idx(g)) for g in range(npar)]
                  + [pl.BlockSpec((1, nh, rows), lf_idx(g)) for g in range(npar)]),
        out_specs=tok,
        scratch_shapes=[pltpu.VMEM((nh, LANES), F32), pltpu.VMEM((nh, LANES), F32),
                        pltpu.VMEM((nh, d), F32), pltpu.VMEM((nh, LANES), F32)],
    )
    return pl.pallas_call(
        functools.partial(_attn_sample_kernel, npar=npar),
        grid_spec=grid_spec,
        out_shape=jax.ShapeDtypeStruct((m, 1, d), F32),
        compiler_params=_cparams("arbitrary", "arbitrary"),
        name="attn_sample",
    )(page_table, q, k_new, v_new, lf_new, *([cache_kt] * npar), *([cache_vt] * npar), *([cache_lft] * npar))


TM_MIX = 256
TM_S5 = 256
TM_FFN = 512
FC_FFN = 256
TQ_ATTN = 512
TK_ATTN = 1024
PAGES_PER_STEP = 8


def kernel(x_prompt, x_sample, state_sconv_l0, state_ssm_re_l1, state_ssm_im_l1, cache_k_l2, cache_v_l2, cache_logf_l2, state_sconv_l3, state_ffn_conv, page_table, sc_w_in_l0, sc_conv_w_l0, sc_w_out_l0, s5_w_in_l1, s5_lambda_re_l1, s5_lambda_im_l1, s5_log_dt_l1, s5_b_re_l1, s5_b_im_l1, s5_c_re_l1, s5_c_im_l1, s5_d_l1, s5_glu_wa_l1, s5_glu_wb_l1, fox_w_qkvf_l2, fox_b_f_l2, fox_w_o_l2, sc_w_in_l3, sc_conv_w_l3, sc_w_out_l3, norm_mix_pre, norm_mix_post, norm_ffn_pre, norm_ffn_post, ffn_w_gate, ffn_w_up, ffn_conv_w, ffn_w_down):
    bp, n, d = x_prompt.shape
    m = x_sample.shape[0]
    nh = fox_b_f_l2.shape[0]
    hd = d // nh
    f = ffn_w_gate.shape[-1]
    g, p = s5_lambda_re_l1.shape
    ns = g * p
    bf = lambda w: w.astype(BF16)
    row = lambda v: v.reshape(1, -1)

    tm_mix = min(TM_MIX, n)
    tm_s5 = min(TM_S5, n)
    tm_ffn = min(TM_FFN, n)
    tq = min(TQ_ATTN, n)
    fc = min(FC_FFN, f)

    ptab, coef, w_bu, w_cr, w_ci = _s5_tables(s5_lambda_re_l1, s5_lambda_im_l1, s5_log_dt_l1,
                                              s5_b_re_l1, s5_b_im_l1, s5_c_re_l1, s5_c_im_l1,
                                              tm_s5 // SUBLANES)
    scale = hd ** -0.5
    w_qkvf = jnp.concatenate([fox_w_qkvf_l2[:, :d] * scale, fox_w_qkvf_l2[:, d:],
                              jnp.zeros((d, LANES - nh), F32)], axis=1).astype(BF16)
    bf_pad = jnp.concatenate([fox_b_f_l2, jnp.zeros((LANES - nh,), F32)]).reshape(1, LANES)
    pq, pk = _fox_placement(nh)
    sc_params = {0: (bf(sc_w_in_l0), sc_conv_w_l0, bf(sc_w_out_l0)),
                 3: (bf(sc_w_in_l3), sc_conv_w_l3, bf(sc_w_out_l3))}
    s5_tail = (w_bu, w_cr, w_ci, row(s5_d_l1), bf(s5_glu_wa_l1), bf(s5_glu_wb_l1))
    s5_w_in = bf(s5_w_in_l1)
    w_o = bf(fox_w_o_l2)
    ffn_w = (norm_ffn_pre[:, None, :], norm_ffn_post[:, None, :],
             bf(ffn_w_gate), bf(ffn_w_up), ffn_conv_w, bf(ffn_w_down))

    xp = x_prompt
    ffn_p = []
    xp, sc0_p = _sconv_prompt(xp, row(norm_mix_pre[0]), row(norm_mix_post[0]), *sc_params[0], tm_ffn)
    xp, fb = _ffn_prompt(xp, 0, *ffn_w, tm_ffn, fc)
    ffn_p.append(fb)
    zeros_state = jnp.zeros((bp, 1, ns), F32)
    xp, sr_p, si_p = _s5_prompt(xp, row(norm_mix_pre[1]), row(norm_mix_post[1]), s5_w_in, ptab, coef, *s5_tail,
                                zeros_state, zeros_state, tm_s5)
    xp, fb = _ffn_prompt(xp, 1, *ffn_w, tm_ffn, fc)
    ffn_p.append(fb)
    qa, ka, va, k_p, v_p, lf_p = _qkv_prompt(xp, row(norm_mix_pre[2]), w_qkvf, bf_pad, pq, pk, nh, tm_mix)
    o_p = _attn_prompt(qa, ka, va, hd, tq, min(TK_ATTN, n))
    xp = _oproj(o_p.reshape(bp * n, d), xp.reshape(bp * n, d), w_o, row(norm_mix_post[2]),
                min(512, bp * n)).reshape(bp, n, d)
    xp, fb = _ffn_prompt(xp, 2, *ffn_w, tm_ffn, fc)
    ffn_p.append(fb)
    xp, sc3_p = _sconv_prompt(xp, row(norm_mix_pre[3]), row(norm_mix_post[3]), *sc_params[3], tm_ffn)
    xp, fb = _ffn_prompt(xp, 3, *ffn_w, tm_ffn, fc)
    ffn_p.append(fb)

    xs = x_sample.reshape(m, d)
    ffn_s = []
    st_ffn = jnp.swapaxes(state_ffn_conv, 1, 2)

    def ffn_s_layer(xs, i):
        y, ns_ = _ffn_sample(xs, i, st_ffn, *ffn_w, fc)
        ffn_s.append(jnp.swapaxes(ns_, 0, 1))
        return y

    xs, sc0_s = _sconv_sample(xs, jnp.swapaxes(state_sconv_l0, 0, 1), row(norm_mix_pre[0]),
                              row(norm_mix_post[0]), *sc_params[0])
    xs = ffn_s_layer(xs, 0)
    xs, sr_s, si_s = _s5_sample(xs, state_ssm_re_l1.reshape(m, ns), state_ssm_im_l1.reshape(m, ns),
                                row(norm_mix_pre[1]), row(norm_mix_post[1]), s5_w_in, ptab, *s5_tail)
    xs = ffn_s_layer(xs, 1)
    q_s, k_s, v_s, lf_s = _qkv_sample(xs, row(norm_mix_pre[2]), w_qkvf, bf_pad)
    npg = page_table.shape[1]
    npar = max(c for c in range(1, PAGES_PER_STEP + 1) if npg % c == 0)
    o_s = _attn_sample(page_table, q_s.reshape(m, 1, d), k_s.reshape(m, 1, d), v_s.reshape(m, 1, d),
                       jnp.broadcast_to(lf_s[:, :nh, None], (m, nh, LANES)),
                       jnp.transpose(cache_k_l2, (0, 2, 3, 1)), jnp.transpose(cache_v_l2, (0, 2, 3, 1)),
                       jnp.transpose(cache_logf_l2, (0, 2, 1)), npar)
    xs = _oproj(o_s.reshape(m, d), xs, w_o, row(norm_mix_post[2]), m)
    xs = ffn_s_layer(xs, 2)
    xs, sc3_s = _sconv_sample(xs, jnp.swapaxes(state_sconv_l3, 0, 1), row(norm_mix_pre[3]),
                              row(norm_mix_post[3]), *sc_params[3])
    xs = ffn_s_layer(xs, 3)

    return (xp, xs.reshape(m, 1, d),
            sc0_p, jnp.swapaxes(sc0_s, 0, 1),
            sr_p.reshape(bp, g, p), sr_s.reshape(m, g, p), si_p.reshape(bp, g, p), si_s.reshape(m, g, p),
            k_p.reshape(bp, n, nh, hd), k_s.reshape(m, 1, nh, hd),
            v_p.reshape(bp, n, nh, hd), v_s.reshape(m, 1, nh, hd),
            lf_p, lf_s[:, :nh].reshape(m, 1, nh),
            sc3_p, jnp.swapaxes(sc3_s, 0, 1),
            jnp.stack(ffn_p), jnp.stack(ffn_s))
```

```python
import functools
import math

import jax
import jax.numpy as jnp
from jax import lax
from jax.experimental import pallas as pl
from jax.experimental.pallas import tpu as pltpu

F32 = jnp.float32
BF16 = jnp.bfloat16

RMS_EPS = 1e-6
CONV_W = 3
S5_GROUP = 16
S5_STATE = 64
FOX_HEADS = 16
PAGE_SIZE = 128

LANES = 128
SUBLANES = 8
MXU_DIM = 256
VMEM_LIMIT_BYTES = 56 * 1024 * 1024

NEG_BIG = -1e30
LOG2E = math.log2(math.e)
SCAN_UNROLL = True


def _cparams(*sem):
    return pltpu.CompilerParams(dimension_semantics=sem, vmem_limit_bytes=VMEM_LIMIT_BYTES)


def _const_spec(shape):
    nd = len(shape)
    return pl.BlockSpec(shape, lambda *_: (0,) * nd, pipeline_mode=pl.Buffered(1))


def _rms(x, g):
    ms = jnp.mean(x * x, axis=-1, keepdims=True)
    return x * lax.rsqrt(ms + RMS_EPS) * g


def _dot(a, b):
    return jnp.dot(a, b, preferred_element_type=F32)


def _shift_rows(cur, prev_tail, k):
    rolled = pltpu.roll(cur, k, axis=0)
    rid = lax.broadcasted_iota(jnp.int32, (SUBLANES, cur.shape[1]), 0)
    head = jnp.where(rid < k, pltpu.roll(prev_tail, k, axis=0), rolled[:SUBLANES, :])
    return jnp.concatenate([head, rolled[SUBLANES:, :]], axis=0)


def _causal_conv3(cur, prev_tail, w):
    x1 = _shift_rows(cur, prev_tail, 1)
    x2 = _shift_rows(cur, prev_tail, 2)
    return x2 * w[0:1, :] + x1 * w[1:2, :] + cur * w[2:3, :]


def _sconv_prompt_kernel(x_ref, gpre_ref, gpost_ref, win_ref, cw_ref, wout_ref,
                         y_ref, st_ref, tail_ref):
    t = pl.program_id(1)
    nt = pl.num_programs(1)
    d = x_ref.shape[-1]
    tm = x_ref.shape[1]

    @pl.when(t == 0)
    def _():
        tail_ref[...] = jnp.zeros_like(tail_ref)

    x = x_ref[0]
    h = _rms(x, gpre_ref[...]).astype(BF16)
    proj = _dot(h, win_ref[...])
    b = proj[:, :d]
    ch = proj[:, d:2 * d] * proj[:, 2 * d:]
    z = _causal_conv3(ch, tail_ref[...], cw_ref[...])
    y = _dot((b * z).astype(BF16), wout_ref[...])
    y_ref[0] = x + _rms(y, gpost_ref[...])
    tail_ref[...] = ch[tm - SUBLANES:, :]

    @pl.when(t == nt - 1)
    def _():
        st_ref[0] = ch[tm - (CONV_W - 1):, :]


def _sconv_prompt(x, gpre, gpost, w_in, conv_w, w_out, tm):
    bsz, n, d = x.shape
    return pl.pallas_call(
        _sconv_prompt_kernel,
        grid=(bsz, n // tm),
        in_specs=[
            pl.BlockSpec((1, tm, d), lambda b, t: (b, t, 0)),
            _const_spec((1, d)), _const_spec((1, d)),
            _const_spec((d, 3 * d)), _const_spec((CONV_W, d)), _const_spec((d, d)),
        ],
        out_specs=[
            pl.BlockSpec((1, tm, d), lambda b, t: (b, t, 0)),
            pl.BlockSpec((1, CONV_W - 1, d), lambda b, t: (b, 0, 0)),
        ],
        out_shape=[jax.ShapeDtypeStruct((bsz, n, d), F32),
                   jax.ShapeDtypeStruct((bsz, CONV_W - 1, d), F32)],
        scratch_shapes=[pltpu.VMEM((SUBLANES, d), F32)],
        compiler_params=_cparams("arbitrary", "arbitrary"),
        name="sconv_prompt",
    )(x, gpre, gpost, w_in, conv_w, w_out)


def _ffn_prompt_kernel(x_ref, gpre_ref, gpost_ref, wg_ref, wu_ref, cw_ref, wd_ref,
                       y_ref, st_ref, tail_ref, act_ref, *, fc):
    t = pl.program_id(1)
    nt = pl.num_programs(1)
    tm = x_ref.shape[1]
    f = wg_ref.shape[1]

    @pl.when(t == 0)
    def _():
        tail_ref[...] = jnp.zeros_like(tail_ref)

    x = x_ref[0]
    h = _rms(x, gpre_ref[...]).astype(BF16)
    for c in range(f // fc):
        cs = slice(c * fc, (c + 1) * fc)
        g = _dot(h, wg_ref[:, cs])
        u = _dot(h, wu_ref[:, cs])
        gc = _causal_conv3(g, tail_ref[:, cs], cw_ref[:, cs])
        act_ref[:, cs] = (gc * jax.nn.sigmoid(gc) * u).astype(BF16)
        tail_ref[:, cs] = g[tm - SUBLANES:, :]
    y = _dot(act_ref[...], wd_ref[...])
    y_ref[0] = x + _rms(y, gpost_ref[...])

    @pl.when(t == nt - 1)
    def _():
        st_ref[0] = tail_ref[SUBLANES - (CONV_W - 1):, :]


def _layer_spec(shape, layer):
    nd = len(shape)
    return pl.BlockSpec((None,) + tuple(shape), lambda *_: (layer,) + (0,) * nd, pipeline_mode=pl.Buffered(1))


def _ffn_prompt(x, layer, gpre, gpost, wg, wu, conv_w, wd, tm, fc):
    bsz, n, d = x.shape
    f = wg.shape[-1]
    return pl.pallas_call(
        functools.partial(_ffn_prompt_kernel, fc=fc),
        grid=(bsz, n // tm),
        in_specs=[
            pl.BlockSpec((1, tm, d), lambda b, t: (b, t, 0)),
            _layer_spec((1, d), layer), _layer_spec((1, d), layer),
            _layer_spec((d, f), layer), _layer_spec((d, f), layer),
            _layer_spec((CONV_W, f), layer), _layer_spec((f, d), layer),
        ],
        out_specs=[
            pl.BlockSpec((1, tm, d), lambda b, t: (b, t, 0)),
            pl.BlockSpec((1, CONV_W - 1, f), lambda b, t: (b, 0, 0)),
        ],
        out_shape=[jax.ShapeDtypeStruct((bsz, n, d), F32),
                   jax.ShapeDtypeStruct((bsz, CONV_W - 1, f), F32)],
        scratch_shapes=[pltpu.VMEM((SUBLANES, f), F32), pltpu.VMEM((tm, f), BF16)],
        compiler_params=_cparams("arbitrary", "arbitrary"),
        name="ffn_prompt",
    )(x, gpre, gpost, wg, wu, conv_w, wd)


def _s5_prep_kernel(lr_ref, li_ref, ldt_ref, bre_ref, bim_ref,
                    pwr_ref, pwi_ref, sgr_ref, sgi_ref, bbr_ref, bbi_ref):
    lr = lr_ref[...]
    li = li_ref[...]
    dt = jnp.exp(ldt_ref[...])
    mag = jnp.exp(lr * dt)
    abar_r = mag * jnp.cos(li * dt)
    abar_i = mag * jnp.sin(li * dt)

    def powers(ref_r, ref_i, br, bi):
        cr, ci = br, bi
        for k in range(ref_r.shape[0]):
            ref_r[k] = cr
            ref_i[k] = ci
            cr, ci = cr * br - ci * bi, cr * bi + ci * br

    seg = pwr_ref.shape[0]
    powers(pwr_ref, pwi_ref, abar_r, abar_i)
    powers(sgr_ref, sgi_ref, pwr_ref[seg - 1], pwi_ref[seg - 1])
    den = lr * lr + li * li
    nr = abar_r - 1.0
    kr = (nr * lr + abar_i * li) / den
    ki = (abar_i * lr - nr * li) / den
    bre = bre_ref[...]
    bim = bim_ref[...]
    bbr_ref[...] = kr[:, None, :] * bre - ki[:, None, :] * bim
    bbi_ref[...] = kr[:, None, :] * bim + ki[:, None, :] * bre


def _s5_prep(lam_re, lam_im, log_dt, b_re_t, b_im_t, seg):
    g, p = lam_re.shape
    n = b_re_t.shape[1]
    return pl.pallas_call(
        _s5_prep_kernel,
        out_shape=[jax.ShapeDtypeStruct((seg, g, p), F32), jax.ShapeDtypeStruct((seg, g, p), F32),
                   jax.ShapeDtypeStruct((SUBLANES, g, p), F32), jax.ShapeDtypeStruct((SUBLANES, g, p), F32),
                   jax.ShapeDtypeStruct((g, n, p), F32), jax.ShapeDtypeStruct((g, n, p), F32)],
        name="s5_prep",
    )(lam_re, lam_im, log_dt.reshape(g, 1), b_re_t, b_im_t)


def _s5_block_diag(w, sgroups):
    g, a, b = w.shape
    w4 = w.reshape(g // sgroups, sgroups, a, b)
    eye = jnp.eye(sgroups, dtype=w.dtype)
    return jnp.einsum('sgab,gh->sgahb', w4, eye).reshape(g // sgroups, sgroups * a, sgroups * b)


def _s5_tables(lam_re, lam_im, log_dt, b_re, b_im, c_re, c_im, seg):
    g, p = lam_re.shape
    pwr, pwi, sgr, sgi, bbr, bbi = _s5_prep(lam_re, lam_im, log_dt,
                                            jnp.swapaxes(b_re, 1, 2), jnp.swapaxes(b_im, 1, 2), seg)
    sg = MXU_DIM // S5_GROUP
    w_bu = jnp.concatenate([_s5_block_diag(bbr, sg), _s5_block_diag(bbi, sg)], axis=-1).astype(BF16)
    w_cr = _s5_block_diag(jnp.swapaxes(c_re, 1, 2), sg).astype(BF16)
    w_ci = _s5_block_diag(jnp.swapaxes(c_im, 1, 2), sg).astype(BF16)
    ptab = jnp.stack([pwr.reshape(seg, g * p)[:1], pwi.reshape(seg, g * p)[:1]])
    sgr = sgr.reshape(SUBLANES, g * p)
    sgi = sgi.reshape(SUBLANES, g * p)
    rows = jnp.arange(SUBLANES)[:, None]
    steps = []
    for s in (1, 2, 4):
        steps.append(jnp.where(rows >= s, sgr[s - 1][None, :], 0.0))
        steps.append(jnp.where(rows >= s, sgi[s - 1][None, :], 0.0))
    coef = jnp.stack(steps + [sgr, sgi])
    return ptab, coef, w_bu, w_cr, w_ci


def _cmul_add(xr, xi, ar, ai, sr, si):
    return xr + ar * sr - ai * si, xi + ar * si + ai * sr


def _s5_prompt_kernel(x_ref, gpre_ref, gpost_ref, perm_ref, win_ref, ptab_ref, coef_ref, wbu_ref, wcr_ref,
                      wci_ref, dsk_ref, wa_ref, wb_ref, h0r_ref, h0i_ref,
                      y_ref, sr_ref, si_ref, br_ref, bi_ref, cr_ref, ci_ref):
    t = pl.program_id(1)
    tm = x_ref.shape[1]
    seg = tm // SUBLANES
    nsg = wbu_ref.shape[0]
    cw = wbu_ref.shape[1]
    sw = wbu_ref.shape[2] // 2

    @pl.when(t == 0)
    def _():
        cr_ref[...] = jnp.broadcast_to(h0r_ref[0], cr_ref.shape)
        ci_ref[...] = jnp.broadcast_to(h0i_ref[0], ci_ref.shape)

    x = x_ref[0]
    h = _dot(perm_ref[0], _rms(x, gpre_ref[...]).astype(BF16)).astype(BF16)
    u = _dot(h, win_ref[...])
    ub = u.astype(BF16)
    lc = sw
    for c in range(nsg):
        bu = _dot(ub[:, c * cw:(c + 1) * cw], wbu_ref[c])
        br_ref[:, c * lc:(c + 1) * lc] = bu[:, :sw]
        bi_ref[:, c * lc:(c + 1) * lc] = bu[:, sw:]

    rid = lax.broadcasted_iota(jnp.int32, (SUBLANES, lc), 0)
    for c in range(nsg):
        cs = slice(c * lc, (c + 1) * lc)
        ar = ptab_ref[0, 0:1, cs]
        ai = ptab_ref[1, 0:1, cs]

        def local_step(j, hh, cs=cs, ar=ar, ai=ai):
            r0 = pl.multiple_of(j * SUBLANES, SUBLANES)
            nr, ni = _cmul_add(br_ref[pl.ds(r0, SUBLANES), cs], bi_ref[pl.ds(r0, SUBLANES), cs],
                               ar, ai, hh[0], hh[1])
            br_ref[pl.ds(r0, SUBLANES), cs] = nr
            bi_ref[pl.ds(r0, SUBLANES), cs] = ni
            return nr, ni

        zero = jnp.zeros((SUBLANES, lc), F32)
        er, ei = lax.fori_loop(0, seg, local_step, (zero, zero), unroll=SCAN_UNROLL)
        for j, sh in enumerate((1, 2, 4)):
            er, ei = _cmul_add(er, ei, coef_ref[2 * j, :, cs], coef_ref[2 * j + 1, :, cs],
                               pltpu.roll(er, sh, axis=0), pltpu.roll(ei, sh, axis=0))
        er, ei = _cmul_add(er, ei, coef_ref[6, :, cs], coef_ref[7, :, cs], cr_ref[:, cs], ci_ref[:, cs])
        inr = jnp.where(rid == 0, cr_ref[:, cs], pltpu.roll(er, 1, axis=0))
        ini = jnp.where(rid == 0, ci_ref[:, cs], pltpu.roll(ei, 1, axis=0))
        cr_ref[:, cs] = jnp.broadcast_to(er[SUBLANES - 1:, :], (SUBLANES, lc))
        ci_ref[:, cs] = jnp.broadcast_to(ei[SUBLANES - 1:, :], (SUBLANES, lc))

        def carry_step(j, cc, cs=cs, ar=ar, ai=ai):
            r0 = pl.multiple_of(j * SUBLANES, SUBLANES)
            nr = ar * cc[0] - ai * cc[1]
            ni = ar * cc[1] + ai * cc[0]
            br_ref[pl.ds(r0, SUBLANES), cs] = br_ref[pl.ds(r0, SUBLANES), cs] + nr
            bi_ref[pl.ds(r0, SUBLANES), cs] = bi_ref[pl.ds(r0, SUBLANES), cs] + ni
            return nr, ni

        lax.fori_loop(0, seg, carry_step, (inr, ini), unroll=SCAN_UNROLL)

    ys = []
    for c in range(nsg):
        cs = slice(c * lc, (c + 1) * lc)
        ys.append(_dot(br_ref[:, cs].astype(BF16), wcr_ref[c]) - _dot(bi_ref[:, cs].astype(BF16), wci_ref[c]))
    yy = jnp.concatenate(ys, axis=1) + dsk_ref[...] * u
    z = _dot(perm_ref[1], jax.nn.gelu(yy).astype(BF16)).astype(BF16)
    out = _dot(z, wa_ref[...]) * jax.nn.sigmoid(_dot(z, wb_ref[...]))
    y_ref[0] = x + _rms(out, gpost_ref[...])
    sr_ref[0] = cr_ref[0:1, :]
    si_ref[0] = ci_ref[0:1, :]


def _s5_prompt(x, gpre, gpost, w_in, ptab, coef, w_bu, w_cr, w_ci, d_skip, wa, wb, h0r, h0i, tm):
    bsz, n, d = x.shape
    ns = coef.shape[-1]
    seg = tm // SUBLANES
    src = (jnp.arange(tm) % SUBLANES) * seg + jnp.arange(tm) // SUBLANES
    gather = (src[:, None] == jnp.arange(tm)[None, :])
    perm = jnp.stack([gather, gather.T]).astype(BF16)
    return pl.pallas_call(
        _s5_prompt_kernel,
        grid=(bsz, n // tm),
        in_specs=[
            pl.BlockSpec((1, tm, d), lambda b, t: (b, t, 0)),
            _const_spec((1, d)), _const_spec((1, d)), _const_spec(perm.shape), _const_spec((d, d)),
            _const_spec(ptab.shape), _const_spec(coef.shape),
            _const_spec(w_bu.shape), _const_spec(w_cr.shape), _const_spec(w_ci.shape),
            _const_spec((1, d)), _const_spec((d, d)), _const_spec((d, d)),
            pl.BlockSpec((1, 1, ns), lambda b, t: (b, 0, 0)),
            pl.BlockSpec((1, 1, ns), lambda b, t: (b, 0, 0)),
        ],
        out_specs=[
            pl.BlockSpec((1, tm, d), lambda b, t: (b, t, 0)),
            pl.BlockSpec((1, 1, ns), lambda b, t: (b, 0, 0)),
            pl.BlockSpec((1, 1, ns), lambda b, t: (b, 0, 0)),
        ],
        out_shape=[jax.ShapeDtypeStruct((bsz, n, d), F32),
                   jax.ShapeDtypeStruct((bsz, 1, ns), F32), jax.ShapeDtypeStruct((bsz, 1, ns), F32)],
        scratch_shapes=[pltpu.VMEM((tm, ns), F32), pltpu.VMEM((tm, ns), F32),
                        pltpu.VMEM((SUBLANES, ns), F32), pltpu.VMEM((SUBLANES, ns), F32)],
        compiler_params=_cparams("arbitrary", "arbitrary"),
        name="s5_prompt",
    )(x, gpre, gpost, perm, w_in, ptab, coef, w_bu, w_cr, w_ci, d_skip, wa, wb, h0r, h0i)


def _log_sigmoid(x):
    return -(jnp.maximum(-x, 0.0) + jnp.log1p(jnp.exp(-jnp.abs(x))))


def _split3(x):
    hi = x.astype(BF16).astype(F32)
    r = x - hi
    mid = r.astype(BF16).astype(F32)
    lo = r - mid
    return hi, mid, lo


def _fox_placement(nh):
    hd = LANES // 2
    pq = [[0.0] * (nh * LANES) for _ in range(LANES)]
    pk = [[0.0] * (nh * LANES) for _ in range(LANES)]
    one = 3 * nh
    for h in range(nh):
        base = h * LANES + hd
        for j in range(3):
            pq[j * nh + h][base + j] = 1.0
            pq[one][base + 3 + j] = 1.0
            pk[one][base + j] = 1.0
            pk[j * nh + h][base + 3 + j] = -1.0
    return jnp.array(pq, BF16), jnp.array(pk, BF16)


def _qkv_prompt_kernel(x_ref, gpre_ref, w_ref, bf_ref, pq_ref, pk_ref,
                       qa_ref, ka_ref, va_ref, k_ref, v_ref, lf_ref, carry_ref):
    t = pl.program_id(1)
    tm = x_ref.shape[1]
    d = x_ref.shape[-1]
    nh = qa_ref.shape[1]
    hd = d // nh

    @pl.when(t == 0)
    def _():
        carry_ref[...] = jnp.zeros_like(carry_ref)

    x = x_ref[0]
    h = _rms(x, gpre_ref[...]).astype(BF16)
    proj = _dot(h, w_ref[...])
    lane = lax.broadcasted_iota(jnp.int32, (tm, LANES), 1)
    logf = jnp.where(lane < nh, _log_sigmoid(proj[:, 3 * d:] + bf_ref[...]), 0.0)
    ri = lax.broadcasted_iota(jnp.int32, (tm, tm), 0)
    ci = lax.broadcasted_iota(jnp.int32, (tm, tm), 1)
    tri = (ci <= ri).astype(F32)
    cum = jnp.dot(tri, logf, preferred_element_type=F32, precision=lax.Precision.HIGHEST) + carry_ref[0:1, :]
    carry_ref[...] = jnp.broadcast_to(cum[tm - 1:, :], carry_ref.shape)
    hi, mid, lo = _split3(cum * LOG2E)
    src = hi + pltpu.roll(mid, nh, axis=1) + pltpu.roll(lo, 2 * nh, axis=1) + (lane == 3 * nh).astype(F32)
    src = src.astype(BF16)
    aug_q = _dot(src, pq_ref[...])
    aug_k = _dot(src, pk_ref[...])
    low = lane < hd
    one_at_hd = (lane == hd).astype(F32)
    for c in range(d // LANES):
        for o, dst in enumerate((qa_ref, ka_ref, va_ref)):
            blk = proj[:, o * d + c * LANES:o * d + (c + 1) * LANES]
            if o == 0:
                blk = blk * LOG2E
            for half in range(LANES // hd):
                hh = c * (LANES // hd) + half
                v = blk if half == 0 else pltpu.roll(blk, LANES - half * hd, axis=1)
                fill = (aug_q, aug_k)[o][:, hh * LANES:(hh + 1) * LANES] if o < 2 else one_at_hd
                dst[0, hh] = jnp.where(low, v, fill).astype(BF16)
    k_ref[0] = proj[:, d:2 * d]
    v_ref[0] = proj[:, 2 * d:3 * d]
    lf_ref[0] = logf[:, :nh]


def _qkv_prompt(x, gpre, w_all, bf_pad, pq, pk, nh, tm):
    bsz, n, d = x.shape
    head_spec = pl.BlockSpec((1, nh, tm, LANES), lambda b, t: (b, 0, t, 0))
    row_spec = pl.BlockSpec((1, tm, d), lambda b, t: (b, t, 0))
    head_shape = jax.ShapeDtypeStruct((bsz, nh, n, LANES), BF16)
    return pl.pallas_call(
        _qkv_prompt_kernel,
        grid=(bsz, n // tm),
        in_specs=[row_spec, _const_spec((1, d)), _const_spec(w_all.shape), _const_spec((1, LANES)),
                  _const_spec(pq.shape), _const_spec(pk.shape)],
        out_specs=[head_spec, head_spec, head_spec, row_spec, row_spec,
                   pl.BlockSpec((1, tm, nh), lambda b, t: (b, t, 0))],
        out_shape=[head_shape, head_shape, head_shape,
                   jax.ShapeDtypeStruct((bsz, n, d), F32), jax.ShapeDtypeStruct((bsz, n, d), F32),
                   jax.ShapeDtypeStruct((bsz, n, nh), F32)],
        scratch_shapes=[pltpu.VMEM((SUBLANES, LANES), F32)],
        compiler_params=_cparams("arbitrary", "arbitrary"),
        name="qkv_prompt",
    )(x, gpre, w_all, bf_pad, pq, pk)


def _attn_prompt_kernel(qe_ref, qo_ref, k_ref, v_ref, oe_ref, oo_ref, q_ref, m_ref, acc_ref, *, hd, nfull):
    e = pl.program_id(2)
    hpb, tq = qe_ref.shape[1], qe_ref.shape[2]
    tk = 2 * tq
    nt = (((1,), (1,)), ((), ()))
    q_ref[0] = qe_ref[0]
    q_ref[1] = qo_ref[0]
    m_ref[...] = jnp.full_like(m_ref, NEG_BIG)
    acc_ref[...] = jnp.zeros_like(acc_ref)

    def update(w, start, cols, mask_off):
        for j in range(hpb):
            s = lax.dot_general(q_ref[w, j], k_ref[0, j, pl.ds(start, cols), :], nt,
                                preferred_element_type=F32)
            if mask_off is not None:
                row = lax.broadcasted_iota(jnp.int32, (tq, cols), 0)
                col = lax.broadcasted_iota(jnp.int32, (tq, cols), 1)
                s = jnp.where(col <= row + mask_off, s, NEG_BIG)
            m_prev = m_ref[w, j]
            m_new = jnp.maximum(m_prev, jnp.max(s, axis=1, keepdims=True))
            pb = jnp.concatenate([jnp.exp2(s[:, c * LANES:(c + 1) * LANES] - m_new).astype(BF16)
                                  for c in range(cols // LANES)], axis=1)
            acc_ref[w, j] = (jnp.exp2(m_prev - m_new) * acc_ref[w, j]
                             + _dot(pb, v_ref[0, j, pl.ds(start, cols), :]))
            m_ref[w, j] = m_new

    for s in range(nfull):
        w = (s >= e).astype(jnp.int32)
        blk = jnp.where(s < e, s, s - e)
        update(w, pl.multiple_of(blk * tk, tk), tk, None)
    update(0, pl.multiple_of(e * tk, tk), tq, 0)
    update(1, pl.multiple_of((nfull - e) * tk, tk), tk, tq)

    lane = lax.broadcasted_iota(jnp.int32, (tq, LANES), 1)
    for w, o_ref in enumerate((oe_ref, oo_ref)):
        out = None
        for j in range(hpb):
            acc = acc_ref[w, j]
            oj = acc / jnp.sum(jnp.where(lane == hd, acc, 0.0), axis=1, keepdims=True)
            out = oj if j == 0 else jnp.where(lane < j * hd, out, pltpu.roll(oj, j * hd, axis=1))
        o_ref[0, 0] = out


def _attn_prompt(qa, ka, va, hd, tq):
    bsz, nh, n, _ = qa.shape
    hpb = LANES // hd
    nq = n // tq
    assert nq % 2 == 0
    half = nq // 2
    seq_spec = pl.BlockSpec((1, hpb, n, LANES), lambda b, h, e: (b, h, 0, 0))
    out_spec = pl.BlockSpec((1, 1, tq, LANES), lambda b, h, e: (b, e, 0, h))
    out_shape = jax.ShapeDtypeStruct((bsz, half, tq, nh * hd), F32)
    return pl.pallas_call(
        functools.partial(_attn_prompt_kernel, hd=hd, nfull=half - 1),
        grid=(bsz, nh // hpb, half),
        in_specs=[pl.BlockSpec((1, hpb, tq, LANES), lambda b, h, e: (b, h, 2 * e, 0)),
                  pl.BlockSpec((1, hpb, tq, LANES), lambda b, h, e: (b, h, nq - 1 - 2 * e, 0)),
                  seq_spec, seq_spec],
        out_specs=[out_spec, out_spec],
        out_shape=[out_shape, out_shape],
        scratch_shapes=[pltpu.VMEM((2, hpb, tq, LANES), BF16), pltpu.VMEM((2, hpb, tq, LANES), F32),
                        pltpu.VMEM((2, hpb, tq, LANES), F32)],
        compiler_params=_cparams("arbitrary", "arbitrary", "arbitrary"),
        name="attn_prompt",
    )(qa, qa, ka, va)


def _oproj_kernel(o_ref, x_ref, w_ref, g_ref, y_ref):
    y = _dot(o_ref[...].astype(BF16), w_ref[...])
    y_ref[...] = x_ref[...] + _rms(y, g_ref[...])


def _oproj_pair_kernel(oe_ref, oo_ref, x_ref, w_ref, g_ref, y_ref):
    even = pl.program_id(0) % 2 == 0
    o = jnp.where(even, oe_ref[0, 0], oo_ref[0, 0])
    y = _dot(o.astype(BF16), w_ref[...])
    y_ref[...] = x_ref[...] + _rms(y, g_ref[...])


def _oproj_pair(o_even, o_odd, x, w, g):
    bsz, half, tq, d = o_even.shape
    nq = 2 * half
    row = pl.BlockSpec((tq, d), lambda t: (t, 0))
    return pl.pallas_call(
        _oproj_pair_kernel,
        grid=(bsz * nq,),
        in_specs=[pl.BlockSpec((1, 1, tq, d), lambda t: (t // nq, (t % nq) // 2, 0, 0)),
                  pl.BlockSpec((1, 1, tq, d), lambda t: (t // nq, (nq - 1 - t % nq) // 2, 0, 0)),
                  row, _const_spec((d, d)), _const_spec((1, d))],
        out_specs=row,
        out_shape=jax.ShapeDtypeStruct(x.shape, F32),
        compiler_params=_cparams("arbitrary"),
        name="oproj_pair",
    )(o_even, o_odd, x, w, g)


def _oproj(o, x, w, g, tm):
    m, d = x.shape
    row = pl.BlockSpec((tm, d), lambda t: (t, 0))
    return pl.pallas_call(
        _oproj_kernel,
        grid=(m // tm,),
        in_specs=[row, row, _const_spec((d, d)), _const_spec((1, d))],
        out_specs=row,
        out_shape=jax.ShapeDtypeStruct((m, d), F32),
        compiler_params=_cparams("arbitrary"),
        name="oproj",
    )(o, x, w, g)


def _sconv_sample_kernel(x_ref, st_ref, gpre_ref, gpost_ref, win_ref, cw_ref, wout_ref, y_ref, ns_ref):
    d = x_ref.shape[-1]
    x = x_ref[...]
    h = _rms(x, gpre_ref[...]).astype(BF16)
    proj = _dot(h, win_ref[...])
    b = proj[:, :d]
    ch = proj[:, d:2 * d] * proj[:, 2 * d:]
    s0 = st_ref[0]
    s1 = st_ref[1]
    w = cw_ref[...]
    z = s0 * w[0:1, :] + s1 * w[1:2, :] + ch * w[2:3, :]
    y = _dot((b * z).astype(BF16), wout_ref[...])
    y_ref[...] = x + _rms(y, gpost_ref[...])
    ns_ref[0] = s1
    ns_ref[1] = ch


def _sconv_sample(x, st, gpre, gpost, w_in, conv_w, w_out):
    m, d = x.shape
    return pl.pallas_call(
        _sconv_sample_kernel,
        out_shape=[jax.ShapeDtypeStruct((m, d), F32), jax.ShapeDtypeStruct((CONV_W - 1, m, d), F32)],
        compiler_params=pltpu.CompilerParams(vmem_limit_bytes=VMEM_LIMIT_BYTES),
        name="sconv_sample",
    )(x, st, gpre, gpost, w_in, conv_w, w_out)


def _ffn_sample_kernel(x_ref, st_ref, gpre_ref, gpost_ref, wg_ref, wu_ref, cw_ref, wd_ref,
                       y_ref, ns_ref, acc_ref):
    c = pl.program_id(0)

    @pl.when(c == 0)
    def _():
        acc_ref[...] = jnp.zeros_like(acc_ref)

    x = x_ref[...]
    h = _rms(x, gpre_ref[...]).astype(BF16)
    g = _dot(h, wg_ref[...])
    u = _dot(h, wu_ref[...])
    s0 = st_ref[0]
    s1 = st_ref[1]
    w = cw_ref[...]
    gc = s0 * w[0:1, :] + s1 * w[1:2, :] + g * w[2:3, :]
    act = (gc * jax.nn.sigmoid(gc) * u).astype(BF16)
    acc_ref[...] += _dot(act, wd_ref[...])
    ns_ref[0] = s1
    ns_ref[1] = g

    @pl.when(c == pl.num_programs(0) - 1)
    def _():
        y_ref[...] = x + _rms(acc_ref[...], gpost_ref[...])


def _ffn_sample(x, layer, st, gpre, gpost, wg, wu, conv_w, wd, fc):
    m, d = x.shape
    f = wg.shape[-1]
    full = pl.BlockSpec((m, d), lambda c: (0, 0))
    vec = pl.BlockSpec((None, 1, d), lambda c: (layer, 0, 0))
    return pl.pallas_call(
        _ffn_sample_kernel,
        grid=(f // fc,),
        in_specs=[full, pl.BlockSpec((None, CONV_W - 1, m, fc), lambda c: (layer, 0, 0, c)), vec, vec,
                  pl.BlockSpec((None, d, fc), lambda c: (layer, 0, c)),
                  pl.BlockSpec((None, d, fc), lambda c: (layer, 0, c)),
                  pl.BlockSpec((None, CONV_W, fc), lambda c: (layer, 0, c)),
                  pl.BlockSpec((None, fc, d), lambda c: (layer, c, 0))],
        out_specs=[full, pl.BlockSpec((CONV_W - 1, m, fc), lambda c: (0, 0, c))],
        out_shape=[jax.ShapeDtypeStruct((m, d), F32), jax.ShapeDtypeStruct((CONV_W - 1, m, f), F32)],
        scratch_shapes=[pltpu.VMEM((m, d), F32)],
        compiler_params=_cparams("arbitrary"),
        name="ffn_sample",
    )(x, st, gpre, gpost, wg, wu, conv_w, wd)


def _s5_sample_kernel(x_ref, h0r_ref, h0i_ref, gpre_ref, gpost_ref, win_ref, ptab_ref, wbu_ref,
                      wcr_ref, wci_ref, dsk_ref, wa_ref, wb_ref, y_ref, sr_ref, si_ref):
    nsg = wbu_ref.shape[0]
    cw = wbu_ref.shape[1]
    sw = wbu_ref.shape[2] // 2
    x = x_ref[...]
    h = _rms(x, gpre_ref[...]).astype(BF16)
    u = _dot(h, win_ref[...])
    ub = u.astype(BF16)
    ar = ptab_ref[0, 0:1, :]
    ai = ptab_ref[1, 0:1, :]
    ys = []
    for s in range(nsg):
        cs = slice(s * sw, (s + 1) * sw)
        bu = _dot(ub[:, s * cw:(s + 1) * cw], wbu_ref[s])
        hr, hi = _cmul_add(bu[:, :sw], bu[:, sw:], ar[:, cs], ai[:, cs], h0r_ref[:, cs], h0i_ref[:, cs])
        sr_ref[:, cs] = hr
        si_ref[:, cs] = hi
        ys.append(_dot(hr.astype(BF16), wcr_ref[s]) - _dot(hi.astype(BF16), wci_ref[s]))
    yy = jnp.concatenate(ys, axis=1) + dsk_ref[...] * u
    z = jax.nn.gelu(yy).astype(BF16)
    out = _dot(z, wa_ref[...]) * jax.nn.sigmoid(_dot(z, wb_ref[...]))
    y_ref[...] = x + _rms(out, gpost_ref[...])


def _s5_sample(x, h0r, h0i, gpre, gpost, w_in, ptab, w_bu, w_cr, w_ci, d_skip, wa, wb):
    m, d = x.shape
    ns = h0r.shape[1]
    return pl.pallas_call(
        _s5_sample_kernel,
        out_shape=[jax.ShapeDtypeStruct((m, d), F32),
                   jax.ShapeDtypeStruct((m, ns), F32), jax.ShapeDtypeStruct((m, ns), F32)],
        compiler_params=pltpu.CompilerParams(vmem_limit_bytes=VMEM_LIMIT_BYTES),
        name="s5_sample",
    )(x, h0r, h0i, gpre, gpost, w_in, ptab, w_bu, w_cr, w_ci, d_skip, wa, wb)


def _qkv_sample_kernel(x_ref, gpre_ref, w_ref, bf_ref, q_ref, k_ref, v_ref, lf_ref):
    d = x_ref.shape[-1]
    h = _rms(x_ref[...], gpre_ref[...]).astype(BF16)
    proj = _dot(h, w_ref[...])
    q_ref[...] = proj[:, :d]
    k_ref[...] = proj[:, d:2 * d]
    v_ref[...] = proj[:, 2 * d:3 * d]
    lf_ref[...] = _log_sigmoid(proj[:, 3 * d:] + bf_ref[...])


def _qkv_sample(x, gpre, w_all, bf_pad):
    m, d = x.shape
    row = jax.ShapeDtypeStruct((m, d), F32)
    return pl.pallas_call(
        _qkv_sample_kernel,
        out_shape=[row, row, row, jax.ShapeDtypeStruct((m, LANES), F32)],
        compiler_params=pltpu.CompilerParams(vmem_limit_bytes=VMEM_LIMIT_BYTES),
        name="qkv_sample",
    )(x, gpre, w_all, bf_pad)


def _attn_sample_kernel(pt_ref, q_ref, kn_ref, vn_ref, lfn_ref, *rest, npar):
    del pt_ref
    kc, vc, lfc = rest[:npar], rest[npar:2 * npar], rest[2 * npar:3 * npar]
    o_ref, m_ref, l_ref, acc_ref, suf_ref = rest[3 * npar:]
    st = pl.program_id(1)
    nst = pl.num_programs(1)
    nh, hd, rows = kc[0].shape[1], kc[0].shape[2], kc[0].shape[3]
    d = nh * hd
    nt = (((1,), (1,)), ((), ()))
    rid = lax.broadcasted_iota(jnp.int32, (nh, d), 0)
    lid = lax.broadcasted_iota(jnp.int32, (nh, d), 1)
    own = (lid >= rid * hd) & (lid < (rid + 1) * hd)
    qbd = jnp.where(own, q_ref[0], 0.0).astype(BF16)

    @pl.when(st == 0)
    def _():
        kb = jnp.broadcast_to(kn_ref[0], (LANES, d)).astype(BF16)
        m_ref[...] = lax.dot_general(qbd, kb, nt, preferred_element_type=F32)
        l_ref[...] = jnp.ones_like(l_ref)
        acc_ref[...] = jnp.broadcast_to(vn_ref[0], (nh, d))
        suf_ref[...] = lfn_ref[0]

    lf_all = jnp.concatenate([r[0] for r in lfc], axis=0)
    ri = lax.broadcasted_iota(jnp.int32, (rows, 2 * rows), 0)
    ci = lax.broadcasted_iota(jnp.int32, (rows, 2 * rows), 1)
    later = ((ri > ci) | (ci >= rows)).astype(F32)
    sums = jnp.dot(lf_all, later, preferred_element_type=F32, precision=lax.Precision.HIGHEST)
    carry = suf_ref[...]
    scores = []
    for g in range(npar):
        bias = sums[g * nh:(g + 1) * nh, :rows] + carry
        carry = carry + sums[g * nh:(g + 1) * nh, rows:]
        scores.append(_dot(qbd, kc[g][0].reshape(d, rows).astype(BF16)) + bias)
    suf_ref[...] = carry
    s = jnp.concatenate(scores, axis=1)
    m_prev = m_ref[...]
    m_new = jnp.maximum(m_prev, jnp.max(s, axis=1, keepdims=True))
    pr = jnp.exp(s - jnp.concatenate([m_new] * (npar * rows // LANES), axis=1))
    alpha = jnp.exp(m_prev - m_new)
    l_ref[...] = alpha * l_ref[...] + jnp.sum(pr, axis=1, keepdims=True)
    pb = pr.astype(BF16)
    pv = None
    for g in range(npar):
        part = lax.dot_general(pb[:, g * rows:(g + 1) * rows], vc[g][0].reshape(d, rows).astype(BF16), nt,
                               preferred_element_type=F32)
        pv = part if pv is None else pv + part
    acc_ref[...] = jnp.concatenate([alpha] * (d // LANES), axis=1) * acc_ref[...] + pv
    m_ref[...] = m_new

    @pl.when(st == nst - 1)
    def _():
        out = acc_ref[...] / jnp.concatenate([l_ref[...]] * (d // LANES), axis=1)
        o_ref[0] = jnp.sum(jnp.where(own, out, 0.0), axis=0, keepdims=True)


def _attn_sample(page_table, q, k_new, v_new, lf_new, cache_kt, cache_vt, cache_lft, npar):
    m, _, d = q.shape
    npg = page_table.shape[1]
    _, nh, hd, rows = cache_kt.shape
    assert rows == LANES and npg % npar == 0

    def page_idx(g):
        return lambda b, s, pt: (pt[b, npg - 1 - (s * npar + g)], 0, 0, 0)

    def lf_idx(g):
        return lambda b, s, pt: (pt[b, npg - 1 - (s * npar + g)], 0, 0)

    tok = pl.BlockSpec((1, 1, d), lambda b, s, pt: (b, 0, 0))
    grid_spec = pltpu.PrefetchScalarGridSpec(
        num_scalar_prefetch=1,
        grid=(m, npg // npar),
        in_specs=([tok, tok, tok, pl.BlockSpec((1, nh, LANES), lambda b, s, pt: (b, 0, 0))]
                  + [pl.BlockSpec((1, nh, hd, rows), page_idx(g)) for g in range(npar)]
                  + [pl.BlockSpec((1, nh, hd, rows), page_idx(g)) for g in range(npar)]
                  + [pl.BlockSpec((1, nh, rows), lf_idx(g)) for g in range(npar)]),
        out_specs=tok,
        scratch_shapes=[pltpu.VMEM((nh, LANES), F32), pltpu.VMEM((nh, LANES), F32),
                        pltpu.VMEM((nh, d), F32), pltpu.VMEM((nh, LANES), F32)],
    )
    return pl.pallas_call(
        functools.partial(_attn_sample_kernel, npar=npar),
        grid_spec=grid_spec,
        out_shape=jax.ShapeDtypeStruct((m, 1, d), F32),
        compiler_params=_cparams("arbitrary", "arbitrary"),
        name="attn_sample",
    )(page_table, q, k_new, v_new, lf_new, *([cache_kt] * npar), *([cache_vt] * npar), *([cache_lft] * npar))


TM_MIX = 256
TM_S5 = 256
TM_FFN = 512
FC_FFN = 256
TQ_ATTN = 512
PAGES_PER_STEP = 8


def kernel(x_prompt, x_sample, state_sconv_l0, state_ssm_re_l1, state_ssm_im_l1, cache_k_l2, cache_v_l2, cache_logf_l2, state_sconv_l3, state_ffn_conv, page_table, sc_w_in_l0, sc_conv_w_l0, sc_w_out_l0, s5_w_in_l1, s5_lambda_re_l1, s5_lambda_im_l1, s5_log_dt_l1, s5_b_re_l1, s5_b_im_l1, s5_c_re_l1, s5_c_im_l1, s5_d_l1, s5_glu_wa_l1, s5_glu_wb_l1, fox_w_qkvf_l2, fox_b_f_l2, fox_w_o_l2, sc_w_in_l3, sc_conv_w_l3, sc_w_out_l3, norm_mix_pre, norm_mix_post, norm_ffn_pre, norm_ffn_post, ffn_w_gate, ffn_w_up, ffn_conv_w, ffn_w_down):
    bp, n, d = x_prompt.shape
    m = x_sample.shape[0]
    nh = fox_b_f_l2.shape[0]
    hd = d // nh
    f = ffn_w_gate.shape[-1]
    g, p = s5_lambda_re_l1.shape
    ns = g * p
    bf = lambda w: w.astype(BF16)
    row = lambda v: v.reshape(1, -1)

    tm_mix = min(TM_MIX, n)
    tm_s5 = min(TM_S5, n)
    tm_ffn = min(TM_FFN, n)
    tq = min(TQ_ATTN, n)
    fc = min(FC_FFN, f)

    ptab, coef, w_bu, w_cr, w_ci = _s5_tables(s5_lambda_re_l1, s5_lambda_im_l1, s5_log_dt_l1,
                                              s5_b_re_l1, s5_b_im_l1, s5_c_re_l1, s5_c_im_l1,
                                              tm_s5 // SUBLANES)
    scale = hd ** -0.5
    w_qkvf = jnp.concatenate([fox_w_qkvf_l2[:, :d] * scale, fox_w_qkvf_l2[:, d:],
                              jnp.zeros((d, LANES - nh), F32)], axis=1).astype(BF16)
    bf_pad = jnp.concatenate([fox_b_f_l2, jnp.zeros((LANES - nh,), F32)]).reshape(1, LANES)
    pq, pk = _fox_placement(nh)
    sc_params = {0: (bf(sc_w_in_l0), sc_conv_w_l0, bf(sc_w_out_l0)),
                 3: (bf(sc_w_in_l3), sc_conv_w_l3, bf(sc_w_out_l3))}
    s5_tail = (w_bu, w_cr, w_ci, row(s5_d_l1), bf(s5_glu_wa_l1), bf(s5_glu_wb_l1))
    s5_w_in = bf(s5_w_in_l1)
    w_o = bf(fox_w_o_l2)
    ffn_w = (norm_ffn_pre[:, None, :], norm_ffn_post[:, None, :],
             bf(ffn_w_gate), bf(ffn_w_up), ffn_conv_w, bf(ffn_w_down))

    xp = x_prompt
    ffn_p = []
    xp, sc0_p = _sconv_prompt(xp, row(norm_mix_pre[0]), row(norm_mix_post[0]), *sc_params[0], tm_ffn)
    xp, fb = _ffn_prompt(xp, 0, *ffn_w, tm_ffn, fc)
    ffn_p.append(fb)
    zeros_state = jnp.zeros((bp, 1, ns), F32)
    xp, sr_p, si_p = _s5_prompt(xp, row(norm_mix_pre[1]), row(norm_mix_post[1]), s5_w_in, ptab, coef, *s5_tail,
                                zeros_state, zeros_state, tm_s5)
    xp, fb = _ffn_prompt(xp, 1, *ffn_w, tm_ffn, fc)
    ffn_p.append(fb)
    qa, ka, va, k_p, v_p, lf_p = _qkv_prompt(xp, row(norm_mix_pre[2]), w_qkvf, bf_pad, pq, pk, nh, tm_mix)
    o_even, o_odd = _attn_prompt(qa, ka, va, hd, tq)
    xp = _oproj_pair(o_even, o_odd, xp.reshape(bp * n, d), w_o, row(norm_mix_post[2])).reshape(bp, n, d)
    xp, fb = _ffn_prompt(xp, 2, *ffn_w, tm_ffn, fc)
    ffn_p.append(fb)
    xp, sc3_p = _sconv_prompt(xp, row(norm_mix_pre[3]), row(norm_mix_post[3]), *sc_params[3], tm_ffn)
    xp, fb = _ffn_prompt(xp, 3, *ffn_w, tm_ffn, fc)
    ffn_p.append(fb)

    xs = x_sample.reshape(m, d)
    ffn_s = []
    st_ffn = jnp.swapaxes(state_ffn_conv, 1, 2)

    def ffn_s_layer(xs, i):
        y, ns_ = _ffn_sample(xs, i, st_ffn, *ffn_w, fc)
        ffn_s.append(jnp.swapaxes(ns_, 0, 1))
        return y

    xs, sc0_s = _sconv_sample(xs, jnp.swapaxes(state_sconv_l0, 0, 1), row(norm_mix_pre[0]),
                              row(norm_mix_post[0]), *sc_params[0])
    xs = ffn_s_layer(xs, 0)
    xs, sr_s, si_s = _s5_sample(xs, state_ssm_re_l1.reshape(m, ns), state_ssm_im_l1.reshape(m, ns),
                                row(norm_mix_pre[1]), row(norm_mix_post[1]), s5_w_in, ptab, *s5_tail)
    xs = ffn_s_layer(xs, 1)
    q_s, k_s, v_s, lf_s = _qkv_sample(xs, row(norm_mix_pre[2]), w_qkvf, bf_pad)
    npg = page_table.shape[1]
    npar = max(c for c in range(1, PAGES_PER_STEP + 1) if npg % c == 0)
    o_s = _attn_sample(page_table, q_s.reshape(m, 1, d), k_s.reshape(m, 1, d), v_s.reshape(m, 1, d),
                       jnp.broadcast_to(lf_s[:, :nh, None], (m, nh, LANES)),
                       jnp.transpose(cache_k_l2, (0, 2, 3, 1)), jnp.transpose(cache_v_l2, (0, 2, 3, 1)),
                       jnp.transpose(cache_logf_l2, (0, 2, 1)), npar)
    xs = _oproj(o_s.reshape(m, d), xs, w_o, row(norm_mix_post[2]), m)
    xs = ffn_s_layer(xs, 2)
    xs, sc3_s = _sconv_sample(xs, jnp.swapaxes(state_sconv_l3, 0, 1), row(norm_mix_pre[3]),
                              row(norm_mix_post[3]), *sc_params[3])
    xs = ffn_s_layer(xs, 3)

    return (xp, xs.reshape(m, 1, d),
            sc0_p, jnp.swapaxes(sc0_s, 0, 1),
            sr_p.reshape(bp, g, p), sr_s.reshape(m, g, p), si_p.reshape(bp, g, p), si_s.reshape(m, g, p),
            k_p.reshape(bp, n, nh, hd), k_s.reshape(m, 1, nh, hd),
            v_p.reshape(bp, n, nh, hd), v_s.reshape(m, 1, nh, hd),
            lf_p, lf_s[:, :nh].reshape(m, 1, nh),
            sc3_p, jnp.swapaxes(sc3_s, 0, 1),
            jnp.stack(ffn_p), jnp.stack(ffn_s))
```

```python
import functools
import math

import jax
import jax.numpy as jnp
from jax import lax
from jax.experimental import pallas as pl
from jax.experimental.pallas import tpu as pltpu

F32 = jnp.float32
BF16 = jnp.bfloat16

RMS_EPS = 1e-6
CONV_W = 3
S5_GROUP = 16
S5_STATE = 64
FOX_HEADS = 16
PAGE_SIZE = 128

LANES = 128
SUBLANES = 8
MXU_DIM = 256
VMEM_LIMIT_BYTES = 56 * 1024 * 1024

NEG_BIG = -1e30
LOG2E = math.log2(math.e)
SCAN_UNROLL = True


def _cparams(*sem):
    return pltpu.CompilerParams(dimension_semantics=sem, vmem_limit_bytes=VMEM_LIMIT_BYTES)


def _const_spec(shape):
    nd = len(shape)
    return pl.BlockSpec(shape, lambda *_: (0,) * nd, pipeline_mode=pl.Buffered(1))


def _rms(x, g):
    ms = jnp.mean(x * x, axis=-1, keepdims=True)
    return x * lax.rsqrt(ms + RMS_EPS) * g


def _dot(a, b):
    return jnp.dot(a, b, preferred_element_type=F32)


def _shift_rows(cur, prev_tail, k):
    rolled = pltpu.roll(cur, k, axis=0)
    rid = lax.broadcasted_iota(jnp.int32, (SUBLANES, cur.shape[1]), 0)
    head = jnp.where(rid < k, pltpu.roll(prev_tail, k, axis=0), rolled[:SUBLANES, :])
    return jnp.concatenate([head, rolled[SUBLANES:, :]], axis=0)


def _causal_conv3(cur, prev_tail, w):
    x1 = _shift_rows(cur, prev_tail, 1)
    x2 = _shift_rows(cur, prev_tail, 2)
    return x2 * w[0:1, :] + x1 * w[1:2, :] + cur * w[2:3, :]


def _sconv_prompt_kernel(x_ref, gpre_ref, gpost_ref, win_ref, cw_ref, wout_ref,
                         y_ref, st_ref, tail_ref):
    t = pl.program_id(1)
    nt = pl.num_programs(1)
    d = x_ref.shape[-1]
    tm = x_ref.shape[1]

    @pl.when(t == 0)
    def _():
        tail_ref[...] = jnp.zeros_like(tail_ref)

    x = x_ref[0]
    h = _rms(x, gpre_ref[...]).astype(BF16)
    proj = _dot(h, win_ref[...])
    b = proj[:, :d]
    ch = proj[:, d:2 * d] * proj[:, 2 * d:]
    z = _causal_conv3(ch, tail_ref[...], cw_ref[...])
    y = _dot((b * z).astype(BF16), wout_ref[...])
    y_ref[0] = x + _rms(y, gpost_ref[...])
    tail_ref[...] = ch[tm - SUBLANES:, :]

    @pl.when(t == nt - 1)
    def _():
        st_ref[0] = ch[tm - (CONV_W - 1):, :]


def _sconv_prompt(x, gpre, gpost, w_in, conv_w, w_out, tm):
    bsz, n, d = x.shape
    return pl.pallas_call(
        _sconv_prompt_kernel,
        grid=(bsz, n // tm),
        in_specs=[
            pl.BlockSpec((1, tm, d), lambda b, t: (b, t, 0)),
            _const_spec((1, d)), _const_spec((1, d)),
            _const_spec((d, 3 * d)), _const_spec((CONV_W, d)), _const_spec((d, d)),
        ],
        out_specs=[
            pl.BlockSpec((1, tm, d), lambda b, t: (b, t, 0)),
            pl.BlockSpec((1, CONV_W - 1, d), lambda b, t: (b, 0, 0)),
        ],
        out_shape=[jax.ShapeDtypeStruct((bsz, n, d), F32),
                   jax.ShapeDtypeStruct((bsz, CONV_W - 1, d), F32)],
        scratch_shapes=[pltpu.VMEM((SUBLANES, d), F32)],
        compiler_params=_cparams("arbitrary", "arbitrary"),
        name="sconv_prompt",
    )(x, gpre, gpost, w_in, conv_w, w_out)


def _ffn_prompt_kernel(x_ref, gpre_ref, gpost_ref, wg_ref, wu_ref, cw_ref, wd_ref,
                       y_ref, st_ref, tail_ref, act_ref, *, fc):
    t = pl.program_id(1)
    nt = pl.num_programs(1)
    tm = x_ref.shape[1]
    f = wg_ref.shape[1]

    @pl.when(t == 0)
    def _():
        tail_ref[...] = jnp.zeros_like(tail_ref)

    x = x_ref[0]
    h = _rms(x, gpre_ref[...]).astype(BF16)
    for c in range(f // fc):
        cs = slice(c * fc, (c + 1) * fc)
        g = _dot(h, wg_ref[:, cs])
        u = _dot(h, wu_ref[:, cs])
        gc = _causal_conv3(g, tail_ref[:, cs], cw_ref[:, cs])
        act_ref[:, cs] = (gc * jax.nn.sigmoid(gc) * u).astype(BF16)
        tail_ref[:, cs] = g[tm - SUBLANES:, :]
    y = _dot(act_ref[...], wd_ref[...])
    y_ref[0] = x + _rms(y, gpost_ref[...])

    @pl.when(t == nt - 1)
    def _():
        st_ref[0] = tail_ref[SUBLANES - (CONV_W - 1):, :]


def _layer_spec(shape, layer):
    nd = len(shape)
    return pl.BlockSpec((None,) + tuple(shape), lambda *_: (layer,) + (0,) * nd, pipeline_mode=pl.Buffered(1))


def _ffn_prompt(x, layer, gpre, gpost, wg, wu, conv_w, wd, tm, fc):
    bsz, n, d = x.shape
    f = wg.shape[-1]
    return pl.pallas_call(
        functools.partial(_ffn_prompt_kernel, fc=fc),
        grid=(bsz, n // tm),
        in_specs=[
            pl.BlockSpec((1, tm, d), lambda b, t: (b, t, 0)),
            _layer_spec((1, d), layer), _layer_spec((1, d), layer),
            _const_spec((d, f)), _const_spec((d, f)),
            _layer_spec((CONV_W, f), layer), _const_spec((f, d)),
        ],
        out_specs=[
            pl.BlockSpec((1, tm, d), lambda b, t: (b, t, 0)),
            pl.BlockSpec((1, CONV_W - 1, f), lambda b, t: (b, 0, 0)),
        ],
        out_shape=[jax.ShapeDtypeStruct((bsz, n, d), F32),
                   jax.ShapeDtypeStruct((bsz, CONV_W - 1, f), F32)],
        scratch_shapes=[pltpu.VMEM((SUBLANES, f), F32), pltpu.VMEM((tm, f), BF16)],
        compiler_params=_cparams("arbitrary", "arbitrary"),
        name="ffn_prompt",
    )(x, gpre, gpost, wg, wu, conv_w, wd)


def _s5_prep_kernel(lr_ref, li_ref, ldt_ref, bre_ref, bim_ref,
                    pwr_ref, pwi_ref, sgr_ref, sgi_ref, bbr_ref, bbi_ref):
    lr = lr_ref[...]
    li = li_ref[...]
    dt = jnp.exp(ldt_ref[...])
    mag = jnp.exp(lr * dt)
    abar_r = mag * jnp.cos(li * dt)
    abar_i = mag * jnp.sin(li * dt)

    def powers(ref_r, ref_i, br, bi):
        cr, ci = br, bi
        for k in range(ref_r.shape[0]):
            ref_r[k] = cr
            ref_i[k] = ci
            cr, ci = cr * br - ci * bi, cr * bi + ci * br

    seg = pwr_ref.shape[0]
    powers(pwr_ref, pwi_ref, abar_r, abar_i)
    powers(sgr_ref, sgi_ref, pwr_ref[seg - 1], pwi_ref[seg - 1])
    den = lr * lr + li * li
    nr = abar_r - 1.0
    kr = (nr * lr + abar_i * li) / den
    ki = (abar_i * lr - nr * li) / den
    bre = bre_ref[...]
    bim = bim_ref[...]
    bbr_ref[...] = kr[:, None, :] * bre - ki[:, None, :] * bim
    bbi_ref[...] = kr[:, None, :] * bim + ki[:, None, :] * bre


def _s5_prep(lam_re, lam_im, log_dt, b_re_t, b_im_t, seg):
    g, p = lam_re.shape
    n = b_re_t.shape[1]
    return pl.pallas_call(
        _s5_prep_kernel,
        out_shape=[jax.ShapeDtypeStruct((seg, g, p), F32), jax.ShapeDtypeStruct((seg, g, p), F32),
                   jax.ShapeDtypeStruct((SUBLANES, g, p), F32), jax.ShapeDtypeStruct((SUBLANES, g, p), F32),
                   jax.ShapeDtypeStruct((g, n, p), F32), jax.ShapeDtypeStruct((g, n, p), F32)],
        name="s5_prep",
    )(lam_re, lam_im, log_dt.reshape(g, 1), b_re_t, b_im_t)


def _s5_block_diag(w, sgroups):
    g, a, b = w.shape
    w4 = w.reshape(g // sgroups, sgroups, a, b)
    eye = jnp.eye(sgroups, dtype=w.dtype)
    return jnp.einsum('sgab,gh->sgahb', w4, eye).reshape(g // sgroups, sgroups * a, sgroups * b)


def _s5_tables(lam_re, lam_im, log_dt, b_re, b_im, c_re, c_im, seg):
    g, p = lam_re.shape
    pwr, pwi, sgr, sgi, bbr, bbi = _s5_prep(lam_re, lam_im, log_dt,
                                            jnp.swapaxes(b_re, 1, 2), jnp.swapaxes(b_im, 1, 2), seg)
    sg = MXU_DIM // S5_GROUP
    w_bu = jnp.concatenate([_s5_block_diag(bbr, sg), _s5_block_diag(bbi, sg)], axis=-1).astype(BF16)
    w_cr = _s5_block_diag(jnp.swapaxes(c_re, 1, 2), sg).astype(BF16)
    w_ci = _s5_block_diag(jnp.swapaxes(c_im, 1, 2), sg).astype(BF16)
    ptab = jnp.stack([pwr.reshape(seg, g * p)[:1], pwi.reshape(seg, g * p)[:1]])
    sgr = sgr.reshape(SUBLANES, g * p)
    sgi = sgi.reshape(SUBLANES, g * p)
    rows = jnp.arange(SUBLANES)[:, None]
    steps = []
    for s in (1, 2, 4):
        steps.append(jnp.where(rows >= s, sgr[s - 1][None, :], 0.0))
        steps.append(jnp.where(rows >= s, sgi[s - 1][None, :], 0.0))
    coef = jnp.stack(steps + [sgr, sgi])
    return ptab, coef, w_bu, w_cr, w_ci


def _cmul_add(xr, xi, ar, ai, sr, si):
    return xr + ar * sr - ai * si, xi + ar * si + ai * sr


def _s5_prompt_kernel(x_ref, gpre_ref, gpost_ref, perm_ref, win_ref, ptab_ref, coef_ref, wbu_ref, wcr_ref,
                      wci_ref, dsk_ref, wa_ref, wb_ref, h0r_ref, h0i_ref,
                      y_ref, sr_ref, si_ref, br_ref, bi_ref, cr_ref, ci_ref):
    t = pl.program_id(1)
    tm = x_ref.shape[1]
    seg = tm // SUBLANES
    nsg = wbu_ref.shape[0]
    cw = wbu_ref.shape[1]
    sw = wbu_ref.shape[2] // 2

    @pl.when(t == 0)
    def _():
        cr_ref[...] = jnp.broadcast_to(h0r_ref[0], cr_ref.shape)
        ci_ref[...] = jnp.broadcast_to(h0i_ref[0], ci_ref.shape)

    x = x_ref[0]
    h = _dot(perm_ref[0], _rms(x, gpre_ref[...]).astype(BF16)).astype(BF16)
    u = _dot(h, win_ref[...])
    ub = u.astype(BF16)
    lc = sw
    for c in range(nsg):
        bu = _dot(ub[:, c * cw:(c + 1) * cw], wbu_ref[c])
        br_ref[:, c * lc:(c + 1) * lc] = bu[:, :sw]
        bi_ref[:, c * lc:(c + 1) * lc] = bu[:, sw:]

    rid = lax.broadcasted_iota(jnp.int32, (SUBLANES, lc), 0)
    for c in range(nsg):
        cs = slice(c * lc, (c + 1) * lc)
        ar = ptab_ref[0, 0:1, cs]
        ai = ptab_ref[1, 0:1, cs]

        def local_step(j, hh, cs=cs, ar=ar, ai=ai):
            r0 = pl.multiple_of(j * SUBLANES, SUBLANES)
            nr, ni = _cmul_add(br_ref[pl.ds(r0, SUBLANES), cs], bi_ref[pl.ds(r0, SUBLANES), cs],
                               ar, ai, hh[0], hh[1])
            br_ref[pl.ds(r0, SUBLANES), cs] = nr
            bi_ref[pl.ds(r0, SUBLANES), cs] = ni
            return nr, ni

        zero = jnp.zeros((SUBLANES, lc), F32)
        er, ei = lax.fori_loop(0, seg, local_step, (zero, zero), unroll=SCAN_UNROLL)
        for j, sh in enumerate((1, 2, 4)):
            er, ei = _cmul_add(er, ei, coef_ref[2 * j, :, cs], coef_ref[2 * j + 1, :, cs],
                               pltpu.roll(er, sh, axis=0), pltpu.roll(ei, sh, axis=0))
        er, ei = _cmul_add(er, ei, coef_ref[6, :, cs], coef_ref[7, :, cs], cr_ref[:, cs], ci_ref[:, cs])
        inr = jnp.where(rid == 0, cr_ref[:, cs], pltpu.roll(er, 1, axis=0))
        ini = jnp.where(rid == 0, ci_ref[:, cs], pltpu.roll(ei, 1, axis=0))
        cr_ref[:, cs] = jnp.broadcast_to(er[SUBLANES - 1:, :], (SUBLANES, lc))
        ci_ref[:, cs] = jnp.broadcast_to(ei[SUBLANES - 1:, :], (SUBLANES, lc))

        def carry_step(j, cc, cs=cs, ar=ar, ai=ai):
            r0 = pl.multiple_of(j * SUBLANES, SUBLANES)
            nr = ar * cc[0] - ai * cc[1]
            ni = ar * cc[1] + ai * cc[0]
            br_ref[pl.ds(r0, SUBLANES), cs] = br_ref[pl.ds(r0, SUBLANES), cs] + nr
            bi_ref[pl.ds(r0, SUBLANES), cs] = bi_ref[pl.ds(r0, SUBLANES), cs] + ni
            return nr, ni

        lax.fori_loop(0, seg, carry_step, (inr, ini), unroll=SCAN_UNROLL)

    ys = []
    for c in range(nsg):
        cs = slice(c * lc, (c + 1) * lc)
        ys.append(_dot(br_ref[:, cs].astype(BF16), wcr_ref[c]) - _dot(bi_ref[:, cs].astype(BF16), wci_ref[c]))
    yy = jnp.concatenate(ys, axis=1) + dsk_ref[...] * u
    z = _dot(perm_ref[1], jax.nn.gelu(yy).astype(BF16)).astype(BF16)
    out = _dot(z, wa_ref[...]) * jax.nn.sigmoid(_dot(z, wb_ref[...]))
    y_ref[0] = x + _rms(out, gpost_ref[...])
    sr_ref[0] = cr_ref[0:1, :]
    si_ref[0] = ci_ref[0:1, :]


def _s5_prompt(x, gpre, gpost, w_in, ptab, coef, w_bu, w_cr, w_ci, d_skip, wa, wb, h0r, h0i, tm):
    bsz, n, d = x.shape
    ns = coef.shape[-1]
    seg = tm // SUBLANES
    src = (jnp.arange(tm) % SUBLANES) * seg + jnp.arange(tm) // SUBLANES
    gather = (src[:, None] == jnp.arange(tm)[None, :])
    perm = jnp.stack([gather, gather.T]).astype(BF16)
    return pl.pallas_call(
        _s5_prompt_kernel,
        grid=(bsz, n // tm),
        in_specs=[
            pl.BlockSpec((1, tm, d), lambda b, t: (b, t, 0)),
            _const_spec((1, d)), _const_spec((1, d)), _const_spec(perm.shape), _const_spec((d, d)),
            _const_spec(ptab.shape), _const_spec(coef.shape),
            _const_spec(w_bu.shape), _const_spec(w_cr.shape), _const_spec(w_ci.shape),
            _const_spec((1, d)), _const_spec((d, d)), _const_spec((d, d)),
            pl.BlockSpec((1, 1, ns), lambda b, t: (b, 0, 0)),
            pl.BlockSpec((1, 1, ns), lambda b, t: (b, 0, 0)),
        ],
        out_specs=[
            pl.BlockSpec((1, tm, d), lambda b, t: (b, t, 0)),
            pl.BlockSpec((1, 1, ns), lambda b, t: (b, 0, 0)),
            pl.BlockSpec((1, 1, ns), lambda b, t: (b, 0, 0)),
        ],
        out_shape=[jax.ShapeDtypeStruct((bsz, n, d), F32),
                   jax.ShapeDtypeStruct((bsz, 1, ns), F32), jax.ShapeDtypeStruct((bsz, 1, ns), F32)],
        scratch_shapes=[pltpu.VMEM((tm, ns), F32), pltpu.VMEM((tm, ns), F32),
                        pltpu.VMEM((SUBLANES, ns), F32), pltpu.VMEM((SUBLANES, ns), F32)],
        compiler_params=_cparams("arbitrary", "arbitrary"),
        name="s5_prompt",
    )(x, gpre, gpost, perm, w_in, ptab, coef, w_bu, w_cr, w_ci, d_skip, wa, wb, h0r, h0i)


def _log_sigmoid(x):
    return -(jnp.maximum(-x, 0.0) + jnp.log1p(jnp.exp(-jnp.abs(x))))


def _split3(x):
    hi = x.astype(BF16).astype(F32)
    r = x - hi
    mid = r.astype(BF16).astype(F32)
    lo = r - mid
    return hi, mid, lo


def _fox_placement(nh):
    hd = LANES // 2
    pq = [[0.0] * (nh * LANES) for _ in range(LANES)]
    pk = [[0.0] * (nh * LANES) for _ in range(LANES)]
    one = 3 * nh
    for h in range(nh):
        base = h * LANES + hd
        for j in range(3):
            pq[j * nh + h][base + j] = 1.0
            pq[one][base + 3 + j] = 1.0
            pk[one][base + j] = 1.0
            pk[j * nh + h][base + 3 + j] = -1.0
    return jnp.array(pq, BF16), jnp.array(pk, BF16)


def _qkv_prompt_kernel(x_ref, gpre_ref, w_ref, bf_ref, pq_ref, pk_ref,
                       qa_ref, ka_ref, va_ref, k_ref, v_ref, lf_ref, carry_ref):
    t = pl.program_id(1)
    tm = x_ref.shape[1]
    d = x_ref.shape[-1]
    nh = qa_ref.shape[1]
    hd = d // nh

    @pl.when(t == 0)
    def _():
        carry_ref[...] = jnp.zeros_like(carry_ref)

    x = x_ref[0]
    h = _rms(x, gpre_ref[...]).astype(BF16)
    proj = _dot(h, w_ref[...])
    lane = lax.broadcasted_iota(jnp.int32, (tm, LANES), 1)
    logf = jnp.where(lane < nh, _log_sigmoid(proj[:, 3 * d:] + bf_ref[...]), 0.0)
    ri = lax.broadcasted_iota(jnp.int32, (tm, tm), 0)
    ci = lax.broadcasted_iota(jnp.int32, (tm, tm), 1)
    tri = (ci <= ri).astype(F32)
    cum = jnp.dot(tri, logf, preferred_element_type=F32, precision=lax.Precision.HIGHEST) + carry_ref[0:1, :]
    carry_ref[...] = jnp.broadcast_to(cum[tm - 1:, :], carry_ref.shape)
    hi, mid, lo = _split3(cum * LOG2E)
    src = hi + pltpu.roll(mid, nh, axis=1) + pltpu.roll(lo, 2 * nh, axis=1) + (lane == 3 * nh).astype(F32)
    src = src.astype(BF16)
    aug_q = _dot(src, pq_ref[...])
    aug_k = _dot(src, pk_ref[...])
    low = lane < hd
    one_at_hd = (lane == hd).astype(F32)
    for c in range(d // LANES):
        for o, dst in enumerate((qa_ref, ka_ref, va_ref)):
            blk = proj[:, o * d + c * LANES:o * d + (c + 1) * LANES]
            if o == 0:
                blk = blk * LOG2E
            for half in range(LANES // hd):
                hh = c * (LANES // hd) + half
                v = blk if half == 0 else pltpu.roll(blk, LANES - half * hd, axis=1)
                fill = (aug_q, aug_k)[o][:, hh * LANES:(hh + 1) * LANES] if o < 2 else one_at_hd
                dst[0, hh] = jnp.where(low, v, fill).astype(BF16)
    k_ref[0] = proj[:, d:2 * d]
    v_ref[0] = proj[:, 2 * d:3 * d]
    lf_ref[0] = logf[:, :nh]


def _qkv_prompt(x, gpre, w_all, bf_pad, pq, pk, nh, tm):
    bsz, n, d = x.shape
    head_spec = pl.BlockSpec((1, nh, tm, LANES), lambda b, t: (b, 0, t, 0))
    row_spec = pl.BlockSpec((1, tm, d), lambda b, t: (b, t, 0))
    head_shape = jax.ShapeDtypeStruct((bsz, nh, n, LANES), BF16)
    return pl.pallas_call(
        _qkv_prompt_kernel,
        grid=(bsz, n // tm),
        in_specs=[row_spec, _const_spec((1, d)), _const_spec(w_all.shape), _const_spec((1, LANES)),
                  _const_spec(pq.shape), _const_spec(pk.shape)],
        out_specs=[head_spec, head_spec, head_spec, row_spec, row_spec,
                   pl.BlockSpec((1, tm, nh), lambda b, t: (b, t, 0))],
        out_shape=[head_shape, head_shape, head_shape,
                   jax.ShapeDtypeStruct((bsz, n, d), F32), jax.ShapeDtypeStruct((bsz, n, d), F32),
                   jax.ShapeDtypeStruct((bsz, n, nh), F32)],
        scratch_shapes=[pltpu.VMEM((SUBLANES, LANES), F32)],
        compiler_params=_cparams("arbitrary", "arbitrary"),
        name="qkv_prompt",
    )(x, gpre, w_all, bf_pad, pq, pk)


def _attn_prompt_kernel(qe_ref, qo_ref, k_ref, v_ref, oe_ref, oo_ref, q_ref, m_ref, acc_ref, *, hd, nfull):
    e = pl.program_id(2)
    hpb, tq = qe_ref.shape[1], qe_ref.shape[2]
    tk = 2 * tq
    nt = (((1,), (1,)), ((), ()))
    q_ref[0] = qe_ref[0]
    q_ref[1] = qo_ref[0]
    m_ref[...] = jnp.full_like(m_ref, NEG_BIG)
    acc_ref[...] = jnp.zeros_like(acc_ref)

    def update(w, start, cols, mask_off):
        for j in range(hpb):
            s = lax.dot_general(q_ref[w, j], k_ref[0, j, pl.ds(start, cols), :], nt,
                                preferred_element_type=F32)
            if mask_off is not None:
                row = lax.broadcasted_iota(jnp.int32, (tq, cols), 0)
                col = lax.broadcasted_iota(jnp.int32, (tq, cols), 1)
                s = jnp.where(col <= row + mask_off, s, NEG_BIG)
            m_prev = m_ref[w, j]
            m_new = jnp.maximum(m_prev, jnp.max(s, axis=1, keepdims=True))
            pb = jnp.concatenate([jnp.exp2(s[:, c * LANES:(c + 1) * LANES] - m_new).astype(BF16)
                                  for c in range(cols // LANES)], axis=1)
            acc_ref[w, j] = (jnp.exp2(m_prev - m_new) * acc_ref[w, j]
                             + _dot(pb, v_ref[0, j, pl.ds(start, cols), :]))
            m_ref[w, j] = m_new

    for s in range(nfull):
        w = (s >= e).astype(jnp.int32)
        blk = jnp.where(s < e, s, s - e)
        update(w, pl.multiple_of(blk * tk, tk), tk, None)
    update(0, pl.multiple_of(e * tk, tk), tq, 0)
    update(1, pl.multiple_of((nfull - e) * tk, tk), tk, tq)

    lane = lax.broadcasted_iota(jnp.int32, (tq, LANES), 1)
    for w, o_ref in enumerate((oe_ref, oo_ref)):
        out = None
        for j in range(hpb):
            acc = acc_ref[w, j]
            oj = acc / jnp.sum(jnp.where(lane == hd, acc, 0.0), axis=1, keepdims=True)
            out = oj if j == 0 else jnp.where(lane < j * hd, out, pltpu.roll(oj, j * hd, axis=1))
        o_ref[0, 0] = out


def _attn_prompt(qa, ka, va, hd, tq):
    bsz, nh, n, _ = qa.shape
    hpb = LANES // hd
    nq = n // tq
    assert nq % 2 == 0
    half = nq // 2
    seq_spec = pl.BlockSpec((1, hpb, n, LANES), lambda b, h, e: (b, h, 0, 0))
    out_spec = pl.BlockSpec((1, 1, tq, LANES), lambda b, h, e: (b, e, 0, h))
    out_shape = jax.ShapeDtypeStruct((bsz, half, tq, nh * hd), F32)
    return pl.pallas_call(
        functools.partial(_attn_prompt_kernel, hd=hd, nfull=half - 1),
        grid=(bsz, nh // hpb, half),
        in_specs=[pl.BlockSpec((1, hpb, tq, LANES), lambda b, h, e: (b, h, 2 * e, 0)),
                  pl.BlockSpec((1, hpb, tq, LANES), lambda b, h, e: (b, h, nq - 1 - 2 * e, 0)),
                  seq_spec, seq_spec],
        out_specs=[out_spec, out_spec],
        out_shape=[out_shape, out_shape],
        scratch_shapes=[pltpu.VMEM((2, hpb, tq, LANES), BF16), pltpu.VMEM((2, hpb, tq, LANES), F32),
                        pltpu.VMEM((2, hpb, tq, LANES), F32)],
        compiler_params=_cparams("arbitrary", "arbitrary", "arbitrary"),
        name="attn_prompt",
    )(qa, qa, ka, va)


def _oproj_kernel(o_ref, x_ref, w_ref, g_ref, y_ref):
    y = _dot(o_ref[...].astype(BF16), w_ref[...])
    y_ref[...] = x_ref[...] + _rms(y, g_ref[...])


def _oproj_pair_kernel(oe_ref, oo_ref, x_ref, w_ref, g_ref, y_ref):
    even = pl.program_id(0) % 2 == 0
    o = jnp.where(even, oe_ref[0, 0], oo_ref[0, 0])
    y = _dot(o.astype(BF16), w_ref[...])
    y_ref[...] = x_ref[...] + _rms(y, g_ref[...])


def _oproj_pair(o_even, o_odd, x, w, g):
    bsz, half, tq, d = o_even.shape
    nq = 2 * half
    row = pl.BlockSpec((tq, d), lambda t: (t, 0))
    return pl.pallas_call(
        _oproj_pair_kernel,
        grid=(bsz * nq,),
        in_specs=[pl.BlockSpec((1, 1, tq, d), lambda t: (t // nq, (t % nq) // 2, 0, 0)),
                  pl.BlockSpec((1, 1, tq, d), lambda t: (t // nq, (nq - 1 - t % nq) // 2, 0, 0)),
                  row, _const_spec((d, d)), _const_spec((1, d))],
        out_specs=row,
        out_shape=jax.ShapeDtypeStruct(x.shape, F32),
        compiler_params=_cparams("arbitrary"),
        name="oproj_pair",
    )(o_even, o_odd, x, w, g)


def _oproj(o, x, w, g, tm):
    m, d = x.shape
    row = pl.BlockSpec((tm, d), lambda t: (t, 0))
    return pl.pallas_call(
        _oproj_kernel,
        grid=(m // tm,),
        in_specs=[row, row, _const_spec((d, d)), _const_spec((1, d))],
        out_specs=row,
        out_shape=jax.ShapeDtypeStruct((m, d), F32),
        compiler_params=_cparams("arbitrary"),
        name="oproj",
    )(o, x, w, g)


def _sconv_sample_kernel(x_ref, st_ref, gpre_ref, gpost_ref, win_ref, cw_ref, wout_ref,
                         y_ref, ns_ref, winb_ref, woutb_ref):
    d = x_ref.shape[-1]
    winb_ref[...] = win_ref[...].astype(BF16)
    woutb_ref[...] = wout_ref[...].astype(BF16)
    x = x_ref[...]
    h = _rms(x, gpre_ref[...]).astype(BF16)
    proj = _dot(h, winb_ref[...])
    b = proj[:, :d]
    ch = proj[:, d:2 * d] * proj[:, 2 * d:]
    s0 = st_ref[0]
    s1 = st_ref[1]
    w = cw_ref[...]
    z = s0 * w[0:1, :] + s1 * w[1:2, :] + ch * w[2:3, :]
    y = _dot((b * z).astype(BF16), woutb_ref[...])
    y_ref[...] = x + _rms(y, gpost_ref[...])
    ns_ref[0] = s1
    ns_ref[1] = ch


def _sconv_sample(x, st, gpre, gpost, w_in, conv_w, w_out):
    m, d = x.shape
    return pl.pallas_call(
        _sconv_sample_kernel,
        out_shape=[jax.ShapeDtypeStruct((m, d), F32), jax.ShapeDtypeStruct((CONV_W - 1, m, d), F32),
                   jax.ShapeDtypeStruct(w_in.shape, BF16), jax.ShapeDtypeStruct(w_out.shape, BF16)],
        compiler_params=pltpu.CompilerParams(vmem_limit_bytes=VMEM_LIMIT_BYTES),
        name="sconv_sample",
    )(x, st, gpre, gpost, w_in, conv_w, w_out)


def _ffn_sample_kernel(x_ref, st_ref, gpre_ref, gpost_ref, wg_ref, wu_ref, cw_ref, wd_ref,
                       y_ref, ns_ref, wgb_ref, wub_ref, wdb_ref, acc_ref):
    c = pl.program_id(0)

    @pl.when(c == 0)
    def _():
        acc_ref[...] = jnp.zeros_like(acc_ref)

    wgb_ref[...] = wg_ref[...].astype(BF16)
    wub_ref[...] = wu_ref[...].astype(BF16)
    wdb_ref[...] = wd_ref[...].astype(BF16)
    x = x_ref[...]
    h = _rms(x, gpre_ref[...]).astype(BF16)
    g = _dot(h, wgb_ref[...])
    u = _dot(h, wub_ref[...])
    s0 = st_ref[0]
    s1 = st_ref[1]
    w = cw_ref[...]
    gc = s0 * w[0:1, :] + s1 * w[1:2, :] + g * w[2:3, :]
    act = (gc * jax.nn.sigmoid(gc) * u).astype(BF16)
    acc_ref[...] += _dot(act, wdb_ref[...])
    ns_ref[0] = s1
    ns_ref[1] = g

    @pl.when(c == pl.num_programs(0) - 1)
    def _():
        y_ref[...] = x + _rms(acc_ref[...], gpost_ref[...])


def _ffn_sample(x, layer, st, gpre, gpost, wg, wu, conv_w, wd, fc):
    m, d = x.shape
    f = wg.shape[-1]
    full = pl.BlockSpec((m, d), lambda c: (0, 0))
    vec = pl.BlockSpec((None, 1, d), lambda c: (layer, 0, 0))
    return pl.pallas_call(
        _ffn_sample_kernel,
        grid=(f // fc,),
        in_specs=[full, pl.BlockSpec((None, CONV_W - 1, m, fc), lambda c: (layer, 0, 0, c)), vec, vec,
                  pl.BlockSpec((None, d, fc), lambda c: (layer, 0, c)),
                  pl.BlockSpec((None, d, fc), lambda c: (layer, 0, c)),
                  pl.BlockSpec((None, CONV_W, fc), lambda c: (layer, 0, c)),
                  pl.BlockSpec((None, fc, d), lambda c: (layer, c, 0))],
        out_specs=[full, pl.BlockSpec((CONV_W - 1, m, fc), lambda c: (0, 0, c)),
                   pl.BlockSpec((d, fc), lambda c: (0, c)), pl.BlockSpec((d, fc), lambda c: (0, c)),
                   pl.BlockSpec((fc, d), lambda c: (c, 0))],
        out_shape=[jax.ShapeDtypeStruct((m, d), F32), jax.ShapeDtypeStruct((CONV_W - 1, m, f), F32),
                   jax.ShapeDtypeStruct((d, f), BF16), jax.ShapeDtypeStruct((d, f), BF16),
                   jax.ShapeDtypeStruct((f, d), BF16)],
        scratch_shapes=[pltpu.VMEM((m, d), F32)],
        compiler_params=_cparams("arbitrary"),
        name="ffn_sample",
    )(x, st, gpre, gpost, wg, wu, conv_w, wd)


def _s5_sample_kernel(x_ref, h0r_ref, h0i_ref, gpre_ref, gpost_ref, win_ref, ptab_ref, wbu_ref,
                      wcr_ref, wci_ref, dsk_ref, wa_ref, wb_ref, y_ref, sr_ref, si_ref):
    nsg = wbu_ref.shape[0]
    cw = wbu_ref.shape[1]
    sw = wbu_ref.shape[2] // 2
    x = x_ref[...]
    h = _rms(x, gpre_ref[...]).astype(BF16)
    u = _dot(h, win_ref[...])
    ub = u.astype(BF16)
    ar = ptab_ref[0, 0:1, :]
    ai = ptab_ref[1, 0:1, :]
    ys = []
    for s in range(nsg):
        cs = slice(s * sw, (s + 1) * sw)
        bu = _dot(ub[:, s * cw:(s + 1) * cw], wbu_ref[s])
        hr, hi = _cmul_add(bu[:, :sw], bu[:, sw:], ar[:, cs], ai[:, cs], h0r_ref[:, cs], h0i_ref[:, cs])
        sr_ref[:, cs] = hr
        si_ref[:, cs] = hi
        ys.append(_dot(hr.astype(BF16), wcr_ref[s]) - _dot(hi.astype(BF16), wci_ref[s]))
    yy = jnp.concatenate(ys, axis=1) + dsk_ref[...] * u
    z = jax.nn.gelu(yy).astype(BF16)
    out = _dot(z, wa_ref[...]) * jax.nn.sigmoid(_dot(z, wb_ref[...]))
    y_ref[...] = x + _rms(out, gpost_ref[...])


def _s5_sample(x, h0r, h0i, gpre, gpost, w_in, ptab, w_bu, w_cr, w_ci, d_skip, wa, wb):
    m, d = x.shape
    ns = h0r.shape[1]
    return pl.pallas_call(
        _s5_sample_kernel,
        out_shape=[jax.ShapeDtypeStruct((m, d), F32),
                   jax.ShapeDtypeStruct((m, ns), F32), jax.ShapeDtypeStruct((m, ns), F32)],
        compiler_params=pltpu.CompilerParams(vmem_limit_bytes=VMEM_LIMIT_BYTES),
        name="s5_sample",
    )(x, h0r, h0i, gpre, gpost, w_in, ptab, w_bu, w_cr, w_ci, d_skip, wa, wb)


def _qkv_sample_kernel(x_ref, gpre_ref, w_ref, bf_ref, q_ref, k_ref, v_ref, lf_ref):
    d = x_ref.shape[-1]
    h = _rms(x_ref[...], gpre_ref[...]).astype(BF16)
    proj = _dot(h, w_ref[...])
    q_ref[...] = proj[:, :d]
    k_ref[...] = proj[:, d:2 * d]
    v_ref[...] = proj[:, 2 * d:3 * d]
    lf_ref[...] = _log_sigmoid(proj[:, 3 * d:] + bf_ref[...])


def _qkv_sample(x, gpre, w_all, bf_pad):
    m, d = x.shape
    row = jax.ShapeDtypeStruct((m, d), F32)
    return pl.pallas_call(
        _qkv_sample_kernel,
        out_shape=[row, row, row, jax.ShapeDtypeStruct((m, LANES), F32)],
        compiler_params=pltpu.CompilerParams(vmem_limit_bytes=VMEM_LIMIT_BYTES),
        name="qkv_sample",
    )(x, gpre, w_all, bf_pad)


def _attn_sample_kernel(pt_ref, q_ref, kn_ref, vn_ref, lfn_ref, *rest, npar):
    del pt_ref
    kc, vc, lfc = rest[:npar], rest[npar:2 * npar], rest[2 * npar:3 * npar]
    o_ref, m_ref, l_ref, acc_ref, suf_ref = rest[3 * npar:]
    st = pl.program_id(1)
    nst = pl.num_programs(1)
    nh, hd, rows = kc[0].shape[1], kc[0].shape[2], kc[0].shape[3]
    d = nh * hd
    nt = (((1,), (1,)), ((), ()))
    rid = lax.broadcasted_iota(jnp.int32, (nh, d), 0)
    lid = lax.broadcasted_iota(jnp.int32, (nh, d), 1)
    own = (lid >= rid * hd) & (lid < (rid + 1) * hd)
    qbd = jnp.where(own, q_ref[0], 0.0).astype(BF16)

    @pl.when(st == 0)
    def _():
        kb = jnp.broadcast_to(kn_ref[0], (LANES, d)).astype(BF16)
        m_ref[...] = lax.dot_general(qbd, kb, nt, preferred_element_type=F32)
        l_ref[...] = jnp.ones_like(l_ref)
        acc_ref[...] = jnp.broadcast_to(vn_ref[0], (nh, d))
        suf_ref[...] = lfn_ref[0]

    lf_all = jnp.concatenate([r[0] for r in lfc], axis=0)
    ri = lax.broadcasted_iota(jnp.int32, (rows, 2 * rows), 0)
    ci = lax.broadcasted_iota(jnp.int32, (rows, 2 * rows), 1)
    later = ((ri > ci) | (ci >= rows)).astype(F32)
    sums = jnp.dot(lf_all, later, preferred_element_type=F32, precision=lax.Precision.HIGHEST)
    carry = suf_ref[...]
    scores = []
    for g in range(npar):
        bias = sums[g * nh:(g + 1) * nh, :rows] + carry
        carry = carry + sums[g * nh:(g + 1) * nh, rows:]
        scores.append(_dot(qbd, kc[g][0].reshape(d, rows).astype(BF16)) + bias)
    suf_ref[...] = carry
    s = jnp.concatenate(scores, axis=1)
    m_prev = m_ref[...]
    m_new = jnp.maximum(m_prev, jnp.max(s, axis=1, keepdims=True))
    pr = jnp.exp(s - jnp.concatenate([m_new] * (npar * rows // LANES), axis=1))
    alpha = jnp.exp(m_prev - m_new)
    l_ref[...] = alpha * l_ref[...] + jnp.sum(pr, axis=1, keepdims=True)
    pb = pr.astype(BF16)
    pv = None
    for g in range(npar):
        part = lax.dot_general(pb[:, g * rows:(g + 1) * rows], vc[g][0].reshape(d, rows).astype(BF16), nt,
                               preferred_element_type=F32)
        pv = part if pv is None else pv + part
    acc_ref[...] = jnp.concatenate([alpha] * (d // LANES), axis=1) * acc_ref[...] + pv
    m_ref[...] = m_new

    @pl.when(st == nst - 1)
    def _():
        out = acc_ref[...] / jnp.concatenate([l_ref[...]] * (d // LANES), axis=1)
        o_ref[0] = jnp.sum(jnp.where(own, out, 0.0), axis=0, keepdims=True)


def _attn_sample(page_table, q, k_new, v_new, lf_new, cache_kt, cache_vt, cache_lft, npar):
    m, _, d = q.shape
    npg = page_table.shape[1]
    _, nh, hd, rows = cache_kt.shape
    assert rows == LANES and npg % npar == 0

    def page_idx(g):
        return lambda b, s, pt: (pt[b, npg - 1 - (s * npar + g)], 0, 0, 0)

    def lf_idx(g):
        return lambda b, s, pt: (pt[b, npg - 1 - (s * npar + g)], 0, 0)

    tok = pl.BlockSpec((1, 1, d), lambda b, s, pt: (b, 0, 0))
    grid_spec = pltpu.PrefetchScalarGridSpec(
        num_scalar_prefetch=1,
        grid=(m, npg // npar),
        in_specs=([tok, tok, tok, pl.BlockSpec((1, nh, LANES), lambda b, s, pt: (b, 0, 0))]
                  + [pl.BlockSpec((1, nh, hd, rows), page_idx(g)) for g in range(npar)]
                  + [pl.BlockSpec((1, nh, hd, rows), page_idx(g)) for g in range(npar)]
                  + [pl.BlockSpec((1, nh, rows), lf_idx(g)) for g in range(npar)]),
        out_specs=tok,
        scratch_shapes=[pltpu.VMEM((nh, LANES), F32), pltpu.VMEM((nh, LANES), F32),
                        pltpu.VMEM((nh, d), F32), pltpu.VMEM((nh, LANES), F32)],
    )
    return pl.pallas_call(
        functools.partial(_attn_sample_kernel, npar=npar),
        grid_spec=grid_spec,
        out_shape=jax.ShapeDtypeStruct((m, 1, d), F32),
        compiler_params=_cparams("arbitrary", "arbitrary"),
        name="attn_sample",
    )(page_table, q, k_new, v_new, lf_new, *([cache_kt] * npar), *([cache_vt] * npar), *([cache_lft] * npar))


TM_MIX = 256
TM_S5 = 256
TM_FFN = 512
FC_FFN = 256
FC_FFN_SAMPLE = 256
TQ_ATTN = 512
PAGES_PER_STEP = 16


def kernel(x_prompt, x_sample, state_sconv_l0, state_ssm_re_l1, state_ssm_im_l1, cache_k_l2, cache_v_l2, cache_logf_l2, state_sconv_l3, state_ffn_conv, page_table, sc_w_in_l0, sc_conv_w_l0, sc_w_out_l0, s5_w_in_l1, s5_lambda_re_l1, s5_lambda_im_l1, s5_log_dt_l1, s5_b_re_l1, s5_b_im_l1, s5_c_re_l1, s5_c_im_l1, s5_d_l1, s5_glu_wa_l1, s5_glu_wb_l1, fox_w_qkvf_l2, fox_b_f_l2, fox_w_o_l2, sc_w_in_l3, sc_conv_w_l3, sc_w_out_l3, norm_mix_pre, norm_mix_post, norm_ffn_pre, norm_ffn_post, ffn_w_gate, ffn_w_up, ffn_conv_w, ffn_w_down):
    bp, n, d = x_prompt.shape
    m = x_sample.shape[0]
    nh = fox_b_f_l2.shape[0]
    hd = d // nh
    f = ffn_w_gate.shape[-1]
    g, p = s5_lambda_re_l1.shape
    ns = g * p
    bf = lambda w: w.astype(BF16)
    row = lambda v: v.reshape(1, -1)

    tm_mix = min(TM_MIX, n)
    tm_s5 = min(TM_S5, n)
    tm_ffn = min(TM_FFN, n)
    tq = min(TQ_ATTN, n)
    fc = min(FC_FFN, f)
    fc_sample = max(c for c in range(LANES, min(FC_FFN_SAMPLE, f) + 1, LANES) if f % c == 0)

    ptab, coef, w_bu, w_cr, w_ci = _s5_tables(s5_lambda_re_l1, s5_lambda_im_l1, s5_log_dt_l1,
                                              s5_b_re_l1, s5_b_im_l1, s5_c_re_l1, s5_c_im_l1,
                                              tm_s5 // SUBLANES)
    scale = hd ** -0.5
    w_qkvf = jnp.concatenate([fox_w_qkvf_l2[:, :d] * scale, fox_w_qkvf_l2[:, d:],
                              jnp.zeros((d, LANES - nh), F32)], axis=1).astype(BF16)
    bf_pad = jnp.concatenate([fox_b_f_l2, jnp.zeros((LANES - nh,), F32)]).reshape(1, LANES)
    pq, pk = _fox_placement(nh)
    s5_tail = (w_bu, w_cr, w_ci, row(s5_d_l1), bf(s5_glu_wa_l1), bf(s5_glu_wb_l1))
    s5_w_in = bf(s5_w_in_l1)
    w_o = bf(fox_w_o_l2)
    ffn_g = (norm_ffn_pre[:, None, :], norm_ffn_post[:, None, :])

    xs = x_sample.reshape(m, d)
    ffn_s = []
    ffn_wb = []
    st_ffn = jnp.swapaxes(state_ffn_conv, 1, 2)

    def ffn_s_layer(xs, i):
        y, ns_, wgb, wub, wdb = _ffn_sample(xs, i, st_ffn, *ffn_g, ffn_w_gate, ffn_w_up, ffn_conv_w,
                                            ffn_w_down, fc_sample)
        ffn_s.append(jnp.swapaxes(ns_, 0, 1))
        ffn_wb.append((wgb, wub, wdb))
        return y

    xs, sc0_s, w_in0, w_out0 = _sconv_sample(xs, jnp.swapaxes(state_sconv_l0, 0, 1), row(norm_mix_pre[0]),
                                             row(norm_mix_post[0]), sc_w_in_l0, sc_conv_w_l0, sc_w_out_l0)
    xs = ffn_s_layer(xs, 0)
    xs, sr_s, si_s = _s5_sample(xs, state_ssm_re_l1.reshape(m, ns), state_ssm_im_l1.reshape(m, ns),
                                row(norm_mix_pre[1]), row(norm_mix_post[1]), s5_w_in, ptab, *s5_tail)
    xs = ffn_s_layer(xs, 1)
    q_s, k_s, v_s, lf_s = _qkv_sample(xs, row(norm_mix_pre[2]), w_qkvf, bf_pad)
    npg = page_table.shape[1]
    npar = max(c for c in range(1, PAGES_PER_STEP + 1) if npg % c == 0)
    o_s = _attn_sample(page_table, q_s.reshape(m, 1, d), k_s.reshape(m, 1, d), v_s.reshape(m, 1, d),
                       jnp.broadcast_to(lf_s[:, :nh, None], (m, nh, LANES)),
                       jnp.transpose(cache_k_l2, (0, 2, 3, 1)), jnp.transpose(cache_v_l2, (0, 2, 3, 1)),
                       jnp.transpose(cache_logf_l2, (0, 2, 1)), npar)
    xs = _oproj(o_s.reshape(m, d), xs, w_o, row(norm_mix_post[2]), m)
    xs = ffn_s_layer(xs, 2)
    xs, sc3_s, w_in3, w_out3 = _sconv_sample(xs, jnp.swapaxes(state_sconv_l3, 0, 1), row(norm_mix_pre[3]),
                                             row(norm_mix_post[3]), sc_w_in_l3, sc_conv_w_l3, sc_w_out_l3)
    xs = ffn_s_layer(xs, 3)

    def ffn_p_layer(xp, i):
        wgb, wub, wdb = ffn_wb[i]
        y, fb = _ffn_prompt(xp, i, *ffn_g, wgb, wub, ffn_conv_w, wdb, tm_ffn, fc)
        ffn_p.append(fb)
        return y

    xp = x_prompt
    ffn_p = []
    xp, sc0_p = _sconv_prompt(xp, row(norm_mix_pre[0]), row(norm_mix_post[0]), w_in0, sc_conv_w_l0, w_out0, tm_ffn)
    xp = ffn_p_layer(xp, 0)
    zeros_state = jnp.zeros((bp, 1, ns), F32)
    xp, sr_p, si_p = _s5_prompt(xp, row(norm_mix_pre[1]), row(norm_mix_post[1]), s5_w_in, ptab, coef, *s5_tail,
                                zeros_state, zeros_state, tm_s5)
    xp = ffn_p_layer(xp, 1)
    qa, ka, va, k_p, v_p, lf_p = _qkv_prompt(xp, row(norm_mix_pre[2]), w_qkvf, bf_pad, pq, pk, nh, tm_mix)
    o_even, o_odd = _attn_prompt(qa, ka, va, hd, tq)
    xp = _oproj_pair(o_even, o_odd, xp.reshape(bp * n, d), w_o, row(norm_mix_post[2])).reshape(bp, n, d)
    xp = ffn_p_layer(xp, 2)
    xp, sc3_p = _sconv_prompt(xp, row(norm_mix_pre[3]), row(norm_mix_post[3]), w_in3, sc_conv_w_l3, w_out3, tm_ffn)
    xp = ffn_p_layer(xp, 3)

    return (xp, xs.reshape(m, 1, d),
            sc0_p, jnp.swapaxes(sc0_s, 0, 1),
            sr_p.reshape(bp, g, p), sr_s.reshape(m, g, p), si_p.reshape(bp, g, p), si_s.reshape(m, g, p),
            k_p.reshape(bp, n, nh, hd), k_s.reshape(m, 1, nh, hd),
            v_p.reshape(bp, n, nh, hd), v_s.reshape(m, 1, nh, hd),
            lf_p, lf_s[:, :nh].reshape(m, 1, nh),
            sc3_p, jnp.swapaxes(sc3_s, 0, 1),
            jnp.stack(ffn_p), jnp.stack(ffn_s))
```

```python
import functools
import math

import jax
import jax.numpy as jnp
from jax import lax
from jax.experimental import pallas as pl
from jax.experimental.pallas import tpu as pltpu

F32 = jnp.float32
BF16 = jnp.bfloat16

RMS_EPS = 1e-6
CONV_W = 3
S5_GROUP = 16
S5_STATE = 64
FOX_HEADS = 16
PAGE_SIZE = 128

LANES = 128
SUBLANES = 8
MXU_DIM = 256
VMEM_LIMIT_BYTES = 56 * 1024 * 1024

NEG_BIG = -1e30
LOG2E = math.log2(math.e)
SCAN_UNROLL = True


def _cparams(*sem):
    return pltpu.CompilerParams(dimension_semantics=sem, vmem_limit_bytes=VMEM_LIMIT_BYTES)


def _const_spec(shape):
    nd = len(shape)
    return pl.BlockSpec(shape, lambda *_: (0,) * nd, pipeline_mode=pl.Buffered(1))


def _rms(x, g):
    ms = jnp.mean(x * x, axis=-1, keepdims=True)
    return x * lax.rsqrt(ms + RMS_EPS) * g


def _dot(a, b):
    return jnp.dot(a, b, preferred_element_type=F32)


def _shift_rows(cur, prev_tail, k):
    rolled = pltpu.roll(cur, k, axis=0)
    rid = lax.broadcasted_iota(jnp.int32, (SUBLANES, cur.shape[1]), 0)
    head = jnp.where(rid < k, pltpu.roll(prev_tail, k, axis=0), rolled[:SUBLANES, :])
    return jnp.concatenate([head, rolled[SUBLANES:, :]], axis=0)


def _causal_conv3(cur, prev_tail, w):
    x1 = _shift_rows(cur, prev_tail, 1)
    x2 = _shift_rows(cur, prev_tail, 2)
    return x2 * w[0:1, :] + x1 * w[1:2, :] + cur * w[2:3, :]


def _sconv_prompt_kernel(x_ref, gpre_ref, gpost_ref, win_ref, cw_ref, wout_ref,
                         y_ref, st_ref, tail_ref):
    t = pl.program_id(1)
    nt = pl.num_programs(1)
    d = x_ref.shape[-1]
    tm = x_ref.shape[1]

    @pl.when(t == 0)
    def _():
        tail_ref[...] = jnp.zeros_like(tail_ref)

    x = x_ref[0]
    h = _rms(x, gpre_ref[...]).astype(BF16)
    proj = _dot(h, win_ref[...])
    b = proj[:, :d]
    ch = proj[:, d:2 * d] * proj[:, 2 * d:]
    z = _causal_conv3(ch, tail_ref[...], cw_ref[...])
    y = _dot((b * z).astype(BF16), wout_ref[...])
    y_ref[0] = x + _rms(y, gpost_ref[...])
    tail_ref[...] = ch[tm - SUBLANES:, :]

    @pl.when(t == nt - 1)
    def _():
        st_ref[0] = ch[tm - (CONV_W - 1):, :]


def _sconv_prompt(x, gpre, gpost, w_in, conv_w, w_out, tm):
    bsz, n, d = x.shape
    return pl.pallas_call(
        _sconv_prompt_kernel,
        grid=(bsz, n // tm),
        in_specs=[
            pl.BlockSpec((1, tm, d), lambda b, t: (b, t, 0)),
            _const_spec((1, d)), _const_spec((1, d)),
            _const_spec((d, 3 * d)), _const_spec((CONV_W, d)), _const_spec((d, d)),
        ],
        out_specs=[
            pl.BlockSpec((1, tm, d), lambda b, t: (b, t, 0)),
            pl.BlockSpec((1, CONV_W - 1, d), lambda b, t: (b, 0, 0)),
        ],
        out_shape=[jax.ShapeDtypeStruct((bsz, n, d), F32),
                   jax.ShapeDtypeStruct((bsz, CONV_W - 1, d), F32)],
        scratch_shapes=[pltpu.VMEM((SUBLANES, d), F32)],
        compiler_params=_cparams("arbitrary", "arbitrary"),
        name="sconv_prompt",
    )(x, gpre, gpost, w_in, conv_w, w_out)


def _ffn_prompt_kernel(x_ref, gpre_ref, gpost_ref, wg_ref, wu_ref, cw_ref, wd_ref,
                       y_ref, st_ref, tail_ref, act_ref, *, fc):
    t = pl.program_id(1)
    nt = pl.num_programs(1)
    tm = x_ref.shape[1]
    f = wg_ref.shape[1]

    @pl.when(t == 0)
    def _():
        tail_ref[...] = jnp.zeros_like(tail_ref)

    x = x_ref[0]
    h = _rms(x, gpre_ref[...]).astype(BF16)
    for c in range(f // fc):
        cs = slice(c * fc, (c + 1) * fc)
        g = _dot(h, wg_ref[:, cs])
        u = _dot(h, wu_ref[:, cs])
        gc = _causal_conv3(g, tail_ref[:, cs], cw_ref[:, cs])
        act_ref[:, cs] = (gc * jax.nn.sigmoid(gc) * u).astype(BF16)
        tail_ref[:, cs] = g[tm - SUBLANES:, :]
    y = _dot(act_ref[...], wd_ref[...])
    y_ref[0] = x + _rms(y, gpost_ref[...])

    @pl.when(t == nt - 1)
    def _():
        st_ref[0] = tail_ref[SUBLANES - (CONV_W - 1):, :]


def _layer_spec(shape, layer):
    nd = len(shape)
    return pl.BlockSpec((None,) + tuple(shape), lambda *_: (layer,) + (0,) * nd, pipeline_mode=pl.Buffered(1))


def _ffn_prompt(x, layer, gpre, gpost, wg, wu, conv_w, wd, tm, fc):
    bsz, n, d = x.shape
    f = wg.shape[-1]
    return pl.pallas_call(
        functools.partial(_ffn_prompt_kernel, fc=fc),
        grid=(bsz, n // tm),
        in_specs=[
            pl.BlockSpec((1, tm, d), lambda b, t: (b, t, 0)),
            _layer_spec((1, d), layer), _layer_spec((1, d), layer),
            _const_spec((d, f)), _const_spec((d, f)),
            _layer_spec((CONV_W, f), layer), _const_spec((f, d)),
        ],
        out_specs=[
            pl.BlockSpec((1, tm, d), lambda b, t: (b, t, 0)),
            pl.BlockSpec((1, CONV_W - 1, f), lambda b, t: (b, 0, 0)),
        ],
        out_shape=[jax.ShapeDtypeStruct((bsz, n, d), F32),
                   jax.ShapeDtypeStruct((bsz, CONV_W - 1, f), F32)],
        scratch_shapes=[pltpu.VMEM((SUBLANES, f), F32), pltpu.VMEM((tm, f), BF16)],
        compiler_params=_cparams("arbitrary", "arbitrary"),
        name="ffn_prompt",
    )(x, gpre, gpost, wg, wu, conv_w, wd)


def _s5_prep_kernel(lr_ref, li_ref, ldt_ref, bre_ref, bim_ref,
                    pwr_ref, pwi_ref, sgr_ref, sgi_ref, bbr_ref, bbi_ref):
    lr = lr_ref[...]
    li = li_ref[...]
    dt = jnp.exp(ldt_ref[...])
    mag = jnp.exp(lr * dt)
    abar_r = mag * jnp.cos(li * dt)
    abar_i = mag * jnp.sin(li * dt)

    def powers(ref_r, ref_i, br, bi):
        cr, ci = br, bi
        for k in range(ref_r.shape[0]):
            ref_r[k] = cr
            ref_i[k] = ci
            cr, ci = cr * br - ci * bi, cr * bi + ci * br

    seg = pwr_ref.shape[0]
    powers(pwr_ref, pwi_ref, abar_r, abar_i)
    powers(sgr_ref, sgi_ref, pwr_ref[seg - 1], pwi_ref[seg - 1])
    den = lr * lr + li * li
    nr = abar_r - 1.0
    kr = (nr * lr + abar_i * li) / den
    ki = (abar_i * lr - nr * li) / den
    bre = bre_ref[...]
    bim = bim_ref[...]
    bbr_ref[...] = kr[:, None, :] * bre - ki[:, None, :] * bim
    bbi_ref[...] = kr[:, None, :] * bim + ki[:, None, :] * bre


def _s5_prep(lam_re, lam_im, log_dt, b_re_t, b_im_t, seg):
    g, p = lam_re.shape
    n = b_re_t.shape[1]
    return pl.pallas_call(
        _s5_prep_kernel,
        out_shape=[jax.ShapeDtypeStruct((seg, g, p), F32), jax.ShapeDtypeStruct((seg, g, p), F32),
                   jax.ShapeDtypeStruct((SUBLANES, g, p), F32), jax.ShapeDtypeStruct((SUBLANES, g, p), F32),
                   jax.ShapeDtypeStruct((g, n, p), F32), jax.ShapeDtypeStruct((g, n, p), F32)],
        name="s5_prep",
    )(lam_re, lam_im, log_dt.reshape(g, 1), b_re_t, b_im_t)


def _s5_block_diag(w, sgroups):
    g, a, b = w.shape
    w4 = w.reshape(g // sgroups, sgroups, a, b)
    eye = jnp.eye(sgroups, dtype=w.dtype)
    return jnp.einsum('sgab,gh->sgahb', w4, eye).reshape(g // sgroups, sgroups * a, sgroups * b)


def _s5_tables(lam_re, lam_im, log_dt, b_re, b_im, c_re, c_im, seg):
    g, p = lam_re.shape
    pwr, pwi, sgr, sgi, bbr, bbi = _s5_prep(lam_re, lam_im, log_dt,
                                            jnp.swapaxes(b_re, 1, 2), jnp.swapaxes(b_im, 1, 2), seg)
    sg = MXU_DIM // S5_GROUP
    w_bu = jnp.concatenate([_s5_block_diag(bbr, sg), _s5_block_diag(bbi, sg)], axis=-1).astype(BF16)
    w_cr = _s5_block_diag(jnp.swapaxes(c_re, 1, 2), sg).astype(BF16)
    w_ci = _s5_block_diag(jnp.swapaxes(c_im, 1, 2), sg).astype(BF16)
    ptab = jnp.stack([pwr.reshape(seg, g * p)[:1], pwi.reshape(seg, g * p)[:1]])
    sgr = sgr.reshape(SUBLANES, g * p)
    sgi = sgi.reshape(SUBLANES, g * p)
    rows = jnp.arange(SUBLANES)[:, None]
    steps = []
    for s in (1, 2, 4):
        steps.append(jnp.where(rows >= s, sgr[s - 1][None, :], 0.0))
        steps.append(jnp.where(rows >= s, sgi[s - 1][None, :], 0.0))
    coef = jnp.stack(steps + [sgr, sgi])
    return ptab, coef, w_bu, w_cr, w_ci


def _cmul_add(xr, xi, ar, ai, sr, si):
    return xr + ar * sr - ai * si, xi + ar * si + ai * sr


def _s5_prompt_kernel(x_ref, gpre_ref, gpost_ref, perm_ref, win_ref, ptab_ref, coef_ref, wbu_ref, wcr_ref,
                      wci_ref, dsk_ref, wa_ref, wb_ref, h0r_ref, h0i_ref,
                      y_ref, sr_ref, si_ref, br_ref, bi_ref, cr_ref, ci_ref):
    t = pl.program_id(1)
    tm = x_ref.shape[1]
    seg = tm // SUBLANES
    nsg = wbu_ref.shape[0]
    cw = wbu_ref.shape[1]
    sw = wbu_ref.shape[2] // 2

    @pl.when(t == 0)
    def _():
        cr_ref[...] = jnp.broadcast_to(h0r_ref[0], cr_ref.shape)
        ci_ref[...] = jnp.broadcast_to(h0i_ref[0], ci_ref.shape)

    x = x_ref[0]
    h = _dot(perm_ref[0], _rms(x, gpre_ref[...]).astype(BF16)).astype(BF16)
    u = _dot(h, win_ref[...])
    ub = u.astype(BF16)
    lc = sw
    for c in range(nsg):
        bu = _dot(ub[:, c * cw:(c + 1) * cw], wbu_ref[c])
        br_ref[:, c * lc:(c + 1) * lc] = bu[:, :sw]
        bi_ref[:, c * lc:(c + 1) * lc] = bu[:, sw:]

    rid = lax.broadcasted_iota(jnp.int32, (SUBLANES, lc), 0)
    ys = []
    for c in range(nsg):
        cs = slice(c * lc, (c + 1) * lc)
        ar = ptab_ref[0, 0:1, cs]
        ai = ptab_ref[1, 0:1, cs]

        def local_step(j, hh, cs=cs, ar=ar, ai=ai):
            r0 = pl.multiple_of(j * SUBLANES, SUBLANES)
            nr, ni = _cmul_add(br_ref[pl.ds(r0, SUBLANES), cs], bi_ref[pl.ds(r0, SUBLANES), cs],
                               ar, ai, hh[0], hh[1])
            br_ref[pl.ds(r0, SUBLANES), cs] = nr
            bi_ref[pl.ds(r0, SUBLANES), cs] = ni
            return nr, ni

        zero = jnp.zeros((SUBLANES, lc), F32)
        er, ei = lax.fori_loop(0, seg, local_step, (zero, zero), unroll=SCAN_UNROLL)
        for j, sh in enumerate((1, 2, 4)):
            er, ei = _cmul_add(er, ei, coef_ref[2 * j, :, cs], coef_ref[2 * j + 1, :, cs],
                               pltpu.roll(er, sh, axis=0), pltpu.roll(ei, sh, axis=0))
        er, ei = _cmul_add(er, ei, coef_ref[6, :, cs], coef_ref[7, :, cs], cr_ref[:, cs], ci_ref[:, cs])
        inr = jnp.where(rid == 0, cr_ref[:, cs], pltpu.roll(er, 1, axis=0))
        ini = jnp.where(rid == 0, ci_ref[:, cs], pltpu.roll(ei, 1, axis=0))
        cr_ref[:, cs] = jnp.broadcast_to(er[SUBLANES - 1:, :], (SUBLANES, lc))
        ci_ref[:, cs] = jnp.broadcast_to(ei[SUBLANES - 1:, :], (SUBLANES, lc))

        def carry_step(j, cc, cs=cs, ar=ar, ai=ai):
            r0 = pl.multiple_of(j * SUBLANES, SUBLANES)
            nr = ar * cc[0] - ai * cc[1]
            ni = ar * cc[1] + ai * cc[0]
            br_ref[pl.ds(r0, SUBLANES), cs] = br_ref[pl.ds(r0, SUBLANES), cs] + nr
            bi_ref[pl.ds(r0, SUBLANES), cs] = bi_ref[pl.ds(r0, SUBLANES), cs] + ni
            return nr, ni

        lax.fori_loop(0, seg, carry_step, (inr, ini), unroll=SCAN_UNROLL)
        ys.append(_dot(br_ref[:, cs].astype(BF16), wcr_ref[c]) - _dot(bi_ref[:, cs].astype(BF16), wci_ref[c]))

    yy = jnp.concatenate(ys, axis=1) + dsk_ref[...] * u
    z = _dot(perm_ref[1], jax.nn.gelu(yy).astype(BF16)).astype(BF16)
    out = _dot(z, wa_ref[...]) * jax.nn.sigmoid(_dot(z, wb_ref[...]))
    y_ref[0] = x + _rms(out, gpost_ref[...])
    sr_ref[0] = cr_ref[0:1, :]
    si_ref[0] = ci_ref[0:1, :]


def _s5_prompt(x, gpre, gpost, w_in, ptab, coef, w_bu, w_cr, w_ci, d_skip, wa, wb, h0r, h0i, tm):
    bsz, n, d = x.shape
    ns = coef.shape[-1]
    seg = tm // SUBLANES
    src = (jnp.arange(tm) % SUBLANES) * seg + jnp.arange(tm) // SUBLANES
    gather = (src[:, None] == jnp.arange(tm)[None, :])
    perm = jnp.stack([gather, gather.T]).astype(BF16)
    return pl.pallas_call(
        _s5_prompt_kernel,
        grid=(bsz, n // tm),
        in_specs=[
            pl.BlockSpec((1, tm, d), lambda b, t: (b, t, 0)),
            _const_spec((1, d)), _const_spec((1, d)), _const_spec(perm.shape), _const_spec((d, d)),
            _const_spec(ptab.shape), _const_spec(coef.shape),
            _const_spec(w_bu.shape), _const_spec(w_cr.shape), _const_spec(w_ci.shape),
            _const_spec((1, d)), _const_spec((d, d)), _const_spec((d, d)),
            pl.BlockSpec((1, 1, ns), lambda b, t: (b, 0, 0)),
            pl.BlockSpec((1, 1, ns), lambda b, t: (b, 0, 0)),
        ],
        out_specs=[
            pl.BlockSpec((1, tm, d), lambda b, t: (b, t, 0)),
            pl.BlockSpec((1, 1, ns), lambda b, t: (b, 0, 0)),
            pl.BlockSpec((1, 1, ns), lambda b, t: (b, 0, 0)),
        ],
        out_shape=[jax.ShapeDtypeStruct((bsz, n, d), F32),
                   jax.ShapeDtypeStruct((bsz, 1, ns), F32), jax.ShapeDtypeStruct((bsz, 1, ns), F32)],
        scratch_shapes=[pltpu.VMEM((tm, ns), F32), pltpu.VMEM((tm, ns), F32),
                        pltpu.VMEM((SUBLANES, ns), F32), pltpu.VMEM((SUBLANES, ns), F32)],
        compiler_params=_cparams("arbitrary", "arbitrary"),
        name="s5_prompt",
    )(x, gpre, gpost, perm, w_in, ptab, coef, w_bu, w_cr, w_ci, d_skip, wa, wb, h0r, h0i)


def _log_sigmoid(x):
    return -(jnp.maximum(-x, 0.0) + jnp.log1p(jnp.exp(-jnp.abs(x))))


def _split3(x):
    hi = x.astype(BF16).astype(F32)
    r = x - hi
    mid = r.astype(BF16).astype(F32)
    lo = r - mid
    return hi, mid, lo


def _fox_placement(nh):
    hd = LANES // 2
    pq = [[0.0] * (nh * LANES) for _ in range(LANES)]
    pk = [[0.0] * (nh * LANES) for _ in range(LANES)]
    one = 3 * nh
    for h in range(nh):
        base = h * LANES + hd
        for j in range(3):
            pq[j * nh + h][base + j] = 1.0
            pq[one][base + 3 + j] = 1.0
            pk[one][base + j] = 1.0
            pk[j * nh + h][base + 3 + j] = -1.0
    return jnp.array(pq, BF16), jnp.array(pk, BF16)


def _qkv_prompt_kernel(x_ref, gpre_ref, w_ref, bf_ref, pq_ref, pk_ref,
                       qa_ref, ka_ref, va_ref, k_ref, v_ref, lf_ref, carry_ref):
    t = pl.program_id(1)
    tm = x_ref.shape[1]
    d = x_ref.shape[-1]
    nh = qa_ref.shape[1]
    hd = d // nh

    @pl.when(t == 0)
    def _():
        carry_ref[...] = jnp.zeros_like(carry_ref)

    x = x_ref[0]
    h = _rms(x, gpre_ref[...]).astype(BF16)
    lane = lax.broadcasted_iota(jnp.int32, (tm, LANES), 1)
    logf = jnp.where(lane < nh, _log_sigmoid(_dot(h, w_ref[:, 3 * d:]) + bf_ref[...]), 0.0)
    proj = _dot(h, w_ref[:, :3 * d])
    ri = lax.broadcasted_iota(jnp.int32, (tm, tm), 0)
    ci = lax.broadcasted_iota(jnp.int32, (tm, tm), 1)
    tri = (ci <= ri).astype(F32)
    cum = jnp.dot(tri, logf, preferred_element_type=F32, precision=lax.Precision.HIGHEST) + carry_ref[0:1, :]
    carry_ref[...] = jnp.broadcast_to(cum[tm - 1:, :], carry_ref.shape)
    hi, mid, lo = _split3(cum * LOG2E)
    src = hi + pltpu.roll(mid, nh, axis=1) + pltpu.roll(lo, 2 * nh, axis=1) + (lane == 3 * nh).astype(F32)
    src = src.astype(BF16)
    aug_q = _dot(src, pq_ref[...])
    aug_k = _dot(src, pk_ref[...])
    low = lane < hd
    one_at_hd = (lane == hd).astype(F32)
    for c in range(d // LANES):
        for o, dst in enumerate((qa_ref, ka_ref, va_ref)):
            blk = proj[:, o * d + c * LANES:o * d + (c + 1) * LANES]
            if o == 0:
                blk = blk * LOG2E
            for half in range(LANES // hd):
                hh = c * (LANES // hd) + half
                v = blk if half == 0 else pltpu.roll(blk, LANES - half * hd, axis=1)
                fill = (aug_q, aug_k)[o][:, hh * LANES:(hh + 1) * LANES] if o < 2 else one_at_hd
                dst[0, hh] = jnp.where(low, v, fill).astype(BF16)
    k_ref[0] = proj[:, d:2 * d]
    v_ref[0] = proj[:, 2 * d:3 * d]
    lf_ref[0] = logf[:, :nh]


def _qkv_prompt(x, gpre, w_all, bf_pad, pq, pk, nh, tm):
    bsz, n, d = x.shape
    head_spec = pl.BlockSpec((1, nh, tm, LANES), lambda b, t: (b, 0, t, 0))
    row_spec = pl.BlockSpec((1, tm, d), lambda b, t: (b, t, 0))
    head_shape = jax.ShapeDtypeStruct((bsz, nh, n, LANES), BF16)
    return pl.pallas_call(
        _qkv_prompt_kernel,
        grid=(bsz, n // tm),
        in_specs=[row_spec, _const_spec((1, d)), _const_spec(w_all.shape), _const_spec((1, LANES)),
                  _const_spec(pq.shape), _const_spec(pk.shape)],
        out_specs=[head_spec, head_spec, head_spec, row_spec, row_spec,
                   pl.BlockSpec((1, tm, nh), lambda b, t: (b, t, 0))],
        out_shape=[head_shape, head_shape, head_shape,
                   jax.ShapeDtypeStruct((bsz, n, d), F32), jax.ShapeDtypeStruct((bsz, n, d), F32),
                   jax.ShapeDtypeStruct((bsz, n, nh), F32)],
        scratch_shapes=[pltpu.VMEM((SUBLANES, LANES), F32)],
        compiler_params=_cparams("arbitrary", "arbitrary"),
        name="qkv_prompt",
    )(x, gpre, w_all, bf_pad, pq, pk)


def _attn_prompt_kernel(qe_ref, qo_ref, k_ref, v_ref, oe_ref, oo_ref, q_ref, m_ref, acc_ref, *, hd, nfull):
    e = pl.program_id(2)
    hpb, tq = qe_ref.shape[1], qe_ref.shape[2]
    tk = 2 * tq
    nt = (((1,), (1,)), ((), ()))
    q_ref[0] = qe_ref[0]
    q_ref[1] = qo_ref[0]
    m_ref[...] = jnp.full_like(m_ref, NEG_BIG)
    acc_ref[...] = jnp.zeros_like(acc_ref)

    def scores(w, start, cols, mask_off):
        del mask_off
        return [lax.dot_general(q_ref[w, j], k_ref[0, j, pl.ds(start, cols), :], nt,
                                preferred_element_type=F32) for j in range(hpb)]

    def update(ss, w, start, cols, mask_off):
        for j, s in enumerate(ss):
            if mask_off is not None:
                row = lax.broadcasted_iota(jnp.int32, (tq, cols), 0)
                col = lax.broadcasted_iota(jnp.int32, (tq, cols), 1)
                s = jnp.where(col <= row + mask_off, s, NEG_BIG)
            m_prev = m_ref[w, j]
            m_new = jnp.maximum(m_prev, jnp.max(s, axis=1, keepdims=True))
            pb = jnp.concatenate([jnp.exp2(s[:, c * LANES:(c + 1) * LANES] - m_new).astype(BF16)
                                  for c in range(cols // LANES)], axis=1)
            acc_ref[w, j] = (jnp.exp2(m_prev - m_new) * acc_ref[w, j]
                             + _dot(pb, v_ref[0, j, pl.ds(start, cols), :]))
            m_ref[w, j] = m_new

    slots = []
    for s in range(nfull):
        w = (s >= e).astype(jnp.int32)
        blk = jnp.where(s < e, s, s - e)
        slots.append((w, pl.multiple_of(blk * tk, tk), tk, None))
    slots.append((0, pl.multiple_of(e * tk, tk), tq, 0))
    slots.append((1, pl.multiple_of((nfull - e) * tk, tk), tk, tq))
    ahead = [scores(*slots[i]) for i in range(min(SCORES_AHEAD, len(slots)))]
    for i, slot in enumerate(slots):
        if i + SCORES_AHEAD < len(slots):
            ahead.append(scores(*slots[i + SCORES_AHEAD]))
        update(ahead.pop(0), *slot)

    lane = lax.broadcasted_iota(jnp.int32, (tq, LANES), 1)
    hpl = LANES // hd
    for w, o_ref in enumerate((oe_ref, oo_ref)):
        for g in range(hpb // hpl):
            out = None
            for i in range(hpl):
                acc = acc_ref[w, g * hpl + i]
                oj = acc / jnp.sum(jnp.where(lane == hd, acc, 0.0), axis=1, keepdims=True)
                out = oj if i == 0 else jnp.where(lane < i * hd, out, pltpu.roll(oj, i * hd, axis=1))
            o_ref[0, 0, :, g * LANES:(g + 1) * LANES] = out


def _attn_prompt(qa, ka, va, hd, tq):
    bsz, nh, n, _ = qa.shape
    hpb = min(ATTN_HEADS_PER_STEP, nh)
    nq = n // tq
    assert nq % 2 == 0 and (hpb * hd) % LANES == 0 and nh % hpb == 0
    half = nq // 2
    seq_spec = pl.BlockSpec((1, hpb, n, LANES), lambda b, h, e: (b, h, 0, 0))
    out_spec = pl.BlockSpec((1, 1, tq, hpb * hd), lambda b, h, e: (b, e, 0, h))
    out_shape = jax.ShapeDtypeStruct((bsz, half, tq, nh * hd), F32)
    return pl.pallas_call(
        functools.partial(_attn_prompt_kernel, hd=hd, nfull=half - 1),
        grid=(bsz, nh // hpb, half),
        in_specs=[pl.BlockSpec((1, hpb, tq, LANES), lambda b, h, e: (b, h, 2 * e, 0)),
                  pl.BlockSpec((1, hpb, tq, LANES), lambda b, h, e: (b, h, nq - 1 - 2 * e, 0)),
                  seq_spec, seq_spec],
        out_specs=[out_spec, out_spec],
        out_shape=[out_shape, out_shape],
        scratch_shapes=[pltpu.VMEM((2, hpb, tq, LANES), BF16), pltpu.VMEM((2, hpb, tq, LANES), F32),
                        pltpu.VMEM((2, hpb, tq, LANES), F32)],
        compiler_params=_cparams("arbitrary", "arbitrary", "arbitrary"),
        name="attn_prompt",
    )(qa, qa, ka, va)


def _oproj_kernel(o_ref, x_ref, w_ref, g_ref, y_ref):
    y = _dot(o_ref[...].astype(BF16), w_ref[...])
    y_ref[...] = x_ref[...] + _rms(y, g_ref[...])


def _oproj_pair_kernel(oe_ref, oo_ref, x_ref, w_ref, g_ref, y_ref):
    even = pl.program_id(0) % 2 == 0
    o = jnp.where(even, oe_ref[0, 0], oo_ref[0, 0])
    y = _dot(o.astype(BF16), w_ref[...])
    y_ref[...] = x_ref[...] + _rms(y, g_ref[...])


def _oproj_pair(o_even, o_odd, x, w, g):
    bsz, half, tq, d = o_even.shape
    nq = 2 * half
    row = pl.BlockSpec((tq, d), lambda t: (t, 0))
    return pl.pallas_call(
        _oproj_pair_kernel,
        grid=(bsz * nq,),
        in_specs=[pl.BlockSpec((1, 1, tq, d), lambda t: (t // nq, (t % nq) // 2, 0, 0)),
                  pl.BlockSpec((1, 1, tq, d), lambda t: (t // nq, (nq - 1 - t % nq) // 2, 0, 0)),
                  row, _const_spec((d, d)), _const_spec((1, d))],
        out_specs=row,
        out_shape=jax.ShapeDtypeStruct(x.shape, F32),
        compiler_params=_cparams("arbitrary"),
        name="oproj_pair",
    )(o_even, o_odd, x, w, g)


def _oproj(o, x, w, g, tm):
    m, d = x.shape
    row = pl.BlockSpec((tm, d), lambda t: (t, 0))
    return pl.pallas_call(
        _oproj_kernel,
        grid=(m // tm,),
        in_specs=[row, row, _const_spec((d, d)), _const_spec((1, d))],
        out_specs=row,
        out_shape=jax.ShapeDtypeStruct((m, d), F32),
        compiler_params=_cparams("arbitrary"),
        name="oproj",
    )(o, x, w, g)


def _sconv_sample_kernel(x_ref, st_ref, gpre_ref, gpost_ref, win_ref, cw_ref, wout_ref,
                         y_ref, ns_ref, winb_ref, woutb_ref):
    d = x_ref.shape[-1]
    winb_ref[...] = win_ref[...].astype(BF16)
    woutb_ref[...] = wout_ref[...].astype(BF16)
    x = x_ref[...]
    h = _rms(x, gpre_ref[...]).astype(BF16)
    proj = _dot(h, winb_ref[...])
    b = proj[:, :d]
    ch = proj[:, d:2 * d] * proj[:, 2 * d:]
    s0 = st_ref[0]
    s1 = st_ref[1]
    w = cw_ref[...]
    z = s0 * w[0:1, :] + s1 * w[1:2, :] + ch * w[2:3, :]
    y = _dot((b * z).astype(BF16), woutb_ref[...])
    y_ref[...] = x + _rms(y, gpost_ref[...])
    ns_ref[0] = s1
    ns_ref[1] = ch


def _sconv_sample(x, st, gpre, gpost, w_in, conv_w, w_out):
    m, d = x.shape
    return pl.pallas_call(
        _sconv_sample_kernel,
        out_shape=[jax.ShapeDtypeStruct((m, d), F32), jax.ShapeDtypeStruct((CONV_W - 1, m, d), F32),
                   jax.ShapeDtypeStruct(w_in.shape, BF16), jax.ShapeDtypeStruct(w_out.shape, BF16)],
        compiler_params=pltpu.CompilerParams(vmem_limit_bytes=VMEM_LIMIT_BYTES),
        name="sconv_sample",
    )(x, st, gpre, gpost, w_in, conv_w, w_out)


def _ffn_sample_kernel(x_ref, st_ref, gpre_ref, gpost_ref, wg_ref, wu_ref, cw_ref, wd_ref,
                       y_ref, ns_ref, wgb_ref, wub_ref, wdb_ref, acc_ref):
    c = pl.program_id(0)

    @pl.when(c == 0)
    def _():
        acc_ref[...] = jnp.zeros_like(acc_ref)

    wgb_ref[...] = wg_ref[...].astype(BF16)
    wub_ref[...] = wu_ref[...].astype(BF16)
    wdb_ref[...] = wd_ref[...].astype(BF16)
    x = x_ref[...]
    h = _rms(x, gpre_ref[...]).astype(BF16)
    g = _dot(h, wgb_ref[...])
    u = _dot(h, wub_ref[...])
    s0 = st_ref[0]
    s1 = st_ref[1]
    w = cw_ref[...]
    gc = s0 * w[0:1, :] + s1 * w[1:2, :] + g * w[2:3, :]
    act = (gc * jax.nn.sigmoid(gc) * u).astype(BF16)
    acc_ref[...] += _dot(act, wdb_ref[...])
    ns_ref[0] = s1
    ns_ref[1] = g

    @pl.when(c == pl.num_programs(0) - 1)
    def _():
        y_ref[...] = x + _rms(acc_ref[...], gpost_ref[...])


def _ffn_sample(x, layer, st, gpre, gpost, wg, wu, conv_w, wd, fc):
    m, d = x.shape
    f = wg.shape[-1]
    full = pl.BlockSpec((m, d), lambda c: (0, 0))
    vec = pl.BlockSpec((None, 1, d), lambda c: (layer, 0, 0))
    return pl.pallas_call(
        _ffn_sample_kernel,
        grid=(f // fc,),
        in_specs=[full, pl.BlockSpec((None, CONV_W - 1, m, fc), lambda c: (layer, 0, 0, c)), vec, vec,
                  pl.BlockSpec((None, d, fc), lambda c: (layer, 0, c)),
                  pl.BlockSpec((None, d, fc), lambda c: (layer, 0, c)),
                  pl.BlockSpec((None, CONV_W, fc), lambda c: (layer, 0, c)),
                  pl.BlockSpec((None, fc, d), lambda c: (layer, c, 0))],
        out_specs=[full, pl.BlockSpec((CONV_W - 1, m, fc), lambda c: (0, 0, c)),
                   pl.BlockSpec((d, fc), lambda c: (0, c)), pl.BlockSpec((d, fc), lambda c: (0, c)),
                   pl.BlockSpec((fc, d), lambda c: (c, 0))],
        out_shape=[jax.ShapeDtypeStruct((m, d), F32), jax.ShapeDtypeStruct((CONV_W - 1, m, f), F32),
                   jax.ShapeDtypeStruct((d, f), BF16), jax.ShapeDtypeStruct((d, f), BF16),
                   jax.ShapeDtypeStruct((f, d), BF16)],
        scratch_shapes=[pltpu.VMEM((m, d), F32)],
        compiler_params=_cparams("arbitrary"),
        name="ffn_sample",
    )(x, st, gpre, gpost, wg, wu, conv_w, wd)


def _s5_sample_kernel(x_ref, h0r_ref, h0i_ref, gpre_ref, gpost_ref, win_ref, ptab_ref, wbu_ref,
                      wcr_ref, wci_ref, dsk_ref, wa_ref, wb_ref, y_ref, sr_ref, si_ref):
    nsg = wbu_ref.shape[0]
    cw = wbu_ref.shape[1]
    sw = wbu_ref.shape[2] // 2
    x = x_ref[...]
    h = _rms(x, gpre_ref[...]).astype(BF16)
    u = _dot(h, win_ref[...])
    ub = u.astype(BF16)
    ar = ptab_ref[0, 0:1, :]
    ai = ptab_ref[1, 0:1, :]
    ys = []
    for s in range(nsg):
        cs = slice(s * sw, (s + 1) * sw)
        bu = _dot(ub[:, s * cw:(s + 1) * cw], wbu_ref[s])
        hr, hi = _cmul_add(bu[:, :sw], bu[:, sw:], ar[:, cs], ai[:, cs], h0r_ref[:, cs], h0i_ref[:, cs])
        sr_ref[:, cs] = hr
        si_ref[:, cs] = hi
        ys.append(_dot(hr.astype(BF16), wcr_ref[s]) - _dot(hi.astype(BF16), wci_ref[s]))
    yy = jnp.concatenate(ys, axis=1) + dsk_ref[...] * u
    z = jax.nn.gelu(yy).astype(BF16)
    out = _dot(z, wa_ref[...]) * jax.nn.sigmoid(_dot(z, wb_ref[...]))
    y_ref[...] = x + _rms(out, gpost_ref[...])


def _s5_sample(x, h0r, h0i, gpre, gpost, w_in, ptab, w_bu, w_cr, w_ci, d_skip, wa, wb):
    m, d = x.shape
    ns = h0r.shape[1]
    return pl.pallas_call(
        _s5_sample_kernel,
        out_shape=[jax.ShapeDtypeStruct((m, d), F32),
                   jax.ShapeDtypeStruct((m, ns), F32), jax.ShapeDtypeStruct((m, ns), F32)],
        compiler_params=pltpu.CompilerParams(vmem_limit_bytes=VMEM_LIMIT_BYTES),
        name="s5_sample",
    )(x, h0r, h0i, gpre, gpost, w_in, ptab, w_bu, w_cr, w_ci, d_skip, wa, wb)


def _qkv_sample_kernel(x_ref, gpre_ref, w_ref, bf_ref, q_ref, k_ref, v_ref, lf_ref):
    d = x_ref.shape[-1]
    h = _rms(x_ref[...], gpre_ref[...]).astype(BF16)
    proj = _dot(h, w_ref[...])
    q_ref[...] = proj[:, :d]
    k_ref[...] = proj[:, d:2 * d]
    v_ref[...] = proj[:, 2 * d:3 * d]
    lf_ref[...] = _log_sigmoid(proj[:, 3 * d:] + bf_ref[...])


def _qkv_sample(x, gpre, w_all, bf_pad):
    m, d = x.shape
    row = jax.ShapeDtypeStruct((m, d), F32)
    return pl.pallas_call(
        _qkv_sample_kernel,
        out_shape=[row, row, row, jax.ShapeDtypeStruct((m, LANES), F32)],
        compiler_params=pltpu.CompilerParams(vmem_limit_bytes=VMEM_LIMIT_BYTES),
        name="qkv_sample",
    )(x, gpre, w_all, bf_pad)


def _attn_sample_kernel(pt_ref, q_ref, kn_ref, vn_ref, lfn_ref, *rest, npar):
    del pt_ref
    kc, vc, lfc = rest[:npar], rest[npar:2 * npar], rest[2 * npar:3 * npar]
    o_ref, m_ref, l_ref, acc_ref, suf_ref = rest[3 * npar:]
    st = pl.program_id(1)
    nst = pl.num_programs(1)
    nh, hd, rows = kc[0].shape[1], kc[0].shape[2], kc[0].shape[3]
    d = nh * hd
    nt = (((1,), (1,)), ((), ()))
    rid = lax.broadcasted_iota(jnp.int32, (nh, d), 0)
    lid = lax.broadcasted_iota(jnp.int32, (nh, d), 1)
    own = (lid >= rid * hd) & (lid < (rid + 1) * hd)
    qbd = jnp.where(own, q_ref[0], 0.0).astype(BF16)

    @pl.when(st == 0)
    def _():
        kb = jnp.broadcast_to(kn_ref[0], (LANES, d)).astype(BF16)
        m_ref[...] = lax.dot_general(qbd, kb, nt, preferred_element_type=F32)
        l_ref[...] = jnp.ones_like(l_ref)
        acc_ref[...] = jnp.broadcast_to(vn_ref[0], (nh, d))
        suf_ref[...] = lfn_ref[0]

    lf_all = jnp.concatenate([r[0] for r in lfc], axis=0)
    ri = lax.broadcasted_iota(jnp.int32, (rows, 2 * rows), 0)
    ci = lax.broadcasted_iota(jnp.int32, (rows, 2 * rows), 1)
    later = ((ri > ci) | (ci >= rows)).astype(F32)
    sums = jnp.dot(lf_all, later, preferred_element_type=F32, precision=lax.Precision.HIGHEST)
    carry = suf_ref[...]
    scores = []
    for g in range(npar):
        bias = sums[g * nh:(g + 1) * nh, :rows] + carry
        carry = carry + sums[g * nh:(g + 1) * nh, rows:]
        scores.append(_dot(qbd, kc[g][0].reshape(d, rows).astype(BF16)) + bias)
    suf_ref[...] = carry
    s = jnp.concatenate(scores, axis=1)
    m_prev = m_ref[...]
    m_new = jnp.maximum(m_prev, jnp.max(s, axis=1, keepdims=True))
    pr = jnp.exp(s - jnp.concatenate([m_new] * (npar * rows // LANES), axis=1))
    alpha = jnp.exp(m_prev - m_new)
    l_ref[...] = alpha * l_ref[...] + jnp.sum(pr, axis=1, keepdims=True)
    pb = pr.astype(BF16)
    pv = None
    for g in range(npar):
        part = lax.dot_general(pb[:, g * rows:(g + 1) * rows], vc[g][0].reshape(d, rows).astype(BF16), nt,
                               preferred_element_type=F32)
        pv = part if pv is None else pv + part
    acc_ref[...] = jnp.concatenate([alpha] * (d // LANES), axis=1) * acc_ref[...] + pv
    m_ref[...] = m_new

    @pl.when(st == nst - 1)
    def _():
        out = acc_ref[...] / jnp.concatenate([l_ref[...]] * (d // LANES), axis=1)
        o_ref[0] = jnp.sum(jnp.where(own, out, 0.0), axis=0, keepdims=True)


def _attn_sample(page_table, q, k_new, v_new, lf_new, cache_kt, cache_vt, cache_lft, npar):
    m, _, d = q.shape
    npg = page_table.shape[1]
    _, nh, hd, rows = cache_kt.shape
    assert rows == LANES and npg % npar == 0

    def page_idx(g):
        return lambda b, s, pt: (pt[b, npg - 1 - (s * npar + g)], 0, 0, 0)

    def lf_idx(g):
        return lambda b, s, pt: (pt[b, npg - 1 - (s * npar + g)], 0, 0)

    tok = pl.BlockSpec((1, 1, d), lambda b, s, pt: (b, 0, 0))
    grid_spec = pltpu.PrefetchScalarGridSpec(
        num_scalar_prefetch=1,
        grid=(m, npg // npar),
        in_specs=([tok, tok, tok, pl.BlockSpec((1, nh, LANES), lambda b, s, pt: (b, 0, 0))]
                  + [pl.BlockSpec((1, nh, hd, rows), page_idx(g)) for g in range(npar)]
                  + [pl.BlockSpec((1, nh, hd, rows), page_idx(g)) for g in range(npar)]
                  + [pl.BlockSpec((1, nh, rows), lf_idx(g)) for g in range(npar)]),
        out_specs=tok,
        scratch_shapes=[pltpu.VMEM((nh, LANES), F32), pltpu.VMEM((nh, LANES), F32),
                        pltpu.VMEM((nh, d), F32), pltpu.VMEM((nh, LANES), F32)],
    )
    return pl.pallas_call(
        functools.partial(_attn_sample_kernel, npar=npar),
        grid_spec=grid_spec,
        out_shape=jax.ShapeDtypeStruct((m, 1, d), F32),
        compiler_params=_cparams("arbitrary", "arbitrary"),
        name="attn_sample",
    )(page_table, q, k_new, v_new, lf_new, *([cache_kt] * npar), *([cache_vt] * npar), *([cache_lft] * npar))


TM_MIX = 256
TM_S5 = 256
TM_FFN = 512
FC_FFN = 256
FC_FFN_SAMPLE = 256
TQ_ATTN = 512
ATTN_HEADS_PER_STEP = 2
SCORES_AHEAD = 1
PAGES_PER_STEP = 8


def kernel(x_prompt, x_sample, state_sconv_l0, state_ssm_re_l1, state_ssm_im_l1, cache_k_l2, cache_v_l2, cache_logf_l2, state_sconv_l3, state_ffn_conv, page_table, sc_w_in_l0, sc_conv_w_l0, sc_w_out_l0, s5_w_in_l1, s5_lambda_re_l1, s5_lambda_im_l1, s5_log_dt_l1, s5_b_re_l1, s5_b_im_l1, s5_c_re_l1, s5_c_im_l1, s5_d_l1, s5_glu_wa_l1, s5_glu_wb_l1, fox_w_qkvf_l2, fox_b_f_l2, fox_w_o_l2, sc_w_in_l3, sc_conv_w_l3, sc_w_out_l3, norm_mix_pre, norm_mix_post, norm_ffn_pre, norm_ffn_post, ffn_w_gate, ffn_w_up, ffn_conv_w, ffn_w_down):
    bp, n, d = x_prompt.shape
    m = x_sample.shape[0]
    nh = fox_b_f_l2.shape[0]
    hd = d // nh
    f = ffn_w_gate.shape[-1]
    g, p = s5_lambda_re_l1.shape
    ns = g * p
    bf = lambda w: w.astype(BF16)
    row = lambda v: v.reshape(1, -1)

    tm_mix = min(TM_MIX, n)
    tm_s5 = min(TM_S5, n)
    tm_ffn = min(TM_FFN, n)
    tq = min(TQ_ATTN, n)
    fc = min(FC_FFN, f)
    fc_sample = max(c for c in range(LANES, min(FC_FFN_SAMPLE, f) + 1, LANES) if f % c == 0)

    ptab, coef, w_bu, w_cr, w_ci = _s5_tables(s5_lambda_re_l1, s5_lambda_im_l1, s5_log_dt_l1,
                                              s5_b_re_l1, s5_b_im_l1, s5_c_re_l1, s5_c_im_l1,
                                              tm_s5 // SUBLANES)
    scale = hd ** -0.5
    w_qkvf = jnp.concatenate([fox_w_qkvf_l2[:, :d] * scale, fox_w_qkvf_l2[:, d:],
                              jnp.zeros((d, LANES - nh), F32)], axis=1).astype(BF16)
    bf_pad = jnp.concatenate([fox_b_f_l2, jnp.zeros((LANES - nh,), F32)]).reshape(1, LANES)
    pq, pk = _fox_placement(nh)
    s5_tail = (w_bu, w_cr, w_ci, row(s5_d_l1), bf(s5_glu_wa_l1), bf(s5_glu_wb_l1))
    s5_w_in = bf(s5_w_in_l1)
    w_o = bf(fox_w_o_l2)
    ffn_g = (norm_ffn_pre[:, None, :], norm_ffn_post[:, None, :])

    xs = x_sample.reshape(m, d)
    ffn_s = []
    ffn_wb = []
    st_ffn = jnp.swapaxes(state_ffn_conv, 1, 2)

    def ffn_s_layer(xs, i):
        y, ns_, wgb, wub, wdb = _ffn_sample(xs, i, st_ffn, *ffn_g, ffn_w_gate, ffn_w_up, ffn_conv_w,
                                            ffn_w_down, fc_sample)
        ffn_s.append(jnp.swapaxes(ns_, 0, 1))
        ffn_wb.append((wgb, wub, wdb))
        return y

    xs, sc0_s, w_in0, w_out0 = _sconv_sample(xs, jnp.swapaxes(state_sconv_l0, 0, 1), row(norm_mix_pre[0]),
                                             row(norm_mix_post[0]), sc_w_in_l0, sc_conv_w_l0, sc_w_out_l0)
    xs = ffn_s_layer(xs, 0)
    xs, sr_s, si_s = _s5_sample(xs, state_ssm_re_l1.reshape(m, ns), state_ssm_im_l1.reshape(m, ns),
                                row(norm_mix_pre[1]), row(norm_mix_post[1]), s5_w_in, ptab, *s5_tail)
    xs = ffn_s_layer(xs, 1)
    q_s, k_s, v_s, lf_s = _qkv_sample(xs, row(norm_mix_pre[2]), w_qkvf, bf_pad)
    npg = page_table.shape[1]
    npar = max(c for c in range(1, PAGES_PER_STEP + 1) if npg % c == 0)
    o_s = _attn_sample(page_table, q_s.reshape(m, 1, d), k_s.reshape(m, 1, d), v_s.reshape(m, 1, d),
                       jnp.broadcast_to(lf_s[:, :nh, None], (m, nh, LANES)),
                       jnp.transpose(cache_k_l2, (0, 2, 3, 1)), jnp.transpose(cache_v_l2, (0, 2, 3, 1)),
                       jnp.transpose(cache_logf_l2, (0, 2, 1)), npar)
    xs = _oproj(o_s.reshape(m, d), xs, w_o, row(norm_mix_post[2]), m)
    xs = ffn_s_layer(xs, 2)
    xs, sc3_s, w_in3, w_out3 = _sconv_sample(xs, jnp.swapaxes(state_sconv_l3, 0, 1), row(norm_mix_pre[3]),
                                             row(norm_mix_post[3]), sc_w_in_l3, sc_conv_w_l3, sc_w_out_l3)
    xs = ffn_s_layer(xs, 3)

    def ffn_p_layer(xp, i):
        wgb, wub, wdb = ffn_wb[i]
        y, fb = _ffn_prompt(xp, i, *ffn_g, wgb, wub, ffn_conv_w, wdb, tm_ffn, fc)
        ffn_p.append(fb)
        return y

    xp = x_prompt
    ffn_p = []
    xp, sc0_p = _sconv_prompt(xp, row(norm_mix_pre[0]), row(norm_mix_post[0]), w_in0, sc_conv_w_l0, w_out0, tm_ffn)
    xp = ffn_p_layer(xp, 0)
    zeros_state = jnp.zeros((bp, 1, ns), F32)
    xp, sr_p, si_p = _s5_prompt(xp, row(norm_mix_pre[1]), row(norm_mix_post[1]), s5_w_in, ptab, coef, *s5_tail,
                                zeros_state, zeros_state, tm_s5)
    xp = ffn_p_layer(xp, 1)
    qa, ka, va, k_p, v_p, lf_p = _qkv_prompt(xp, row(norm_mix_pre[2]), w_qkvf, bf_pad, pq, pk, nh, tm_mix)
    o_even, o_odd = _attn_prompt(qa, ka, va, hd, tq)
    xp = _oproj_pair(o_even, o_odd, xp.reshape(bp * n, d), w_o, row(norm_mix_post[2])).reshape(bp, n, d)
    xp = ffn_p_layer(xp, 2)
    xp, sc3_p = _sconv_prompt(xp, row(norm_mix_pre[3]), row(norm_mix_post[3]), w_in3, sc_conv_w_l3, w_out3, tm_ffn)
    xp = ffn_p_layer(xp, 3)

    return (xp, xs.reshape(m, 1, d),
            sc0_p, jnp.swapaxes(sc0_s, 0, 1),
            sr_p.reshape(bp, g, p), sr_s.reshape(m, g, p), si_p.reshape(bp, g, p), si_s.reshape(m, g, p),
            k_p.reshape(bp, n, nh, hd), k_s.reshape(m, 1, nh, hd),
            v_p.reshape(bp, n, nh, hd), v_s.reshape(m, 1, nh, hd),
            lf_p, lf_s[:, :nh].reshape(m, 1, nh),
            sc3_p, jnp.swapaxes(sc3_s, 0, 1),
            jnp.stack(ffn_p), jnp.stack(ffn_s))
```

```python
import functools
import math

import jax
import jax.numpy as jnp
from jax import lax
from jax.experimental import pallas as pl
from jax.experimental.pallas import tpu as pltpu

F32 = jnp.float32
BF16 = jnp.bfloat16

RMS_EPS = 1e-6
CONV_W = 3
S5_GROUP = 16
S5_STATE = 64
FOX_HEADS = 16
PAGE_SIZE = 128

LANES = 128
SUBLANES = 8
MXU_DIM = 256
VMEM_LIMIT_BYTES = 56 * 1024 * 1024

NEG_BIG = -1e30
LOG2E = math.log2(math.e)
SCAN_UNROLL = True


def _cparams(*sem):
    return pltpu.CompilerParams(dimension_semantics=sem, vmem_limit_bytes=VMEM_LIMIT_BYTES)


def _const_spec(shape):
    nd = len(shape)
    return pl.BlockSpec(shape, lambda *_: (0,) * nd, pipeline_mode=pl.Buffered(1))


def _rms(x, g):
    ms = jnp.mean(x * x, axis=-1, keepdims=True)
    return x * lax.rsqrt(ms + RMS_EPS) * g


def _dot(a, b):
    return jnp.dot(a, b, preferred_element_type=F32)


def _shift_rows(cur, prev_tail, k):
    rolled = pltpu.roll(cur, k, axis=0)
    rid = lax.broadcasted_iota(jnp.int32, (SUBLANES, cur.shape[1]), 0)
    head = jnp.where(rid < k, pltpu.roll(prev_tail, k, axis=0), rolled[:SUBLANES, :])
    return jnp.concatenate([head, rolled[SUBLANES:, :]], axis=0)


def _causal_conv3(cur, prev_tail, w):
    x1 = _shift_rows(cur, prev_tail, 1)
    x2 = _shift_rows(cur, prev_tail, 2)
    return x2 * w[0:1, :] + x1 * w[1:2, :] + cur * w[2:3, :]


def _sconv_prompt_kernel(x_ref, gpre_ref, gpost_ref, win_ref, cw_ref, wout_ref,
                         y_ref, st_ref, tail_ref):
    t = pl.program_id(1)
    nt = pl.num_programs(1)
    d = x_ref.shape[-1]
    tm = x_ref.shape[1]

    @pl.when(t == 0)
    def _():
        tail_ref[...] = jnp.zeros_like(tail_ref)

    x = x_ref[0]
    h = _rms(x, gpre_ref[...]).astype(BF16)
    proj = _dot(h, win_ref[...])
    b = proj[:, :d]
    ch = proj[:, d:2 * d] * proj[:, 2 * d:]
    z = _causal_conv3(ch, tail_ref[...], cw_ref[...])
    y = _dot((b * z).astype(BF16), wout_ref[...])
    y_ref[0] = x + _rms(y, gpost_ref[...])
    tail_ref[...] = ch[tm - SUBLANES:, :]

    @pl.when(t == nt - 1)
    def _():
        st_ref[0] = ch[tm - (CONV_W - 1):, :]


def _sconv_prompt(x, gpre, gpost, w_in, conv_w, w_out, tm):
    bsz, n, d = x.shape
    return pl.pallas_call(
        _sconv_prompt_kernel,
        grid=(bsz, n // tm),
        in_specs=[
            pl.BlockSpec((1, tm, d), lambda b, t: (b, t, 0)),
            _const_spec((1, d)), _const_spec((1, d)),
            _const_spec((d, 3 * d)), _const_spec((CONV_W, d)), _const_spec((d, d)),
        ],
        out_specs=[
            pl.BlockSpec((1, tm, d), lambda b, t: (b, t, 0)),
            pl.BlockSpec((1, CONV_W - 1, d), lambda b, t: (b, 0, 0)),
        ],
        out_shape=[jax.ShapeDtypeStruct((bsz, n, d), F32),
                   jax.ShapeDtypeStruct((bsz, CONV_W - 1, d), F32)],
        scratch_shapes=[pltpu.VMEM((SUBLANES, d), F32)],
        compiler_params=_cparams("arbitrary", "arbitrary"),
        name="sconv_prompt",
    )(x, gpre, gpost, w_in, conv_w, w_out)


def _ffn_prompt_kernel(x_ref, gpre_ref, gpost_ref, wg_ref, wu_ref, cw_ref, wd_ref,
                       y_ref, st_ref, tail_ref, act_ref, *, fc):
    t = pl.program_id(1)
    nt = pl.num_programs(1)
    tm = x_ref.shape[1]
    f = wg_ref.shape[1]

    @pl.when(t == 0)
    def _():
        tail_ref[...] = jnp.zeros_like(tail_ref)

    x = x_ref[0]
    h = _rms(x, gpre_ref[...]).astype(BF16)
    for c in range(f // fc):
        cs = slice(c * fc, (c + 1) * fc)
        g = _dot(h, wg_ref[:, cs])
        u = _dot(h, wu_ref[:, cs])
        gc = _causal_conv3(g, tail_ref[:, cs], cw_ref[:, cs])
        act_ref[:, cs] = (gc * jax.nn.sigmoid(gc) * u).astype(BF16)
        tail_ref[:, cs] = g[tm - SUBLANES:, :]
    y = _dot(act_ref[...], wd_ref[...])
    y_ref[0] = x + _rms(y, gpost_ref[...])

    @pl.when(t == nt - 1)
    def _():
        st_ref[0] = tail_ref[SUBLANES - (CONV_W - 1):, :]


def _layer_spec(shape, layer):
    nd = len(shape)
    return pl.BlockSpec((None,) + tuple(shape), lambda *_: (layer,) + (0,) * nd, pipeline_mode=pl.Buffered(1))


def _ffn_prompt(x, layer, gpre, gpost, wg, wu, conv_w, wd, tm, fc):
    bsz, n, d = x.shape
    f = wg.shape[-1]
    return pl.pallas_call(
        functools.partial(_ffn_prompt_kernel, fc=fc),
        grid=(bsz, n // tm),
        in_specs=[
            pl.BlockSpec((1, tm, d), lambda b, t: (b, t, 0)),
            _layer_spec((1, d), layer), _layer_spec((1, d), layer),
            _const_spec((d, f)), _const_spec((d, f)),
            _layer_spec((CONV_W, f), layer), _const_spec((f, d)),
        ],
        out_specs=[
            pl.BlockSpec((1, tm, d), lambda b, t: (b, t, 0)),
            pl.BlockSpec((1, CONV_W - 1, f), lambda b, t: (b, 0, 0)),
        ],
        out_shape=[jax.ShapeDtypeStruct((bsz, n, d), F32),
                   jax.ShapeDtypeStruct((bsz, CONV_W - 1, f), F32)],
        scratch_shapes=[pltpu.VMEM((SUBLANES, f), F32), pltpu.VMEM((tm, f), BF16)],
        compiler_params=_cparams("arbitrary", "arbitrary"),
        name="ffn_prompt",
    )(x, gpre, gpost, wg, wu, conv_w, wd)


def _s5_prep_kernel(lr_ref, li_ref, ldt_ref, bre_ref, bim_ref,
                    pwr_ref, pwi_ref, sgr_ref, sgi_ref, bbr_ref, bbi_ref):
    lr = lr_ref[...]
    li = li_ref[...]
    dt = jnp.exp(ldt_ref[...])
    mag = jnp.exp(lr * dt)
    abar_r = mag * jnp.cos(li * dt)
    abar_i = mag * jnp.sin(li * dt)

    def powers(ref_r, ref_i, br, bi):
        cr, ci = br, bi
        for k in range(ref_r.shape[0]):
            ref_r[k] = cr
            ref_i[k] = ci
            cr, ci = cr * br - ci * bi, cr * bi + ci * br

    seg = pwr_ref.shape[0]
    powers(pwr_ref, pwi_ref, abar_r, abar_i)
    powers(sgr_ref, sgi_ref, pwr_ref[seg - 1], pwi_ref[seg - 1])
    den = lr * lr + li * li
    nr = abar_r - 1.0
    kr = (nr * lr + abar_i * li) / den
    ki = (abar_i * lr - nr * li) / den
    bre = bre_ref[...]
    bim = bim_ref[...]
    bbr_ref[...] = kr[:, None, :] * bre - ki[:, None, :] * bim
    bbi_ref[...] = kr[:, None, :] * bim + ki[:, None, :] * bre


def _s5_prep(lam_re, lam_im, log_dt, b_re_t, b_im_t, seg):
    g, p = lam_re.shape
    n = b_re_t.shape[1]
    return pl.pallas_call(
        _s5_prep_kernel,
        out_shape=[jax.ShapeDtypeStruct((seg, g, p), F32), jax.ShapeDtypeStruct((seg, g, p), F32),
                   jax.ShapeDtypeStruct((SUBLANES, g, p), F32), jax.ShapeDtypeStruct((SUBLANES, g, p), F32),
                   jax.ShapeDtypeStruct((g, n, p), F32), jax.ShapeDtypeStruct((g, n, p), F32)],
        name="s5_prep",
    )(lam_re, lam_im, log_dt.reshape(g, 1), b_re_t, b_im_t)


def _s5_block_diag(w, sgroups):
    g, a, b = w.shape
    w4 = w.reshape(g // sgroups, sgroups, a, b)
    eye = jnp.eye(sgroups, dtype=w.dtype)
    return jnp.einsum('sgab,gh->sgahb', w4, eye).reshape(g // sgroups, sgroups * a, sgroups * b)


def _s5_tables(lam_re, lam_im, log_dt, b_re, b_im, c_re, c_im, seg):
    g, p = lam_re.shape
    pwr, pwi, sgr, sgi, bbr, bbi = _s5_prep(lam_re, lam_im, log_dt,
                                            jnp.swapaxes(b_re, 1, 2), jnp.swapaxes(b_im, 1, 2), seg)
    sg = MXU_DIM // S5_GROUP
    w_bu = jnp.concatenate([_s5_block_diag(bbr, sg), _s5_block_diag(bbi, sg)], axis=-1).astype(BF16)
    w_cr = _s5_block_diag(jnp.swapaxes(c_re, 1, 2), sg).astype(BF16)
    w_ci = _s5_block_diag(jnp.swapaxes(c_im, 1, 2), sg).astype(BF16)
    ptab = jnp.stack([pwr.reshape(seg, g * p)[:1], pwi.reshape(seg, g * p)[:1]])
    sgr = sgr.reshape(SUBLANES, g * p)
    sgi = sgi.reshape(SUBLANES, g * p)
    rows = jnp.arange(SUBLANES)[:, None]
    steps = []
    for s in (1, 2, 4):
        steps.append(jnp.where(rows >= s, sgr[s - 1][None, :], 0.0))
        steps.append(jnp.where(rows >= s, sgi[s - 1][None, :], 0.0))
    coef = jnp.stack(steps + [sgr, sgi])
    return ptab, coef, w_bu, w_cr, w_ci


def _cmul_add(xr, xi, ar, ai, sr, si):
    return xr + ar * sr - ai * si, xi + ar * si + ai * sr


def _s5_prompt_kernel(x_ref, gpre_ref, gpost_ref, perm_ref, win_ref, ptab_ref, coef_ref, wbu_ref, wcr_ref,
                      wci_ref, dsk_ref, wa_ref, wb_ref, h0r_ref, h0i_ref,
                      y_ref, sr_ref, si_ref, br_ref, bi_ref, cr_ref, ci_ref):
    t = pl.program_id(1)
    tm = x_ref.shape[1]
    seg = tm // SUBLANES
    nsg = wbu_ref.shape[0]
    cw = wbu_ref.shape[1]
    sw = wbu_ref.shape[2] // 2

    @pl.when(t == 0)
    def _():
        cr_ref[...] = jnp.broadcast_to(h0r_ref[0], cr_ref.shape)
        ci_ref[...] = jnp.broadcast_to(h0i_ref[0], ci_ref.shape)

    x = x_ref[0]
    h = _dot(perm_ref[0], _rms(x, gpre_ref[...]).astype(BF16)).astype(BF16)
    u = _dot(h, win_ref[...])
    ub = u.astype(BF16)
    lc = sw
    for c in range(nsg):
        bu = _dot(ub[:, c * cw:(c + 1) * cw], wbu_ref[c])
        br_ref[:, c * lc:(c + 1) * lc] = bu[:, :sw]
        bi_ref[:, c * lc:(c + 1) * lc] = bu[:, sw:]

    rid = lax.broadcasted_iota(jnp.int32, (SUBLANES, lc), 0)
    ys = []
    for c in range(nsg):
        cs = slice(c * lc, (c + 1) * lc)
        ar = ptab_ref[0, 0:1, cs]
        ai = ptab_ref[1, 0:1, cs]

        def local_step(j, hh, cs=cs, ar=ar, ai=ai):
            r0 = pl.multiple_of(j * SUBLANES, SUBLANES)
            nr, ni = _cmul_add(br_ref[pl.ds(r0, SUBLANES), cs], bi_ref[pl.ds(r0, SUBLANES), cs],
                               ar, ai, hh[0], hh[1])
            br_ref[pl.ds(r0, SUBLANES), cs] = nr
            bi_ref[pl.ds(r0, SUBLANES), cs] = ni
            return nr, ni

        zero = jnp.zeros((SUBLANES, lc), F32)
        er, ei = lax.fori_loop(0, seg, local_step, (zero, zero), unroll=SCAN_UNROLL)
        for j, sh in enumerate((1, 2, 4)):
            er, ei = _cmul_add(er, ei, coef_ref[2 * j, :, cs], coef_ref[2 * j + 1, :, cs],
                               pltpu.roll(er, sh, axis=0), pltpu.roll(ei, sh, axis=0))
        er, ei = _cmul_add(er, ei, coef_ref[6, :, cs], coef_ref[7, :, cs], cr_ref[:, cs], ci_ref[:, cs])
        inr = jnp.where(rid == 0, cr_ref[:, cs], pltpu.roll(er, 1, axis=0))
        ini = jnp.where(rid == 0, ci_ref[:, cs], pltpu.roll(ei, 1, axis=0))
        cr_ref[:, cs] = jnp.broadcast_to(er[SUBLANES - 1:, :], (SUBLANES, lc))
        ci_ref[:, cs] = jnp.broadcast_to(ei[SUBLANES - 1:, :], (SUBLANES, lc))

        def carry_step(j, cc, cs=cs, ar=ar, ai=ai):
            r0 = pl.multiple_of(j * SUBLANES, SUBLANES)
            nr = ar * cc[0] - ai * cc[1]
            ni = ar * cc[1] + ai * cc[0]
            br_ref[pl.ds(r0, SUBLANES), cs] = br_ref[pl.ds(r0, SUBLANES), cs] + nr
            bi_ref[pl.ds(r0, SUBLANES), cs] = bi_ref[pl.ds(r0, SUBLANES), cs] + ni
            return nr, ni

        lax.fori_loop(0, seg, carry_step, (inr, ini), unroll=SCAN_UNROLL)
        ys.append(_dot(br_ref[:, cs].astype(BF16), wcr_ref[c]) - _dot(bi_ref[:, cs].astype(BF16), wci_ref[c]))

    yy = jnp.concatenate(ys, axis=1) + dsk_ref[...] * u
    z = _dot(perm_ref[1], jax.nn.gelu(yy).astype(BF16)).astype(BF16)
    out = _dot(z, wa_ref[...]) * jax.nn.sigmoid(_dot(z, wb_ref[...]))
    y_ref[0] = x + _rms(out, gpost_ref[...])
    sr_ref[0] = cr_ref[0:1, :]
    si_ref[0] = ci_ref[0:1, :]


def _s5_prompt(x, gpre, gpost, w_in, ptab, coef, w_bu, w_cr, w_ci, d_skip, wa, wb, h0r, h0i, tm):
    bsz, n, d = x.shape
    ns = coef.shape[-1]
    seg = tm // SUBLANES
    src = (jnp.arange(tm) % SUBLANES) * seg + jnp.arange(tm) // SUBLANES
    gather = (src[:, None] == jnp.arange(tm)[None, :])
    perm = jnp.stack([gather, gather.T]).astype(BF16)
    return pl.pallas_call(
        _s5_prompt_kernel,
        grid=(bsz, n // tm),
        in_specs=[
            pl.BlockSpec((1, tm, d), lambda b, t: (b, t, 0)),
            _const_spec((1, d)), _const_spec((1, d)), _const_spec(perm.shape), _const_spec((d, d)),
            _const_spec(ptab.shape), _const_spec(coef.shape),
            _const_spec(w_bu.shape), _const_spec(w_cr.shape), _const_spec(w_ci.shape),
            _const_spec((1, d)), _const_spec((d, d)), _const_spec((d, d)),
            pl.BlockSpec((1, 1, ns), lambda b, t: (b, 0, 0)),
            pl.BlockSpec((1, 1, ns), lambda b, t: (b, 0, 0)),
        ],
        out_specs=[
            pl.BlockSpec((1, tm, d), lambda b, t: (b, t, 0)),
            pl.BlockSpec((1, 1, ns), lambda b, t: (b, 0, 0)),
            pl.BlockSpec((1, 1, ns), lambda b, t: (b, 0, 0)),
        ],
        out_shape=[jax.ShapeDtypeStruct((bsz, n, d), F32),
                   jax.ShapeDtypeStruct((bsz, 1, ns), F32), jax.ShapeDtypeStruct((bsz, 1, ns), F32)],
        scratch_shapes=[pltpu.VMEM((tm, ns), F32), pltpu.VMEM((tm, ns), F32),
                        pltpu.VMEM((SUBLANES, ns), F32), pltpu.VMEM((SUBLANES, ns), F32)],
        compiler_params=_cparams("arbitrary", "arbitrary"),
        name="s5_prompt",
    )(x, gpre, gpost, perm, w_in, ptab, coef, w_bu, w_cr, w_ci, d_skip, wa, wb, h0r, h0i)


def _log_sigmoid(x):
    return -(jnp.maximum(-x, 0.0) + jnp.log1p(jnp.exp(-jnp.abs(x))))


def _split3(x):
    hi = x.astype(BF16).astype(F32)
    r = x - hi
    mid = r.astype(BF16).astype(F32)
    lo = r - mid
    return hi, mid, lo


def _fox_placement(nh):
    hd = LANES // 2
    pq = [[0.0] * (nh * LANES) for _ in range(LANES)]
    pk = [[0.0] * (nh * LANES) for _ in range(LANES)]
    one = 3 * nh
    for h in range(nh):
        base = h * LANES + hd
        for j in range(3):
            pq[j * nh + h][base + j] = 1.0
            pq[one][base + 3 + j] = 1.0
            pk[one][base + j] = 1.0
            pk[j * nh + h][base + 3 + j] = -1.0
    return jnp.array(pq, BF16), jnp.array(pk, BF16)


def _qkv_prompt_kernel(x_ref, gpre_ref, w_ref, bf_ref, pq_ref, pk_ref,
                       qa_ref, ka_ref, va_ref, k_ref, v_ref, lf_ref, carry_ref):
    t = pl.program_id(1)
    tm = x_ref.shape[1]
    d = x_ref.shape[-1]
    nh = qa_ref.shape[1]
    hd = d // nh

    @pl.when(t == 0)
    def _():
        carry_ref[...] = jnp.zeros_like(carry_ref)

    x = x_ref[0]
    h = _rms(x, gpre_ref[...]).astype(BF16)
    lane = lax.broadcasted_iota(jnp.int32, (tm, LANES), 1)
    logf = jnp.where(lane < nh, _log_sigmoid(_dot(h, w_ref[:, 3 * d:]) + bf_ref[...]), 0.0)
    proj = _dot(h, w_ref[:, :3 * d])
    ri = lax.broadcasted_iota(jnp.int32, (tm, tm), 0)
    ci = lax.broadcasted_iota(jnp.int32, (tm, tm), 1)
    tri = (ci <= ri).astype(F32)
    cum = jnp.dot(tri, logf, preferred_element_type=F32, precision=lax.Precision.HIGHEST) + carry_ref[0:1, :]
    carry_ref[...] = jnp.broadcast_to(cum[tm - 1:, :], carry_ref.shape)
    hi, mid, lo = _split3(cum * LOG2E)
    src = hi + pltpu.roll(mid, nh, axis=1) + pltpu.roll(lo, 2 * nh, axis=1) + (lane == 3 * nh).astype(F32)
    src = src.astype(BF16)
    aug_q = _dot(src, pq_ref[...])
    aug_k = _dot(src, pk_ref[...])
    low = lane < hd
    one_at_hd = (lane == hd).astype(F32)
    for c in range(d // LANES):
        for o, dst in enumerate((qa_ref, ka_ref, va_ref)):
            blk = proj[:, o * d + c * LANES:o * d + (c + 1) * LANES]
            if o == 0:
                blk = blk * LOG2E
            for half in range(LANES // hd):
                hh = c * (LANES // hd) + half
                v = blk if half == 0 else pltpu.roll(blk, LANES - half * hd, axis=1)
                fill = (aug_q, aug_k)[o][:, hh * LANES:(hh + 1) * LANES] if o < 2 else one_at_hd
                dst[0, hh] = jnp.where(low, v, fill).astype(BF16)
    k_ref[0] = proj[:, d:2 * d].T
    v_ref[0] = proj[:, 2 * d:3 * d].T
    lf_ref[0] = logf[:, :nh]


def _qkv_prompt(x, gpre, w_all, bf_pad, pq, pk, nh, tm):
    bsz, n, d = x.shape
    head_spec = pl.BlockSpec((1, nh, tm, LANES), lambda b, t: (b, 0, t, 0))
    row_spec = pl.BlockSpec((1, tm, d), lambda b, t: (b, t, 0))
    chan_spec = pl.BlockSpec((1, d, tm), lambda b, t: (b, 0, t))
    head_shape = jax.ShapeDtypeStruct((bsz, nh, n, LANES), BF16)
    return pl.pallas_call(
        _qkv_prompt_kernel,
        grid=(bsz, n // tm),
        in_specs=[row_spec, _const_spec((1, d)), _const_spec(w_all.shape), _const_spec((1, LANES)),
                  _const_spec(pq.shape), _const_spec(pk.shape)],
        out_specs=[head_spec, head_spec, head_spec, chan_spec, chan_spec,
                   pl.BlockSpec((1, tm, nh), lambda b, t: (b, t, 0))],
        out_shape=[head_shape, head_shape, head_shape,
                   jax.ShapeDtypeStruct((bsz, d, n), F32), jax.ShapeDtypeStruct((bsz, d, n), F32),
                   jax.ShapeDtypeStruct((bsz, n, nh), F32)],
        scratch_shapes=[pltpu.VMEM((SUBLANES, LANES), F32)],
        compiler_params=_cparams("arbitrary", "arbitrary"),
        name="qkv_prompt",
    )(x, gpre, w_all, bf_pad, pq, pk)


def _attn_prompt_kernel(qe_ref, qo_ref, k_ref, v_ref, oe_ref, oo_ref, q_ref, m_ref, acc_ref, *, hd, nfull):
    e = pl.program_id(2)
    hpb, tq = qe_ref.shape[1], qe_ref.shape[2]
    tk = 2 * tq
    nt = (((1,), (1,)), ((), ()))
    q_ref[0] = qe_ref[0]
    q_ref[1] = qo_ref[0]
    m_ref[...] = jnp.full_like(m_ref, NEG_BIG)
    acc_ref[...] = jnp.zeros_like(acc_ref)

    def scores(w, start, cols, mask_off):
        del mask_off
        return [lax.dot_general(q_ref[w, j], k_ref[0, j, pl.ds(start, cols), :], nt,
                                preferred_element_type=F32) for j in range(hpb)]

    def update(ss, w, start, cols, mask_off):
        for j, s in enumerate(ss):
            if mask_off is not None:
                row = lax.broadcasted_iota(jnp.int32, (tq, cols), 0)
                col = lax.broadcasted_iota(jnp.int32, (tq, cols), 1)
                s = jnp.where(col <= row + mask_off, s, NEG_BIG)
            m_prev = m_ref[w, j]
            m_new = jnp.maximum(m_prev, jnp.max(s, axis=1, keepdims=True))
            pb = jnp.concatenate([jnp.exp2(s[:, c * LANES:(c + 1) * LANES] - m_new).astype(BF16)
                                  for c in range(cols // LANES)], axis=1)
            acc_ref[w, j] = (jnp.exp2(m_prev - m_new) * acc_ref[w, j]
                             + _dot(pb, v_ref[0, j, pl.ds(start, cols), :]))
            m_ref[w, j] = m_new

    slots = []
    for s in range(nfull):
        w = (s >= e).astype(jnp.int32)
        blk = jnp.where(s < e, s, s - e)
        slots.append((w, pl.multiple_of(blk * tk, tk), tk, None))
    slots.append((0, pl.multiple_of(e * tk, tk), tq, 0))
    slots.append((1, pl.multiple_of((nfull - e) * tk, tk), tk, tq))
    ahead = [scores(*slots[i]) for i in range(min(SCORES_AHEAD, len(slots)))]
    for i, slot in enumerate(slots):
        if i + SCORES_AHEAD < len(slots):
            ahead.append(scores(*slots[i + SCORES_AHEAD]))
        update(ahead.pop(0), *slot)

    lane = lax.broadcasted_iota(jnp.int32, (tq, LANES), 1)
    hpl = LANES // hd
    for w, o_ref in enumerate((oe_ref, oo_ref)):
        for g in range(hpb // hpl):
            out = None
            for i in range(hpl):
                acc = acc_ref[w, g * hpl + i]
                oj = acc / jnp.sum(jnp.where(lane == hd, acc, 0.0), axis=1, keepdims=True)
                out = oj if i == 0 else jnp.where(lane < i * hd, out, pltpu.roll(oj, i * hd, axis=1))
            o_ref[0, 0, :, g * LANES:(g + 1) * LANES] = out


def _attn_prompt(qa, ka, va, hd, tq):
    bsz, nh, n, _ = qa.shape
    hpb = min(ATTN_HEADS_PER_STEP, nh)
    nq = n // tq
    assert nq % 2 == 0 and (hpb * hd) % LANES == 0 and nh % hpb == 0
    half = nq // 2
    seq_spec = pl.BlockSpec((1, hpb, n, LANES), lambda b, h, e: (b, h, 0, 0))
    out_spec = pl.BlockSpec((1, 1, tq, hpb * hd), lambda b, h, e: (b, e, 0, h))
    out_shape = jax.ShapeDtypeStruct((bsz, half, tq, nh * hd), F32)
    return pl.pallas_call(
        functools.partial(_attn_prompt_kernel, hd=hd, nfull=half - 1),
        grid=(bsz, nh // hpb, half),
        in_specs=[pl.BlockSpec((1, hpb, tq, LANES), lambda b, h, e: (b, h, 2 * e, 0)),
                  pl.BlockSpec((1, hpb, tq, LANES), lambda b, h, e: (b, h, nq - 1 - 2 * e, 0)),
                  seq_spec, seq_spec],
        out_specs=[out_spec, out_spec],
        out_shape=[out_shape, out_shape],
        scratch_shapes=[pltpu.VMEM((2, hpb, tq, LANES), BF16), pltpu.VMEM((2, hpb, tq, LANES), F32),
                        pltpu.VMEM((2, hpb, tq, LANES), F32)],
        compiler_params=_cparams("arbitrary", "arbitrary", "arbitrary"),
        name="attn_prompt",
    )(qa, qa, ka, va)


def _oproj_kernel(o_ref, x_ref, w_ref, g_ref, y_ref):
    y = _dot(o_ref[...].astype(BF16), w_ref[...])
    y_ref[...] = x_ref[...] + _rms(y, g_ref[...])


def _oproj_pair_kernel(oe_ref, oo_ref, x_ref, w_ref, g_ref, y_ref):
    even = pl.program_id(0) % 2 == 0
    o = jnp.where(even, oe_ref[0, 0], oo_ref[0, 0])
    y = _dot(o.astype(BF16), w_ref[...])
    y_ref[...] = x_ref[...] + _rms(y, g_ref[...])


def _oproj_pair(o_even, o_odd, x, w, g):
    bsz, half, tq, d = o_even.shape
    nq = 2 * half
    row = pl.BlockSpec((tq, d), lambda t: (t, 0))
    return pl.pallas_call(
        _oproj_pair_kernel,
        grid=(bsz * nq,),
        in_specs=[pl.BlockSpec((1, 1, tq, d), lambda t: (t // nq, (t % nq) // 2, 0, 0)),
                  pl.BlockSpec((1, 1, tq, d), lambda t: (t // nq, (nq - 1 - t % nq) // 2, 0, 0)),
                  row, _const_spec((d, d)), _const_spec((1, d))],
        out_specs=row,
        out_shape=jax.ShapeDtypeStruct(x.shape, F32),
        compiler_params=_cparams("arbitrary"),
        name="oproj_pair",
    )(o_even, o_odd, x, w, g)


def _oproj(o, x, w, g, tm):
    m, d = x.shape
    row = pl.BlockSpec((tm, d), lambda t: (t, 0))
    return pl.pallas_call(
        _oproj_kernel,
        grid=(m // tm,),
        in_specs=[row, row, _const_spec((d, d)), _const_spec((1, d))],
        out_specs=row,
        out_shape=jax.ShapeDtypeStruct((m, d), F32),
        compiler_params=_cparams("arbitrary"),
        name="oproj",
    )(o, x, w, g)


def _sconv_sample_kernel(x_ref, st_ref, gpre_ref, gpost_ref, win_ref, cw_ref, wout_ref,
                         y_ref, ns_ref, winb_ref, woutb_ref):
    d = x_ref.shape[-1]
    winb_ref[...] = win_ref[...].astype(BF16)
    woutb_ref[...] = wout_ref[...].astype(BF16)
    x = x_ref[...]
    h = _rms(x, gpre_ref[...]).astype(BF16)
    proj = _dot(h, winb_ref[...])
    b = proj[:, :d]
    ch = proj[:, d:2 * d] * proj[:, 2 * d:]
    s0 = st_ref[0]
    s1 = st_ref[1]
    w = cw_ref[...]
    z = s0 * w[0:1, :] + s1 * w[1:2, :] + ch * w[2:3, :]
    y = _dot((b * z).astype(BF16), woutb_ref[...])
    y_ref[...] = x + _rms(y, gpost_ref[...])
    ns_ref[0] = s1
    ns_ref[1] = ch


def _sconv_sample(x, st, gpre, gpost, w_in, conv_w, w_out):
    m, d = x.shape
    return pl.pallas_call(
        _sconv_sample_kernel,
        out_shape=[jax.ShapeDtypeStruct((m, d), F32), jax.ShapeDtypeStruct((CONV_W - 1, m, d), F32),
                   jax.ShapeDtypeStruct(w_in.shape, BF16), jax.ShapeDtypeStruct(w_out.shape, BF16)],
        compiler_params=pltpu.CompilerParams(vmem_limit_bytes=VMEM_LIMIT_BYTES),
        name="sconv_sample",
    )(x, st, gpre, gpost, w_in, conv_w, w_out)


def _ffn_sample_kernel(x_ref, st_ref, gpre_ref, gpost_ref, wg_ref, wu_ref, cw_ref, wd_ref,
                       y_ref, ns_ref, wgb_ref, wub_ref, wdb_ref, acc_ref):
    c = pl.program_id(0)

    @pl.when(c == 0)
    def _():
        acc_ref[...] = jnp.zeros_like(acc_ref)

    wgb_ref[...] = wg_ref[...].astype(BF16)
    wub_ref[...] = wu_ref[...].astype(BF16)
    wdb_ref[...] = wd_ref[...].astype(BF16)
    x = x_ref[...]
    h = _rms(x, gpre_ref[...]).astype(BF16)
    g = _dot(h, wgb_ref[...])
    u = _dot(h, wub_ref[...])
    s0 = st_ref[0]
    s1 = st_ref[1]
    w = cw_ref[...]
    gc = s0 * w[0:1, :] + s1 * w[1:2, :] + g * w[2:3, :]
    act = (gc * jax.nn.sigmoid(gc) * u).astype(BF16)
    acc_ref[...] += _dot(act, wdb_ref[...])
    ns_ref[0] = s1
    ns_ref[1] = g

    @pl.when(c == pl.num_programs(0) - 1)
    def _():
        y_ref[...] = x + _rms(acc_ref[...], gpost_ref[...])


def _ffn_sample(x, layer, st, gpre, gpost, wg, wu, conv_w, wd, fc):
    m, d = x.shape
    f = wg.shape[-1]
    full = pl.BlockSpec((m, d), lambda c: (0, 0))
    vec = pl.BlockSpec((None, 1, d), lambda c: (layer, 0, 0))
    return pl.pallas_call(
        _ffn_sample_kernel,
        grid=(f // fc,),
        in_specs=[full, pl.BlockSpec((None, CONV_W - 1, m, fc), lambda c: (layer, 0, 0, c)), vec, vec,
                  pl.BlockSpec((None, d, fc), lambda c: (layer, 0, c)),
                  pl.BlockSpec((None, d, fc), lambda c: (layer, 0, c)),
                  pl.BlockSpec((None, CONV_W, fc), lambda c: (layer, 0, c)),
                  pl.BlockSpec((None, fc, d), lambda c: (layer, c, 0))],
        out_specs=[full, pl.BlockSpec((CONV_W - 1, m, fc), lambda c: (0, 0, c)),
                   pl.BlockSpec((d, fc), lambda c: (0, c)), pl.BlockSpec((d, fc), lambda c: (0, c)),
                   pl.BlockSpec((fc, d), lambda c: (c, 0))],
        out_shape=[jax.ShapeDtypeStruct((m, d), F32), jax.ShapeDtypeStruct((CONV_W - 1, m, f), F32),
                   jax.ShapeDtypeStruct((d, f), BF16), jax.ShapeDtypeStruct((d, f), BF16),
                   jax.ShapeDtypeStruct((f, d), BF16)],
        scratch_shapes=[pltpu.VMEM((m, d), F32)],
        compiler_params=_cparams("arbitrary"),
        name="ffn_sample",
    )(x, st, gpre, gpost, wg, wu, conv_w, wd)


def _s5_sample_kernel(x_ref, h0r_ref, h0i_ref, gpre_ref, gpost_ref, win_ref, ptab_ref, wbu_ref,
                      wcr_ref, wci_ref, dsk_ref, wa_ref, wb_ref, y_ref, sr_ref, si_ref):
    nsg = wbu_ref.shape[0]
    cw = wbu_ref.shape[1]
    sw = wbu_ref.shape[2] // 2
    x = x_ref[...]
    h = _rms(x, gpre_ref[...]).astype(BF16)
    u = _dot(h, win_ref[...])
    ub = u.astype(BF16)
    ar = ptab_ref[0, 0:1, :]
    ai = ptab_ref[1, 0:1, :]
    ys = []
    for s in range(nsg):
        cs = slice(s * sw, (s + 1) * sw)
        bu = _dot(ub[:, s * cw:(s + 1) * cw], wbu_ref[s])
        hr, hi = _cmul_add(bu[:, :sw], bu[:, sw:], ar[:, cs], ai[:, cs], h0r_ref[:, cs], h0i_ref[:, cs])
        sr_ref[:, cs] = hr
        si_ref[:, cs] = hi
        ys.append(_dot(hr.astype(BF16), wcr_ref[s]) - _dot(hi.astype(BF16), wci_ref[s]))
    yy = jnp.concatenate(ys, axis=1) + dsk_ref[...] * u
    z = jax.nn.gelu(yy).astype(BF16)
    out = _dot(z, wa_ref[...]) * jax.nn.sigmoid(_dot(z, wb_ref[...]))
    y_ref[...] = x + _rms(out, gpost_ref[...])


def _s5_sample(x, h0r, h0i, gpre, gpost, w_in, ptab, w_bu, w_cr, w_ci, d_skip, wa, wb):
    m, d = x.shape
    ns = h0r.shape[1]
    return pl.pallas_call(
        _s5_sample_kernel,
        out_shape=[jax.ShapeDtypeStruct((m, d), F32),
                   jax.ShapeDtypeStruct((m, ns), F32), jax.ShapeDtypeStruct((m, ns), F32)],
        compiler_params=pltpu.CompilerParams(vmem_limit_bytes=VMEM_LIMIT_BYTES),
        name="s5_sample",
    )(x, h0r, h0i, gpre, gpost, w_in, ptab, w_bu, w_cr, w_ci, d_skip, wa, wb)


def _qkv_sample_kernel(x_ref, gpre_ref, w_ref, bf_ref, q_ref, k_ref, v_ref, lf_ref):
    d = x_ref.shape[-1]
    h = _rms(x_ref[...], gpre_ref[...]).astype(BF16)
    proj = _dot(h, w_ref[...])
    q_ref[...] = proj[:, :d]
    k_ref[...] = proj[:, d:2 * d]
    v_ref[...] = proj[:, 2 * d:3 * d]
    lf_ref[...] = _log_sigmoid(proj[:, 3 * d:] + bf_ref[...])


def _qkv_sample(x, gpre, w_all, bf_pad):
    m, d = x.shape
    row = jax.ShapeDtypeStruct((m, d), F32)
    return pl.pallas_call(
        _qkv_sample_kernel,
        out_shape=[row, row, row, jax.ShapeDtypeStruct((m, LANES), F32)],
        compiler_params=pltpu.CompilerParams(vmem_limit_bytes=VMEM_LIMIT_BYTES),
        name="qkv_sample",
    )(x, gpre, w_all, bf_pad)


def _attn_sample_kernel(pt_ref, q_ref, kn_ref, vn_ref, lfn_ref, *rest, npar):
    del pt_ref
    kc, vc, lfc = rest[:npar], rest[npar:2 * npar], rest[2 * npar:3 * npar]
    o_ref, m_ref, l_ref, acc_ref, suf_ref = rest[3 * npar:]
    st = pl.program_id(1)
    nst = pl.num_programs(1)
    nh, hd, rows = kc[0].shape[1], kc[0].shape[2], kc[0].shape[3]
    d = nh * hd
    nt = (((1,), (1,)), ((), ()))
    rid = lax.broadcasted_iota(jnp.int32, (nh, d), 0)
    lid = lax.broadcasted_iota(jnp.int32, (nh, d), 1)
    own = (lid >= rid * hd) & (lid < (rid + 1) * hd)
    qbd = jnp.where(own, q_ref[0], 0.0).astype(BF16)

    @pl.when(st == 0)
    def _():
        kb = jnp.broadcast_to(kn_ref[0], (LANES, d)).astype(BF16)
        m_ref[...] = lax.dot_general(qbd, kb, nt, preferred_element_type=F32)
        l_ref[...] = jnp.ones_like(l_ref)
        acc_ref[...] = jnp.broadcast_to(vn_ref[0], (nh, d))
        suf_ref[...] = lfn_ref[0]

    lf_all = jnp.concatenate([r[0] for r in lfc], axis=0)
    ri = lax.broadcasted_iota(jnp.int32, (rows, 2 * rows), 0)
    ci = lax.broadcasted_iota(jnp.int32, (rows, 2 * rows), 1)
    later = ((ri > ci) | (ci >= rows)).astype(F32)
    sums = jnp.dot(lf_all, later, preferred_element_type=F32, precision=lax.Precision.HIGHEST)
    carry = suf_ref[...]
    scores = []
    for g in range(npar):
        bias = sums[g * nh:(g + 1) * nh, :rows] + carry
        carry = carry + sums[g * nh:(g + 1) * nh, rows:]
        scores.append(_dot(qbd, kc[g][0].reshape(d, rows).astype(BF16)) + bias)
    suf_ref[...] = carry
    s = jnp.concatenate(scores, axis=1)
    m_prev = m_ref[...]
    m_new = jnp.maximum(m_prev, jnp.max(s, axis=1, keepdims=True))
    pr = jnp.exp(s - jnp.concatenate([m_new] * (npar * rows // LANES), axis=1))
    alpha = jnp.exp(m_prev - m_new)
    l_ref[...] = alpha * l_ref[...] + jnp.sum(pr, axis=1, keepdims=True)
    pb = pr.astype(BF16)
    pv = None
    for g in range(npar):
        part = lax.dot_general(pb[:, g * rows:(g + 1) * rows], vc[g][0].reshape(d, rows).astype(BF16), nt,
                               preferred_element_type=F32)
        pv = part if pv is None else pv + part
    acc_ref[...] = jnp.concatenate([alpha] * (d // LANES), axis=1) * acc_ref[...] + pv
    m_ref[...] = m_new

    @pl.when(st == nst - 1)
    def _():
        out = acc_ref[...] / jnp.concatenate([l_ref[...]] * (d // LANES), axis=1)
        o_ref[0] = jnp.sum(jnp.where(own, out, 0.0), axis=0, keepdims=True)


def _attn_sample(page_table, q, k_new, v_new, lf_new, cache_kt, cache_vt, cache_lft, npar):
    m, _, d = q.shape
    npg = page_table.shape[1]
    _, nh, hd, rows = cache_kt.shape
    assert rows == LANES and npg % npar == 0

    def page_idx(g):
        return lambda b, s, pt: (pt[b, npg - 1 - (s * npar + g)], 0, 0, 0)

    def lf_idx(g):
        return lambda b, s, pt: (pt[b, npg - 1 - (s * npar + g)], 0, 0)

    tok = pl.BlockSpec((1, 1, d), lambda b, s, pt: (b, 0, 0))
    grid_spec = pltpu.PrefetchScalarGridSpec(
        num_scalar_prefetch=1,
        grid=(m, npg // npar),
        in_specs=([tok, tok, tok, pl.BlockSpec((1, nh, LANES), lambda b, s, pt: (b, 0, 0))]
                  + [pl.BlockSpec((1, nh, hd, rows), page_idx(g)) for g in range(npar)]
                  + [pl.BlockSpec((1, nh, hd, rows), page_idx(g)) for g in range(npar)]
                  + [pl.BlockSpec((1, nh, rows), lf_idx(g)) for g in range(npar)]),
        out_specs=tok,
        scratch_shapes=[pltpu.VMEM((nh, LANES), F32), pltpu.VMEM((nh, LANES), F32),
                        pltpu.VMEM((nh, d), F32), pltpu.VMEM((nh, LANES), F32)],
    )
    return pl.pallas_call(
        functools.partial(_attn_sample_kernel, npar=npar),
        grid_spec=grid_spec,
        out_shape=jax.ShapeDtypeStruct((m, 1, d), F32),
        compiler_params=_cparams("arbitrary", "arbitrary"),
        name="attn_sample",
    )(page_table, q, k_new, v_new, lf_new, *([cache_kt] * npar), *([cache_vt] * npar), *([cache_lft] * npar))


TM_MIX = 256
TM_S5 = 256
TM_FFN = 512
FC_FFN = 256
FC_FFN_SAMPLE = 256
TQ_ATTN = 512
ATTN_HEADS_PER_STEP = 2
SCORES_AHEAD = 1
PAGES_PER_STEP = 8


def kernel(x_prompt, x_sample, state_sconv_l0, state_ssm_re_l1, state_ssm_im_l1, cache_k_l2, cache_v_l2, cache_logf_l2, state_sconv_l3, state_ffn_conv, page_table, sc_w_in_l0, sc_conv_w_l0, sc_w_out_l0, s5_w_in_l1, s5_lambda_re_l1, s5_lambda_im_l1, s5_log_dt_l1, s5_b_re_l1, s5_b_im_l1, s5_c_re_l1, s5_c_im_l1, s5_d_l1, s5_glu_wa_l1, s5_glu_wb_l1, fox_w_qkvf_l2, fox_b_f_l2, fox_w_o_l2, sc_w_in_l3, sc_conv_w_l3, sc_w_out_l3, norm_mix_pre, norm_mix_post, norm_ffn_pre, norm_ffn_post, ffn_w_gate, ffn_w_up, ffn_conv_w, ffn_w_down):
    bp, n, d = x_prompt.shape
    m = x_sample.shape[0]
    nh = fox_b_f_l2.shape[0]
    hd = d // nh
    f = ffn_w_gate.shape[-1]
    g, p = s5_lambda_re_l1.shape
    ns = g * p
    bf = lambda w: w.astype(BF16)
    row = lambda v: v.reshape(1, -1)

    tm_mix = min(TM_MIX, n)
    tm_s5 = min(TM_S5, n)
    tm_ffn = min(TM_FFN, n)
    tq = min(TQ_ATTN, n)
    fc = min(FC_FFN, f)
    fc_sample = max(c for c in range(LANES, min(FC_FFN_SAMPLE, f) + 1, LANES) if f % c == 0)

    ptab, coef, w_bu, w_cr, w_ci = _s5_tables(s5_lambda_re_l1, s5_lambda_im_l1, s5_log_dt_l1,
                                              s5_b_re_l1, s5_b_im_l1, s5_c_re_l1, s5_c_im_l1,
                                              tm_s5 // SUBLANES)
    scale = hd ** -0.5
    w_qkvf = jnp.concatenate([fox_w_qkvf_l2[:, :d] * scale, fox_w_qkvf_l2[:, d:],
                              jnp.zeros((d, LANES - nh), F32)], axis=1).astype(BF16)
    bf_pad = jnp.concatenate([fox_b_f_l2, jnp.zeros((LANES - nh,), F32)]).reshape(1, LANES)
    pq, pk = _fox_placement(nh)
    s5_tail = (w_bu, w_cr, w_ci, row(s5_d_l1), bf(s5_glu_wa_l1), bf(s5_glu_wb_l1))
    s5_w_in = bf(s5_w_in_l1)
    w_o = bf(fox_w_o_l2)
    ffn_g = (norm_ffn_pre[:, None, :], norm_ffn_post[:, None, :])

    xs = x_sample.reshape(m, d)
    ffn_s = []
    ffn_wb = []
    st_ffn = jnp.swapaxes(state_ffn_conv, 1, 2)

    def ffn_s_layer(xs, i):
        y, ns_, wgb, wub, wdb = _ffn_sample(xs, i, st_ffn, *ffn_g, ffn_w_gate, ffn_w_up, ffn_conv_w,
                                            ffn_w_down, fc_sample)
        ffn_s.append(jnp.swapaxes(ns_, 0, 1))
        ffn_wb.append((wgb, wub, wdb))
        return y

    xs, sc0_s, w_in0, w_out0 = _sconv_sample(xs, jnp.swapaxes(state_sconv_l0, 0, 1), row(norm_mix_pre[0]),
                                             row(norm_mix_post[0]), sc_w_in_l0, sc_conv_w_l0, sc_w_out_l0)
    xs = ffn_s_layer(xs, 0)
    xs, sr_s, si_s = _s5_sample(xs, state_ssm_re_l1.reshape(m, ns), state_ssm_im_l1.reshape(m, ns),
                                row(norm_mix_pre[1]), row(norm_mix_post[1]), s5_w_in, ptab, *s5_tail)
    xs = ffn_s_layer(xs, 1)
    q_s, k_s, v_s, lf_s = _qkv_sample(xs, row(norm_mix_pre[2]), w_qkvf, bf_pad)
    npg = page_table.shape[1]
    npar = max(c for c in range(1, PAGES_PER_STEP + 1) if npg % c == 0)
    o_s = _attn_sample(page_table, q_s.reshape(m, 1, d), k_s.reshape(m, 1, d), v_s.reshape(m, 1, d),
                       jnp.broadcast_to(lf_s[:, :nh, None], (m, nh, LANES)),
                       jnp.transpose(cache_k_l2, (0, 2, 3, 1)), jnp.transpose(cache_v_l2, (0, 2, 3, 1)),
                       jnp.transpose(cache_logf_l2, (0, 2, 1)), npar)
    xs = _oproj(o_s.reshape(m, d), xs, w_o, row(norm_mix_post[2]), m)
    xs = ffn_s_layer(xs, 2)
    xs, sc3_s, w_in3, w_out3 = _sconv_sample(xs, jnp.swapaxes(state_sconv_l3, 0, 1), row(norm_mix_pre[3]),
                                             row(norm_mix_post[3]), sc_w_in_l3, sc_conv_w_l3, sc_w_out_l3)
    xs = ffn_s_layer(xs, 3)

    def ffn_p_layer(xp, i):
        wgb, wub, wdb = ffn_wb[i]
        y, fb = _ffn_prompt(xp, i, *ffn_g, wgb, wub, ffn_conv_w, wdb, tm_ffn, fc)
        ffn_p.append(fb)
        return y

    xp = x_prompt
    ffn_p = []
    xp, sc0_p = _sconv_prompt(xp, row(norm_mix_pre[0]), row(norm_mix_post[0]), w_in0, sc_conv_w_l0, w_out0, tm_ffn)
    xp = ffn_p_layer(xp, 0)
    zeros_state = jnp.zeros((bp, 1, ns), F32)
    xp, sr_p, si_p = _s5_prompt(xp, row(norm_mix_pre[1]), row(norm_mix_post[1]), s5_w_in, ptab, coef, *s5_tail,
                                zeros_state, zeros_state, tm_s5)
    xp = ffn_p_layer(xp, 1)
    qa, ka, va, k_p, v_p, lf_p = _qkv_prompt(xp, row(norm_mix_pre[2]), w_qkvf, bf_pad, pq, pk, nh, tm_mix)
    o_even, o_odd = _attn_prompt(qa, ka, va, hd, tq)
    xp = _oproj_pair(o_even, o_odd, xp.reshape(bp * n, d), w_o, row(norm_mix_post[2])).reshape(bp, n, d)
    xp = ffn_p_layer(xp, 2)
    xp, sc3_p = _sconv_prompt(xp, row(norm_mix_pre[3]), row(norm_mix_post[3]), w_in3, sc_conv_w_l3, w_out3, tm_ffn)
    xp = ffn_p_layer(xp, 3)

    return (xp, xs.reshape(m, 1, d),
            sc0_p, jnp.swapaxes(sc0_s, 0, 1),
            sr_p.reshape(bp, g, p), sr_s.reshape(m, g, p), si_p.reshape(bp, g, p), si_s.reshape(m, g, p),
            jnp.transpose(k_p.reshape(bp, nh, hd, n), (0, 3, 1, 2)), k_s.reshape(m, 1, nh, hd),
            jnp.transpose(v_p.reshape(bp, nh, hd, n), (0, 3, 1, 2)), v_s.reshape(m, 1, nh, hd),
            lf_p, lf_s[:, :nh].reshape(m, 1, nh),
            sc3_p, jnp.swapaxes(sc3_s, 0, 1),
            jnp.stack(ffn_p), jnp.stack(ffn_s))
```

```python
import functools
import math

import jax
import jax.numpy as jnp
from jax import lax
from jax.experimental import pallas as pl
from jax.experimental.pallas import tpu as pltpu

F32 = jnp.float32
BF16 = jnp.bfloat16

RMS_EPS = 1e-6
CONV_W = 3
S5_GROUP = 16
S5_STATE = 64
FOX_HEADS = 16
PAGE_SIZE = 128

LANES = 128
SUBLANES = 8
MXU_DIM = 256
VMEM_LIMIT_BYTES = 56 * 1024 * 1024

NEG_BIG = -1e30
LOG2E = math.log2(math.e)
SCAN_UNROLL = True


def _cparams(*sem):
    return pltpu.CompilerParams(dimension_semantics=sem, vmem_limit_bytes=VMEM_LIMIT_BYTES)


def _const_spec(shape):
    nd = len(shape)
    return pl.BlockSpec(shape, lambda *_: (0,) * nd, pipeline_mode=pl.Buffered(1))


def _rms(x, g):
    ms = jnp.mean(x * x, axis=-1, keepdims=True)
    return x * lax.rsqrt(ms + RMS_EPS) * g


def _dot(a, b):
    return jnp.dot(a, b, preferred_element_type=F32)


def _shift_rows(cur, prev_tail, k):
    rolled = pltpu.roll(cur, k, axis=0)
    rid = lax.broadcasted_iota(jnp.int32, (SUBLANES, cur.shape[1]), 0)
    head = jnp.where(rid < k, pltpu.roll(prev_tail, k, axis=0), rolled[:SUBLANES, :])
    return jnp.concatenate([head, rolled[SUBLANES:, :]], axis=0)


def _causal_conv3(cur, prev_tail, w):
    x1 = _shift_rows(cur, prev_tail, 1)
    x2 = _shift_rows(cur, prev_tail, 2)
    return x2 * w[0:1, :] + x1 * w[1:2, :] + cur * w[2:3, :]


def _sconv_prompt_kernel(x_ref, gpre_ref, gpost_ref, win_ref, cw_ref, wout_ref,
                         y_ref, st_ref, tail_ref):
    t = pl.program_id(1)
    nt = pl.num_programs(1)
    d = x_ref.shape[-1]
    tm = x_ref.shape[1]

    @pl.when(t == 0)
    def _():
        tail_ref[...] = jnp.zeros_like(tail_ref)

    x = x_ref[0]
    h = _rms(x, gpre_ref[...]).astype(BF16)
    proj = _dot(h, win_ref[...])
    b = proj[:, :d]
    ch = proj[:, d:2 * d] * proj[:, 2 * d:]
    z = _causal_conv3(ch, tail_ref[...], cw_ref[...])
    y = _dot((b * z).astype(BF16), wout_ref[...])
    y_ref[0] = x + _rms(y, gpost_ref[...])
    tail_ref[...] = ch[tm - SUBLANES:, :]

    @pl.when(t == nt - 1)
    def _():
        st_ref[0] = ch[tm - (CONV_W - 1):, :]


def _sconv_prompt(x, gpre, gpost, w_in, conv_w, w_out, tm):
    bsz, n, d = x.shape
    return pl.pallas_call(
        _sconv_prompt_kernel,
        grid=(bsz, n // tm),
        in_specs=[
            pl.BlockSpec((1, tm, d), lambda b, t: (b, t, 0)),
            _const_spec((1, d)), _const_spec((1, d)),
            _const_spec((d, 3 * d)), _const_spec((CONV_W, d)), _const_spec((d, d)),
        ],
        out_specs=[
            pl.BlockSpec((1, tm, d), lambda b, t: (b, t, 0)),
            pl.BlockSpec((1, CONV_W - 1, d), lambda b, t: (b, 0, 0)),
        ],
        out_shape=[jax.ShapeDtypeStruct((bsz, n, d), F32),
                   jax.ShapeDtypeStruct((bsz, CONV_W - 1, d), F32)],
        scratch_shapes=[pltpu.VMEM((SUBLANES, d), F32)],
        compiler_params=_cparams("arbitrary", "arbitrary"),
        name="sconv_prompt",
    )(x, gpre, gpost, w_in, conv_w, w_out)


def _ffn_prompt_kernel(x_ref, gpre_ref, gpost_ref, wg_ref, wu_ref, cw_ref, wd_ref,
                       y_ref, st_ref, tail_ref, act_ref, *, fc):
    t = pl.program_id(1)
    nt = pl.num_programs(1)
    tm = x_ref.shape[1]
    f = wg_ref.shape[1]

    @pl.when(t == 0)
    def _():
        tail_ref[...] = jnp.zeros_like(tail_ref)

    x = x_ref[0]
    h = _rms(x, gpre_ref[...]).astype(BF16)
    for c in range(f // fc):
        cs = slice(c * fc, (c + 1) * fc)
        g = _dot(h, wg_ref[:, cs])
        u = _dot(h, wu_ref[:, cs])
        gc = _causal_conv3(g, tail_ref[:, cs], cw_ref[:, cs])
        act_ref[:, cs] = (gc * jax.nn.sigmoid(gc) * u).astype(BF16)
        tail_ref[:, cs] = g[tm - SUBLANES:, :]
    y = _dot(act_ref[...], wd_ref[...])
    y_ref[0] = x + _rms(y, gpost_ref[...])

    @pl.when(t == nt - 1)
    def _():
        st_ref[0] = tail_ref[SUBLANES - (CONV_W - 1):, :]


def _layer_spec(shape, layer):
    nd = len(shape)
    return pl.BlockSpec((None,) + tuple(shape), lambda *_: (layer,) + (0,) * nd, pipeline_mode=pl.Buffered(1))


def _ffn_prompt(x, layer, gpre, gpost, wg, wu, conv_w, wd, tm, fc):
    bsz, n, d = x.shape
    f = wg.shape[-1]
    return pl.pallas_call(
        functools.partial(_ffn_prompt_kernel, fc=fc),
        grid=(bsz, n // tm),
        in_specs=[
            pl.BlockSpec((1, tm, d), lambda b, t: (b, t, 0)),
            _layer_spec((1, d), layer), _layer_spec((1, d), layer),
            _const_spec((d, f)), _const_spec((d, f)),
            _layer_spec((CONV_W, f), layer), _const_spec((f, d)),
        ],
        out_specs=[
            pl.BlockSpec((1, tm, d), lambda b, t: (b, t, 0)),
            pl.BlockSpec((1, CONV_W - 1, f), lambda b, t: (b, 0, 0)),
        ],
        out_shape=[jax.ShapeDtypeStruct((bsz, n, d), F32),
                   jax.ShapeDtypeStruct((bsz, CONV_W - 1, f), F32)],
        scratch_shapes=[pltpu.VMEM((SUBLANES, f), F32), pltpu.VMEM((tm, f), BF16)],
        compiler_params=_cparams("arbitrary", "arbitrary"),
        name="ffn_prompt",
    )(x, gpre, gpost, wg, wu, conv_w, wd)


def _s5_prep_kernel(lr_ref, li_ref, ldt_ref, bre_ref, bim_ref,
                    pwr_ref, pwi_ref, sgr_ref, sgi_ref, bbr_ref, bbi_ref):
    lr = lr_ref[...]
    li = li_ref[...]
    dt = jnp.exp(ldt_ref[...])
    mag = jnp.exp(lr * dt)
    abar_r = mag * jnp.cos(li * dt)
    abar_i = mag * jnp.sin(li * dt)

    def powers(ref_r, ref_i, br, bi):
        cr, ci = br, bi
        for k in range(ref_r.shape[0]):
            ref_r[k] = cr
            ref_i[k] = ci
            cr, ci = cr * br - ci * bi, cr * bi + ci * br

    seg = pwr_ref.shape[0]
    powers(pwr_ref, pwi_ref, abar_r, abar_i)
    powers(sgr_ref, sgi_ref, pwr_ref[seg - 1], pwi_ref[seg - 1])
    den = lr * lr + li * li
    nr = abar_r - 1.0
    kr = (nr * lr + abar_i * li) / den
    ki = (abar_i * lr - nr * li) / den
    bre = bre_ref[...]
    bim = bim_ref[...]
    bbr_ref[...] = kr[:, None, :] * bre - ki[:, None, :] * bim
    bbi_ref[...] = kr[:, None, :] * bim + ki[:, None, :] * bre


def _s5_prep(lam_re, lam_im, log_dt, b_re_t, b_im_t, seg):
    g, p = lam_re.shape
    n = b_re_t.shape[1]
    return pl.pallas_call(
        _s5_prep_kernel,
        out_shape=[jax.ShapeDtypeStruct((seg, g, p), F32), jax.ShapeDtypeStruct((seg, g, p), F32),
                   jax.ShapeDtypeStruct((SUBLANES, g, p), F32), jax.ShapeDtypeStruct((SUBLANES, g, p), F32),
                   jax.ShapeDtypeStruct((g, n, p), F32), jax.ShapeDtypeStruct((g, n, p), F32)],
        name="s5_prep",
    )(lam_re, lam_im, log_dt.reshape(g, 1), b_re_t, b_im_t)


def _s5_block_diag(w, sgroups):
    g, a, b = w.shape
    w4 = w.reshape(g // sgroups, sgroups, a, b)
    eye = jnp.eye(sgroups, dtype=w.dtype)
    return jnp.einsum('sgab,gh->sgahb', w4, eye).reshape(g // sgroups, sgroups * a, sgroups * b)


def _s5_tables(lam_re, lam_im, log_dt, b_re, b_im, c_re, c_im, seg):
    g, p = lam_re.shape
    pwr, pwi, sgr, sgi, bbr, bbi = _s5_prep(lam_re, lam_im, log_dt,
                                            jnp.swapaxes(b_re, 1, 2), jnp.swapaxes(b_im, 1, 2), seg)
    sg = MXU_DIM // S5_GROUP
    w_bu = jnp.concatenate([_s5_block_diag(bbr, sg), _s5_block_diag(bbi, sg)], axis=-1).astype(BF16)
    w_cr = _s5_block_diag(jnp.swapaxes(c_re, 1, 2), sg).astype(BF16)
    w_ci = _s5_block_diag(jnp.swapaxes(c_im, 1, 2), sg).astype(BF16)
    ptab = jnp.stack([pwr.reshape(seg, g * p)[:1], pwi.reshape(seg, g * p)[:1]])
    sgr = sgr.reshape(SUBLANES, g * p)
    sgi = sgi.reshape(SUBLANES, g * p)
    rows = jnp.arange(SUBLANES)[:, None]
    steps = []
    for s in (1, 2, 4):
        steps.append(jnp.where(rows >= s, sgr[s - 1][None, :], 0.0))
        steps.append(jnp.where(rows >= s, sgi[s - 1][None, :], 0.0))
    coef = jnp.stack(steps + [sgr, sgi])
    return ptab, coef, w_bu, w_cr, w_ci


def _cmul_add(xr, xi, ar, ai, sr, si):
    return xr + ar * sr - ai * si, xi + ar * si + ai * sr


def _s5_prompt_kernel(x_ref, gpre_ref, gpost_ref, perm_ref, win_ref, ptab_ref, coef_ref, wbu_ref, wcr_ref,
                      wci_ref, dsk_ref, wa_ref, wb_ref, h0r_ref, h0i_ref,
                      y_ref, sr_ref, si_ref, br_ref, bi_ref, cr_ref, ci_ref):
    t = pl.program_id(1)
    tm = x_ref.shape[1]
    seg = tm // SUBLANES
    nsg = wbu_ref.shape[0]
    cw = wbu_ref.shape[1]
    sw = wbu_ref.shape[2] // 2

    @pl.when(t == 0)
    def _():
        cr_ref[...] = jnp.broadcast_to(h0r_ref[0], cr_ref.shape)
        ci_ref[...] = jnp.broadcast_to(h0i_ref[0], ci_ref.shape)

    x = x_ref[0]
    h = _dot(perm_ref[0], _rms(x, gpre_ref[...]).astype(BF16)).astype(BF16)
    u = _dot(h, win_ref[...])
    ub = u.astype(BF16)
    lc = sw
    for c in range(nsg):
        bu = _dot(ub[:, c * cw:(c + 1) * cw], wbu_ref[c])
        br_ref[:, c * lc:(c + 1) * lc] = bu[:, :sw]
        bi_ref[:, c * lc:(c + 1) * lc] = bu[:, sw:]

    rid = lax.broadcasted_iota(jnp.int32, (SUBLANES, lc), 0)
    ys = []
    for c in range(nsg):
        cs = slice(c * lc, (c + 1) * lc)
        ar = ptab_ref[0, 0:1, cs]
        ai = ptab_ref[1, 0:1, cs]

        def local_step(j, hh, cs=cs, ar=ar, ai=ai):
            r0 = pl.multiple_of(j * SUBLANES, SUBLANES)
            nr, ni = _cmul_add(br_ref[pl.ds(r0, SUBLANES), cs], bi_ref[pl.ds(r0, SUBLANES), cs],
                               ar, ai, hh[0], hh[1])
            br_ref[pl.ds(r0, SUBLANES), cs] = nr
            bi_ref[pl.ds(r0, SUBLANES), cs] = ni
            return nr, ni

        zero = jnp.zeros((SUBLANES, lc), F32)
        er, ei = lax.fori_loop(0, seg, local_step, (zero, zero), unroll=SCAN_UNROLL)
        for j, sh in enumerate((1, 2, 4)):
            er, ei = _cmul_add(er, ei, coef_ref[2 * j, :, cs], coef_ref[2 * j + 1, :, cs],
                               pltpu.roll(er, sh, axis=0), pltpu.roll(ei, sh, axis=0))
        er, ei = _cmul_add(er, ei, coef_ref[6, :, cs], coef_ref[7, :, cs], cr_ref[:, cs], ci_ref[:, cs])
        inr = jnp.where(rid == 0, cr_ref[:, cs], pltpu.roll(er, 1, axis=0))
        ini = jnp.where(rid == 0, ci_ref[:, cs], pltpu.roll(ei, 1, axis=0))
        cr_ref[:, cs] = jnp.broadcast_to(er[SUBLANES - 1:, :], (SUBLANES, lc))
        ci_ref[:, cs] = jnp.broadcast_to(ei[SUBLANES - 1:, :], (SUBLANES, lc))

        def carry_step(j, cc, cs=cs, ar=ar, ai=ai):
            r0 = pl.multiple_of(j * SUBLANES, SUBLANES)
            nr = ar * cc[0] - ai * cc[1]
            ni = ar * cc[1] + ai * cc[0]
            br_ref[pl.ds(r0, SUBLANES), cs] = br_ref[pl.ds(r0, SUBLANES), cs] + nr
            bi_ref[pl.ds(r0, SUBLANES), cs] = bi_ref[pl.ds(r0, SUBLANES), cs] + ni
            return nr, ni

        lax.fori_loop(0, seg, carry_step, (inr, ini), unroll=SCAN_UNROLL)
        ys.append(_dot(br_ref[:, cs].astype(BF16), wcr_ref[c]) - _dot(bi_ref[:, cs].astype(BF16), wci_ref[c]))

    yy = jnp.concatenate(ys, axis=1) + dsk_ref[...] * u
    z = _dot(perm_ref[1], jax.nn.gelu(yy).astype(BF16)).astype(BF16)
    out = _dot(z, wa_ref[...]) * jax.nn.sigmoid(_dot(z, wb_ref[...]))
    y_ref[0] = x + _rms(out, gpost_ref[...])
    sr_ref[0] = cr_ref[0:1, :]
    si_ref[0] = ci_ref[0:1, :]


def _s5_prompt(x, gpre, gpost, w_in, ptab, coef, w_bu, w_cr, w_ci, d_skip, wa, wb, h0r, h0i, tm):
    bsz, n, d = x.shape
    ns = coef.shape[-1]
    seg = tm // SUBLANES
    src = (jnp.arange(tm) % SUBLANES) * seg + jnp.arange(tm) // SUBLANES
    gather = (src[:, None] == jnp.arange(tm)[None, :])
    perm = jnp.stack([gather, gather.T]).astype(BF16)
    return pl.pallas_call(
        _s5_prompt_kernel,
        grid=(bsz, n // tm),
        in_specs=[
            pl.BlockSpec((1, tm, d), lambda b, t: (b, t, 0)),
            _const_spec((1, d)), _const_spec((1, d)), _const_spec(perm.shape), _const_spec((d, d)),
            _const_spec(ptab.shape), _const_spec(coef.shape),
            _const_spec(w_bu.shape), _const_spec(w_cr.shape), _const_spec(w_ci.shape),
            _const_spec((1, d)), _const_spec((d, d)), _const_spec((d, d)),
            pl.BlockSpec((1, 1, ns), lambda b, t: (b, 0, 0)),
            pl.BlockSpec((1, 1, ns), lambda b, t: (b, 0, 0)),
        ],
        out_specs=[
            pl.BlockSpec((1, tm, d), lambda b, t: (b, t, 0)),
            pl.BlockSpec((1, 1, ns), lambda b, t: (b, 0, 0)),
            pl.BlockSpec((1, 1, ns), lambda b, t: (b, 0, 0)),
        ],
        out_shape=[jax.ShapeDtypeStruct((bsz, n, d), F32),
                   jax.ShapeDtypeStruct((bsz, 1, ns), F32), jax.ShapeDtypeStruct((bsz, 1, ns), F32)],
        scratch_shapes=[pltpu.VMEM((tm, ns), F32), pltpu.VMEM((tm, ns), F32),
                        pltpu.VMEM((SUBLANES, ns), F32), pltpu.VMEM((SUBLANES, ns), F32)],
        compiler_params=_cparams("arbitrary", "arbitrary"),
        name="s5_prompt",
    )(x, gpre, gpost, perm, w_in, ptab, coef, w_bu, w_cr, w_ci, d_skip, wa, wb, h0r, h0i)


def _log_sigmoid(x):
    return -(jnp.maximum(-x, 0.0) + jnp.log1p(jnp.exp(-jnp.abs(x))))


def _split3(x):
    hi = x.astype(BF16).astype(F32)
    r = x - hi
    mid = r.astype(BF16).astype(F32)
    lo = r - mid
    return hi, mid, lo


def _fox_placement(nh):
    hd = LANES // 2
    pq = [[0.0] * (nh * LANES) for _ in range(LANES)]
    pk = [[0.0] * (nh * LANES) for _ in range(LANES)]
    one = 3 * nh
    for h in range(nh):
        base = h * LANES + hd
        for j in range(3):
            pq[j * nh + h][base + j] = 1.0
            pq[one][base + 3 + j] = 1.0
            pk[one][base + j] = 1.0
            pk[j * nh + h][base + 3 + j] = -1.0
    return jnp.array(pq, BF16), jnp.array(pk, BF16)


def _qkv_prompt_kernel(x_ref, gpre_ref, w_ref, bf_ref, pq_ref, pk_ref,
                       qa_ref, ka_ref, va_ref, k_ref, v_ref, lf_ref, carry_ref):
    t = pl.program_id(1)
    tm = x_ref.shape[1]
    d = x_ref.shape[-1]
    nh = qa_ref.shape[1]
    hd = d // nh

    @pl.when(t == 0)
    def _():
        carry_ref[...] = jnp.zeros_like(carry_ref)

    x = x_ref[0]
    h = _rms(x, gpre_ref[...]).astype(BF16)
    lane = lax.broadcasted_iota(jnp.int32, (tm, LANES), 1)
    logf = jnp.where(lane < nh, _log_sigmoid(_dot(h, w_ref[:, 3 * d:]) + bf_ref[...]), 0.0)
    proj = _dot(h, w_ref[:, :3 * d])
    ri = lax.broadcasted_iota(jnp.int32, (tm, tm), 0)
    ci = lax.broadcasted_iota(jnp.int32, (tm, tm), 1)
    tri = (ci <= ri).astype(F32)
    cum = jnp.dot(tri, logf, preferred_element_type=F32, precision=lax.Precision.HIGHEST) + carry_ref[0:1, :]
    carry_ref[...] = jnp.broadcast_to(cum[tm - 1:, :], carry_ref.shape)
    hi, mid, lo = _split3(cum * LOG2E)
    src = hi + pltpu.roll(mid, nh, axis=1) + pltpu.roll(lo, 2 * nh, axis=1) + (lane == 3 * nh).astype(F32)
    src = src.astype(BF16)
    aug_q = _dot(src, pq_ref[...])
    aug_k = _dot(src, pk_ref[...])
    low = lane < hd
    one_at_hd = (lane == hd).astype(F32)
    for c in range(d // LANES):
        for o, dst in enumerate((qa_ref, ka_ref, va_ref)):
            blk = proj[:, o * d + c * LANES:o * d + (c + 1) * LANES]
            if o == 0:
                blk = blk * LOG2E
            for half in range(LANES // hd):
                hh = c * (LANES // hd) + half
                v = blk if half == 0 else pltpu.roll(blk, LANES - half * hd, axis=1)
                fill = (aug_q, aug_k)[o][:, hh * LANES:(hh + 1) * LANES] if o < 2 else one_at_hd
                dst[0, hh] = jnp.where(low, v, fill).astype(BF16)
    k_ref[0] = proj[:, d:2 * d].T
    v_ref[0] = proj[:, 2 * d:3 * d].T
    lf_ref[0] = logf.T[:nh, :]


def _qkv_prompt(x, gpre, w_all, bf_pad, pq, pk, nh, tm):
    bsz, n, d = x.shape
    head_spec = pl.BlockSpec((1, nh, tm, LANES), lambda b, t: (b, 0, t, 0))
    row_spec = pl.BlockSpec((1, tm, d), lambda b, t: (b, t, 0))
    chan_spec = pl.BlockSpec((1, d, tm), lambda b, t: (b, 0, t))
    head_shape = jax.ShapeDtypeStruct((bsz, nh, n, LANES), BF16)
    return pl.pallas_call(
        _qkv_prompt_kernel,
        grid=(bsz, n // tm),
        in_specs=[row_spec, _const_spec((1, d)), _const_spec(w_all.shape), _const_spec((1, LANES)),
                  _const_spec(pq.shape), _const_spec(pk.shape)],
        out_specs=[head_spec, head_spec, head_spec, chan_spec, chan_spec,
                   pl.BlockSpec((1, nh, tm), lambda b, t: (b, 0, t))],
        out_shape=[head_shape, head_shape, head_shape,
                   jax.ShapeDtypeStruct((bsz, d, n), F32), jax.ShapeDtypeStruct((bsz, d, n), F32),
                   jax.ShapeDtypeStruct((bsz, nh, n), F32)],
        scratch_shapes=[pltpu.VMEM((SUBLANES, LANES), F32)],
        compiler_params=_cparams("arbitrary", "arbitrary"),
        name="qkv_prompt",
    )(x, gpre, w_all, bf_pad, pq, pk)


def _attn_prompt_kernel(qe_ref, qo_ref, k_ref, v_ref, oe_ref, oo_ref, q_ref, m_ref, acc_ref, *, hd, nfull):
    e = pl.program_id(2)
    hpb, tq = qe_ref.shape[1], qe_ref.shape[2]
    tk = 2 * tq
    nt = (((1,), (1,)), ((), ()))
    q_ref[0] = qe_ref[0]
    q_ref[1] = qo_ref[0]
    m_ref[...] = jnp.full_like(m_ref, NEG_BIG)
    acc_ref[...] = jnp.zeros_like(acc_ref)

    def scores(w, start, cols, mask_off):
        del mask_off
        return [lax.dot_general(q_ref[w, j], k_ref[0, j, pl.ds(start, cols), :], nt,
                                preferred_element_type=F32) for j in range(hpb)]

    def update(ss, w, start, cols, mask_off):
        for j, s in enumerate(ss):
            if mask_off is not None:
                row = lax.broadcasted_iota(jnp.int32, (tq, cols), 0)
                col = lax.broadcasted_iota(jnp.int32, (tq, cols), 1)
                s = jnp.where(col <= row + mask_off, s, NEG_BIG)
            m_prev = m_ref[w, j]
            m_new = jnp.maximum(m_prev, jnp.max(s, axis=1, keepdims=True))
            pb = jnp.concatenate([jnp.exp2(s[:, c * LANES:(c + 1) * LANES] - m_new).astype(BF16)
                                  for c in range(cols // LANES)], axis=1)
            acc_ref[w, j] = (jnp.exp2(m_prev - m_new) * acc_ref[w, j]
                             + _dot(pb, v_ref[0, j, pl.ds(start, cols), :]))
            m_ref[w, j] = m_new

    slots = []
    for s in range(nfull):
        w = (s >= e).astype(jnp.int32)
        blk = jnp.where(s < e, s, s - e)
        slots.append((w, pl.multiple_of(blk * tk, tk), tk, None))
    slots.append((0, pl.multiple_of(e * tk, tk), tq, 0))
    slots.append((1, pl.multiple_of((nfull - e) * tk, tk), tk, tq))
    ahead = [scores(*slots[i]) for i in range(min(SCORES_AHEAD, len(slots)))]
    for i, slot in enumerate(slots):
        if i + SCORES_AHEAD < len(slots):
            ahead.append(scores(*slots[i + SCORES_AHEAD]))
        update(ahead.pop(0), *slot)

    lane = lax.broadcasted_iota(jnp.int32, (tq, LANES), 1)
    hpl = LANES // hd
    for w, o_ref in enumerate((oe_ref, oo_ref)):
        for g in range(hpb // hpl):
            out = None
            for i in range(hpl):
                acc = acc_ref[w, g * hpl + i]
                oj = acc / jnp.sum(jnp.where(lane == hd, acc, 0.0), axis=1, keepdims=True)
                out = oj if i == 0 else jnp.where(lane < i * hd, out, pltpu.roll(oj, i * hd, axis=1))
            o_ref[0, 0, :, g * LANES:(g + 1) * LANES] = out


def _attn_prompt(qa, ka, va, hd, tq):
    bsz, nh, n, _ = qa.shape
    hpb = min(ATTN_HEADS_PER_STEP, nh)
    nq = n // tq
    assert nq % 2 == 0 and (hpb * hd) % LANES == 0 and nh % hpb == 0
    half = nq // 2
    seq_spec = pl.BlockSpec((1, hpb, n, LANES), lambda b, h, e: (b, h, 0, 0))
    out_spec = pl.BlockSpec((1, 1, tq, hpb * hd), lambda b, h, e: (b, e, 0, h))
    out_shape = jax.ShapeDtypeStruct((bsz, half, tq, nh * hd), F32)
    return pl.pallas_call(
        functools.partial(_attn_prompt_kernel, hd=hd, nfull=half - 1),
        grid=(bsz, nh // hpb, half),
        in_specs=[pl.BlockSpec((1, hpb, tq, LANES), lambda b, h, e: (b, h, 2 * e, 0)),
                  pl.BlockSpec((1, hpb, tq, LANES), lambda b, h, e: (b, h, nq - 1 - 2 * e, 0)),
                  seq_spec, seq_spec],
        out_specs=[out_spec, out_spec],
        out_shape=[out_shape, out_shape],
        scratch_shapes=[pltpu.VMEM((2, hpb, tq, LANES), BF16), pltpu.VMEM((2, hpb, tq, LANES), F32),
                        pltpu.VMEM((2, hpb, tq, LANES), F32)],
        compiler_params=_cparams("arbitrary", "arbitrary", "arbitrary"),
        name="attn_prompt",
    )(qa, qa, ka, va)


def _oproj_kernel(o_ref, x_ref, w_ref, g_ref, y_ref):
    y = _dot(o_ref[...].astype(BF16), w_ref[...])
    y_ref[...] = x_ref[...] + _rms(y, g_ref[...])


def _oproj_pair_kernel(oe_ref, oo_ref, x_ref, w_ref, g_ref, y_ref):
    even = pl.program_id(0) % 2 == 0
    o = jnp.where(even, oe_ref[0, 0], oo_ref[0, 0])
    y = _dot(o.astype(BF16), w_ref[...])
    y_ref[...] = x_ref[...] + _rms(y, g_ref[...])


def _oproj_pair(o_even, o_odd, x, w, g):
    bsz, half, tq, d = o_even.shape
    nq = 2 * half
    row = pl.BlockSpec((tq, d), lambda t: (t, 0))
    return pl.pallas_call(
        _oproj_pair_kernel,
        grid=(bsz * nq,),
        in_specs=[pl.BlockSpec((1, 1, tq, d), lambda t: (t // nq, (t % nq) // 2, 0, 0)),
                  pl.BlockSpec((1, 1, tq, d), lambda t: (t // nq, (nq - 1 - t % nq) // 2, 0, 0)),
                  row, _const_spec((d, d)), _const_spec((1, d))],
        out_specs=row,
        out_shape=jax.ShapeDtypeStruct(x.shape, F32),
        compiler_params=_cparams("arbitrary"),
        name="oproj_pair",
    )(o_even, o_odd, x, w, g)


def _oproj(o, x, w, g, tm):
    m, d = x.shape
    row = pl.BlockSpec((tm, d), lambda t: (t, 0))
    return pl.pallas_call(
        _oproj_kernel,
        grid=(m // tm,),
        in_specs=[row, row, _const_spec((d, d)), _const_spec((1, d))],
        out_specs=row,
        out_shape=jax.ShapeDtypeStruct((m, d), F32),
        compiler_params=_cparams("arbitrary"),
        name="oproj",
    )(o, x, w, g)


def _sconv_sample_kernel(x_ref, st_ref, gpre_ref, gpost_ref, win_ref, cw_ref, wout_ref,
                         y_ref, ns_ref, winb_ref, woutb_ref):
    d = x_ref.shape[-1]
    winb_ref[...] = win_ref[...].astype(BF16)
    woutb_ref[...] = wout_ref[...].astype(BF16)
    x = x_ref[...]
    h = _rms(x, gpre_ref[...]).astype(BF16)
    proj = _dot(h, winb_ref[...])
    b = proj[:, :d]
    ch = proj[:, d:2 * d] * proj[:, 2 * d:]
    s0 = st_ref[0]
    s1 = st_ref[1]
    w = cw_ref[...]
    z = s0 * w[0:1, :] + s1 * w[1:2, :] + ch * w[2:3, :]
    y = _dot((b * z).astype(BF16), woutb_ref[...])
    y_ref[...] = x + _rms(y, gpost_ref[...])
    ns_ref[0] = s1
    ns_ref[1] = ch


def _sconv_sample(x, st, gpre, gpost, w_in, conv_w, w_out):
    m, d = x.shape
    return pl.pallas_call(
        _sconv_sample_kernel,
        out_shape=[jax.ShapeDtypeStruct((m, d), F32), jax.ShapeDtypeStruct((CONV_W - 1, m, d), F32),
                   jax.ShapeDtypeStruct(w_in.shape, BF16), jax.ShapeDtypeStruct(w_out.shape, BF16)],
        compiler_params=pltpu.CompilerParams(vmem_limit_bytes=VMEM_LIMIT_BYTES),
        name="sconv_sample",
    )(x, st, gpre, gpost, w_in, conv_w, w_out)


def _ffn_sample_kernel(x_ref, st_ref, gpre_ref, gpost_ref, wg_ref, wu_ref, cw_ref, wd_ref,
                       y_ref, ns_ref, wgb_ref, wub_ref, wdb_ref, acc_ref):
    c = pl.program_id(0)

    @pl.when(c == 0)
    def _():
        acc_ref[...] = jnp.zeros_like(acc_ref)

    wgb_ref[...] = wg_ref[...].astype(BF16)
    wub_ref[...] = wu_ref[...].astype(BF16)
    wdb_ref[...] = wd_ref[...].astype(BF16)
    x = x_ref[...]
    h = _rms(x, gpre_ref[...]).astype(BF16)
    g = _dot(h, wgb_ref[...])
    u = _dot(h, wub_ref[...])
    s0 = st_ref[0]
    s1 = st_ref[1]
    w = cw_ref[...]
    gc = s0 * w[0:1, :] + s1 * w[1:2, :] + g * w[2:3, :]
    act = (gc * jax.nn.sigmoid(gc) * u).astype(BF16)
    acc_ref[...] += _dot(act, wdb_ref[...])
    ns_ref[0] = s1
    ns_ref[1] = g

    @pl.when(c == pl.num_programs(0) - 1)
    def _():
        y_ref[...] = x + _rms(acc_ref[...], gpost_ref[...])


def _ffn_sample(x, layer, st, gpre, gpost, wg, wu, conv_w, wd, fc):
    m, d = x.shape
    f = wg.shape[-1]
    full = pl.BlockSpec((m, d), lambda c: (0, 0))
    vec = pl.BlockSpec((None, 1, d), lambda c: (layer, 0, 0))
    return pl.pallas_call(
        _ffn_sample_kernel,
        grid=(f // fc,),
        in_specs=[full, pl.BlockSpec((None, CONV_W - 1, m, fc), lambda c: (layer, 0, 0, c)), vec, vec,
                  pl.BlockSpec((None, d, fc), lambda c: (layer, 0, c)),
                  pl.BlockSpec((None, d, fc), lambda c: (layer, 0, c)),
                  pl.BlockSpec((None, CONV_W, fc), lambda c: (layer, 0, c)),
                  pl.BlockSpec((None, fc, d), lambda c: (layer, c, 0))],
        out_specs=[full, pl.BlockSpec((CONV_W - 1, m, fc), lambda c: (0, 0, c)),
                   pl.BlockSpec((d, fc), lambda c: (0, c)), pl.BlockSpec((d, fc), lambda c: (0, c)),
                   pl.BlockSpec((fc, d), lambda c: (c, 0))],
        out_shape=[jax.ShapeDtypeStruct((m, d), F32), jax.ShapeDtypeStruct((CONV_W - 1, m, f), F32),
                   jax.ShapeDtypeStruct((d, f), BF16), jax.ShapeDtypeStruct((d, f), BF16),
                   jax.ShapeDtypeStruct((f, d), BF16)],
        scratch_shapes=[pltpu.VMEM((m, d), F32)],
        compiler_params=_cparams("arbitrary"),
        name="ffn_sample",
    )(x, st, gpre, gpost, wg, wu, conv_w, wd)


def _s5_sample_kernel(x_ref, h0r_ref, h0i_ref, gpre_ref, gpost_ref, win_ref, ptab_ref, wbu_ref,
                      wcr_ref, wci_ref, dsk_ref, wa_ref, wb_ref, y_ref, sr_ref, si_ref):
    nsg = wbu_ref.shape[0]
    cw = wbu_ref.shape[1]
    sw = wbu_ref.shape[2] // 2
    x = x_ref[...]
    h = _rms(x, gpre_ref[...]).astype(BF16)
    u = _dot(h, win_ref[...])
    ub = u.astype(BF16)
    ar = ptab_ref[0, 0:1, :]
    ai = ptab_ref[1, 0:1, :]
    ys = []
    for s in range(nsg):
        cs = slice(s * sw, (s + 1) * sw)
        bu = _dot(ub[:, s * cw:(s + 1) * cw], wbu_ref[s])
        hr, hi = _cmul_add(bu[:, :sw], bu[:, sw:], ar[:, cs], ai[:, cs], h0r_ref[:, cs], h0i_ref[:, cs])
        sr_ref[:, cs] = hr
        si_ref[:, cs] = hi
        ys.append(_dot(hr.astype(BF16), wcr_ref[s]) - _dot(hi.astype(BF16), wci_ref[s]))
    yy = jnp.concatenate(ys, axis=1) + dsk_ref[...] * u
    z = jax.nn.gelu(yy).astype(BF16)
    out = _dot(z, wa_ref[...]) * jax.nn.sigmoid(_dot(z, wb_ref[...]))
    y_ref[...] = x + _rms(out, gpost_ref[...])


def _s5_sample(x, h0r, h0i, gpre, gpost, w_in, ptab, w_bu, w_cr, w_ci, d_skip, wa, wb):
    m, d = x.shape
    ns = h0r.shape[1]
    return pl.pallas_call(
        _s5_sample_kernel,
        out_shape=[jax.ShapeDtypeStruct((m, d), F32),
                   jax.ShapeDtypeStruct((m, ns), F32), jax.ShapeDtypeStruct((m, ns), F32)],
        compiler_params=pltpu.CompilerParams(vmem_limit_bytes=VMEM_LIMIT_BYTES),
        name="s5_sample",
    )(x, h0r, h0i, gpre, gpost, w_in, ptab, w_bu, w_cr, w_ci, d_skip, wa, wb)


def _qkv_sample_kernel(x_ref, gpre_ref, w_ref, bf_ref, q_ref, k_ref, v_ref, lf_ref):
    d = x_ref.shape[-1]
    h = _rms(x_ref[...], gpre_ref[...]).astype(BF16)
    proj = _dot(h, w_ref[...])
    q_ref[...] = proj[:, :d]
    k_ref[...] = proj[:, d:2 * d]
    v_ref[...] = proj[:, 2 * d:3 * d]
    lf_ref[...] = _log_sigmoid(proj[:, 3 * d:] + bf_ref[...])


def _qkv_sample(x, gpre, w_all, bf_pad):
    m, d = x.shape
    row = jax.ShapeDtypeStruct((m, d), F32)
    return pl.pallas_call(
        _qkv_sample_kernel,
        out_shape=[row, row, row, jax.ShapeDtypeStruct((m, LANES), F32)],
        compiler_params=pltpu.CompilerParams(vmem_limit_bytes=VMEM_LIMIT_BYTES),
        name="qkv_sample",
    )(x, gpre, w_all, bf_pad)


def _attn_sample_kernel(pt_ref, q_ref, kn_ref, vn_ref, lfn_ref, *rest, npar):
    del pt_ref
    kc, vc, lfc = rest[:npar], rest[npar:2 * npar], rest[2 * npar:3 * npar]
    o_ref, m_ref, l_ref, acc_ref, suf_ref = rest[3 * npar:]
    st = pl.program_id(1)
    nst = pl.num_programs(1)
    nh, hd, rows = kc[0].shape[1], kc[0].shape[2], kc[0].shape[3]
    d = nh * hd
    nt = (((1,), (1,)), ((), ()))
    rid = lax.broadcasted_iota(jnp.int32, (nh, d), 0)
    lid = lax.broadcasted_iota(jnp.int32, (nh, d), 1)
    own = (lid >= rid * hd) & (lid < (rid + 1) * hd)
    qbd = jnp.where(own, q_ref[0], 0.0).astype(BF16)

    @pl.when(st == 0)
    def _():
        kb = jnp.broadcast_to(kn_ref[0], (LANES, d)).astype(BF16)
        m_ref[...] = lax.dot_general(qbd, kb, nt, preferred_element_type=F32)
        l_ref[...] = jnp.ones_like(l_ref)
        acc_ref[...] = jnp.broadcast_to(vn_ref[0], (nh, d))
        suf_ref[...] = lfn_ref[0]

    lf_all = jnp.concatenate([r[0] for r in lfc], axis=0)
    ri = lax.broadcasted_iota(jnp.int32, (rows, 2 * rows), 0)
    ci = lax.broadcasted_iota(jnp.int32, (rows, 2 * rows), 1)
    later = ((ri > ci) | (ci >= rows)).astype(F32)
    sums = jnp.dot(lf_all, later, preferred_element_type=F32, precision=lax.Precision.HIGHEST)
    carry = suf_ref[...]
    scores = []
    for g in range(npar):
        bias = sums[g * nh:(g + 1) * nh, :rows] + carry
        carry = carry + sums[g * nh:(g + 1) * nh, rows:]
        scores.append(_dot(qbd, kc[g][0].reshape(d, rows).astype(BF16)) + bias)
    suf_ref[...] = carry
    s = jnp.concatenate(scores, axis=1)
    m_prev = m_ref[...]
    m_new = jnp.maximum(m_prev, jnp.max(s, axis=1, keepdims=True))
    pr = jnp.exp(s - jnp.concatenate([m_new] * (npar * rows // LANES), axis=1))
    alpha = jnp.exp(m_prev - m_new)
    l_ref[...] = alpha * l_ref[...] + jnp.sum(pr, axis=1, keepdims=True)
    pb = pr.astype(BF16)
    pv = None
    for g in range(npar):
        part = lax.dot_general(pb[:, g * rows:(g + 1) * rows], vc[g][0].reshape(d, rows).astype(BF16), nt,
                               preferred_element_type=F32)
        pv = part if pv is None else pv + part
    acc_ref[...] = jnp.concatenate([alpha] * (d // LANES), axis=1) * acc_ref[...] + pv
    m_ref[...] = m_new

    @pl.when(st == nst - 1)
    def _():
        out = acc_ref[...] / jnp.concatenate([l_ref[...]] * (d // LANES), axis=1)
        o_ref[0] = jnp.sum(jnp.where(own, out, 0.0), axis=0, keepdims=True)


def _attn_sample(page_table, q, k_new, v_new, lf_new, cache_kt, cache_vt, cache_lft, npar):
    m, _, d = q.shape
    npg = page_table.shape[1]
    _, nh, hd, rows = cache_kt.shape
    assert rows == LANES and npg % npar == 0

    def page_idx(g):
        return lambda b, s, pt: (pt[b, npg - 1 - (s * npar + g)], 0, 0, 0)

    def lf_idx(g):
        return lambda b, s, pt: (pt[b, npg - 1 - (s * npar + g)], 0, 0)

    tok = pl.BlockSpec((1, 1, d), lambda b, s, pt: (b, 0, 0))
    grid_spec = pltpu.PrefetchScalarGridSpec(
        num_scalar_prefetch=1,
        grid=(m, npg // npar),
        in_specs=([tok, tok, tok, pl.BlockSpec((1, nh, LANES), lambda b, s, pt: (b, 0, 0))]
                  + [pl.BlockSpec((1, nh, hd, rows), page_idx(g)) for g in range(npar)]
                  + [pl.BlockSpec((1, nh, hd, rows), page_idx(g)) for g in range(npar)]
                  + [pl.BlockSpec((1, nh, rows), lf_idx(g)) for g in range(npar)]),
        out_specs=tok,
        scratch_shapes=[pltpu.VMEM((nh, LANES), F32), pltpu.VMEM((nh, LANES), F32),
                        pltpu.VMEM((nh, d), F32), pltpu.VMEM((nh, LANES), F32)],
    )
    return pl.pallas_call(
        functools.partial(_attn_sample_kernel, npar=npar),
        grid_spec=grid_spec,
        out_shape=jax.ShapeDtypeStruct((m, 1, d), F32),
        compiler_params=_cparams("arbitrary", "arbitrary"),
        name="attn_sample",
    )(page_table, q, k_new, v_new, lf_new, *([cache_kt] * npar), *([cache_vt] * npar), *([cache_lft] * npar))


TM_MIX = 256
TM_S5 = 256
TM_FFN = 512
FC_FFN = 256
FC_FFN_SAMPLE = 256
TQ_ATTN = 512
ATTN_HEADS_PER_STEP = 2
SCORES_AHEAD = 1
PAGES_PER_STEP = 16


def kernel(x_prompt, x_sample, state_sconv_l0, state_ssm_re_l1, state_ssm_im_l1, cache_k_l2, cache_v_l2, cache_logf_l2, state_sconv_l3, state_ffn_conv, page_table, sc_w_in_l0, sc_conv_w_l0, sc_w_out_l0, s5_w_in_l1, s5_lambda_re_l1, s5_lambda_im_l1, s5_log_dt_l1, s5_b_re_l1, s5_b_im_l1, s5_c_re_l1, s5_c_im_l1, s5_d_l1, s5_glu_wa_l1, s5_glu_wb_l1, fox_w_qkvf_l2, fox_b_f_l2, fox_w_o_l2, sc_w_in_l3, sc_conv_w_l3, sc_w_out_l3, norm_mix_pre, norm_mix_post, norm_ffn_pre, norm_ffn_post, ffn_w_gate, ffn_w_up, ffn_conv_w, ffn_w_down):
    bp, n, d = x_prompt.shape
    m = x_sample.shape[0]
    nh = fox_b_f_l2.shape[0]
    hd = d // nh
    f = ffn_w_gate.shape[-1]
    g, p = s5_lambda_re_l1.shape
    ns = g * p
    bf = lambda w: w.astype(BF16)
    row = lambda v: v.reshape(1, -1)

    tm_mix = min(TM_MIX, n)
    tm_s5 = min(TM_S5, n)
    tm_ffn = min(TM_FFN, n)
    tq = min(TQ_ATTN, n)
    fc = min(FC_FFN, f)
    fc_sample = max(c for c in range(LANES, min(FC_FFN_SAMPLE, f) + 1, LANES) if f % c == 0)

    ptab, coef, w_bu, w_cr, w_ci = _s5_tables(s5_lambda_re_l1, s5_lambda_im_l1, s5_log_dt_l1,
                                              s5_b_re_l1, s5_b_im_l1, s5_c_re_l1, s5_c_im_l1,
                                              tm_s5 // SUBLANES)
    scale = hd ** -0.5
    w_qkvf = jnp.concatenate([fox_w_qkvf_l2[:, :d] * scale, fox_w_qkvf_l2[:, d:],
                              jnp.zeros((d, LANES - nh), F32)], axis=1).astype(BF16)
    bf_pad = jnp.concatenate([fox_b_f_l2, jnp.zeros((LANES - nh,), F32)]).reshape(1, LANES)
    pq, pk = _fox_placement(nh)
    s5_tail = (w_bu, w_cr, w_ci, row(s5_d_l1), bf(s5_glu_wa_l1), bf(s5_glu_wb_l1))
    s5_w_in = bf(s5_w_in_l1)
    w_o = bf(fox_w_o_l2)
    ffn_g = (norm_ffn_pre[:, None, :], norm_ffn_post[:, None, :])

    xs = x_sample.reshape(m, d)
    ffn_s = []
    ffn_wb = []
    st_ffn = jnp.swapaxes(state_ffn_conv, 1, 2)

    def ffn_s_layer(xs, i):
        y, ns_, wgb, wub, wdb = _ffn_sample(xs, i, st_ffn, *ffn_g, ffn_w_gate, ffn_w_up, ffn_conv_w,
                                            ffn_w_down, fc_sample)
        ffn_s.append(jnp.swapaxes(ns_, 0, 1))
        ffn_wb.append((wgb, wub, wdb))
        return y

    xs, sc0_s, w_in0, w_out0 = _sconv_sample(xs, jnp.swapaxes(state_sconv_l0, 0, 1), row(norm_mix_pre[0]),
                                             row(norm_mix_post[0]), sc_w_in_l0, sc_conv_w_l0, sc_w_out_l0)
    xs = ffn_s_layer(xs, 0)
    xs, sr_s, si_s = _s5_sample(xs, state_ssm_re_l1.reshape(m, ns), state_ssm_im_l1.reshape(m, ns),
                                row(norm_mix_pre[1]), row(norm_mix_post[1]), s5_w_in, ptab, *s5_tail)
    xs = ffn_s_layer(xs, 1)
    q_s, k_s, v_s, lf_s = _qkv_sample(xs, row(norm_mix_pre[2]), w_qkvf, bf_pad)
    npg = page_table.shape[1]
    npar = max(c for c in range(1, PAGES_PER_STEP + 1) if npg % c == 0)
    o_s = _attn_sample(page_table, q_s.reshape(m, 1, d), k_s.reshape(m, 1, d), v_s.reshape(m, 1, d),
                       jnp.broadcast_to(lf_s[:, :nh, None], (m, nh, LANES)),
                       jnp.transpose(cache_k_l2, (0, 2, 3, 1)), jnp.transpose(cache_v_l2, (0, 2, 3, 1)),
                       jnp.transpose(cache_logf_l2, (0, 2, 1)), npar)
    xs = _oproj(o_s.reshape(m, d), xs, w_o, row(norm_mix_post[2]), m)
    xs = ffn_s_layer(xs, 2)
    xs, sc3_s, w_in3, w_out3 = _sconv_sample(xs, jnp.swapaxes(state_sconv_l3, 0, 1), row(norm_mix_pre[3]),
                                             row(norm_mix_post[3]), sc_w_in_l3, sc_conv_w_l3, sc_w_out_l3)
    xs = ffn_s_layer(xs, 3)

    def ffn_p_layer(xp, i):
        wgb, wub, wdb = ffn_wb[i]
        y, fb = _ffn_prompt(xp, i, *ffn_g, wgb, wub, ffn_conv_w, wdb, tm_ffn, fc)
        ffn_p.append(fb)
        return y

    xp = x_prompt
    ffn_p = []
    xp, sc0_p = _sconv_prompt(xp, row(norm_mix_pre[0]), row(norm_mix_post[0]), w_in0, sc_conv_w_l0, w_out0, tm_ffn)
    xp = ffn_p_layer(xp, 0)
    zeros_state = jnp.zeros((bp, 1, ns), F32)
    xp, sr_p, si_p = _s5_prompt(xp, row(norm_mix_pre[1]), row(norm_mix_post[1]), s5_w_in, ptab, coef, *s5_tail,
                                zeros_state, zeros_state, tm_s5)
    xp = ffn_p_layer(xp, 1)
    qa, ka, va, k_p, v_p, lf_p = _qkv_prompt(xp, row(norm_mix_pre[2]), w_qkvf, bf_pad, pq, pk, nh, tm_mix)
    o_even, o_odd = _attn_prompt(qa, ka, va, hd, tq)
    xp = _oproj_pair(o_even, o_odd, xp.reshape(bp * n, d), w_o, row(norm_mix_post[2])).reshape(bp, n, d)
    xp = ffn_p_layer(xp, 2)
    xp, sc3_p = _sconv_prompt(xp, row(norm_mix_pre[3]), row(norm_mix_post[3]), w_in3, sc_conv_w_l3, w_out3, tm_ffn)
    xp = ffn_p_layer(xp, 3)

    return (xp, xs.reshape(m, 1, d),
            sc0_p, jnp.swapaxes(sc0_s, 0, 1),
            sr_p.reshape(bp, g, p), sr_s.reshape(m, g, p), si_p.reshape(bp, g, p), si_s.reshape(m, g, p),
            jnp.transpose(k_p.reshape(bp, nh, hd, n), (0, 3, 1, 2)), k_s.reshape(m, 1, nh, hd),
            jnp.transpose(v_p.reshape(bp, nh, hd, n), (0, 3, 1, 2)), v_s.reshape(m, 1, nh, hd),
            jnp.transpose(lf_p, (0, 2, 1)), lf_s[:, :nh].reshape(m, 1, nh),
            sc3_p, jnp.swapaxes(sc3_s, 0, 1),
            jnp.stack(ffn_p), jnp.stack(ffn_s))
```

```python
import functools
import math

import jax
import jax.numpy as jnp
from jax import lax
from jax.experimental import pallas as pl
from jax.experimental.pallas import tpu as pltpu

F32 = jnp.float32
BF16 = jnp.bfloat16

RMS_EPS = 1e-6
CONV_W = 3
S5_GROUP = 16

LANES = 128
SUBLANES = 8
MXU_DIM = 256
VMEM_LIMIT_BYTES = 56 * 1024 * 1024

NEG_BIG = -1e30
LOG2E = math.log2(math.e)
SCAN_UNROLL = True


def _cparams(*sem):
    return pltpu.CompilerParams(dimension_semantics=sem, vmem_limit_bytes=VMEM_LIMIT_BYTES)


def _const_spec(shape):
    nd = len(shape)
    return pl.BlockSpec(shape, lambda *_: (0,) * nd, pipeline_mode=pl.Buffered(1))


def _rms(x, g):
    ms = jnp.mean(x * x, axis=-1, keepdims=True)
    return x * lax.rsqrt(ms + RMS_EPS) * g


def _dot(a, b):
    return jnp.dot(a, b, preferred_element_type=F32)


def _shift_rows(cur, prev_tail, k):
    rolled = pltpu.roll(cur, k, axis=0)
    rid = lax.broadcasted_iota(jnp.int32, (SUBLANES, cur.shape[1]), 0)
    head = jnp.where(rid < k, pltpu.roll(prev_tail, k, axis=0), rolled[:SUBLANES, :])
    return jnp.concatenate([head, rolled[SUBLANES:, :]], axis=0)


def _causal_conv3(cur, prev_tail, w):
    x1 = _shift_rows(cur, prev_tail, 1)
    x2 = _shift_rows(cur, prev_tail, 2)
    return x2 * w[0:1, :] + x1 * w[1:2, :] + cur * w[2:3, :]


def _sconv_prompt_kernel(x_ref, gpre_ref, gpost_ref, win_ref, cw_ref, wout_ref,
                         y_ref, st_ref, tail_ref):
    t = pl.program_id(1)
    nt = pl.num_programs(1)
    d = x_ref.shape[-1]
    tm = x_ref.shape[1]

    @pl.when(t == 0)
    def _():
        tail_ref[...] = jnp.zeros_like(tail_ref)

    x = x_ref[0]
    h = _rms(x, gpre_ref[...]).astype(BF16)
    proj = _dot(h, win_ref[...])
    b = proj[:, :d]
    ch = proj[:, d:2 * d] * proj[:, 2 * d:]
    z = _causal_conv3(ch, tail_ref[...], cw_ref[...])
    y = _dot((b * z).astype(BF16), wout_ref[...])
    y_ref[0] = x + _rms(y, gpost_ref[...])
    tail_ref[...] = ch[tm - SUBLANES:, :]

    @pl.when(t == nt - 1)
    def _():
        st_ref[0] = ch[tm - (CONV_W - 1):, :]


def _sconv_prompt(x, gpre, gpost, w_in, conv_w, w_out, tm):
    bsz, n, d = x.shape
    return pl.pallas_call(
        _sconv_prompt_kernel,
        grid=(bsz, n // tm),
        in_specs=[
            pl.BlockSpec((1, tm, d), lambda b, t: (b, t, 0)),
            _const_spec((1, d)), _const_spec((1, d)),
            _const_spec((d, 3 * d)), _const_spec((CONV_W, d)), _const_spec((d, d)),
        ],
        out_specs=[
            pl.BlockSpec((1, tm, d), lambda b, t: (b, t, 0)),
            pl.BlockSpec((1, CONV_W - 1, d), lambda b, t: (b, 0, 0)),
        ],
        out_shape=[jax.ShapeDtypeStruct((bsz, n, d), F32),
                   jax.ShapeDtypeStruct((bsz, CONV_W - 1, d), F32)],
        scratch_shapes=[pltpu.VMEM((SUBLANES, d), F32)],
        compiler_params=_cparams("arbitrary", "arbitrary"),
        name="sconv_prompt",
    )(x, gpre, gpost, w_in, conv_w, w_out)


def _ffn_prompt_kernel(x_ref, gpre_ref, gpost_ref, wg_ref, wu_ref, cw_ref, wd_ref,
                       y_ref, st_ref, tail_ref, act_ref, *, fc):
    t = pl.program_id(1)
    nt = pl.num_programs(1)
    tm = x_ref.shape[1]
    f = wg_ref.shape[1]

    @pl.when(t == 0)
    def _():
        tail_ref[...] = jnp.zeros_like(tail_ref)

    x = x_ref[0]
    h = _rms(x, gpre_ref[...]).astype(BF16)
    for c in range(f // fc):
        cs = slice(c * fc, (c + 1) * fc)
        g = _dot(h, wg_ref[:, cs])
        u = _dot(h, wu_ref[:, cs])
        gc = _causal_conv3(g, tail_ref[:, cs], cw_ref[:, cs])
        act_ref[:, cs] = (gc * jax.nn.sigmoid(gc) * u).astype(BF16)
        tail_ref[:, cs] = g[tm - SUBLANES:, :]
    y = _dot(act_ref[...], wd_ref[...])
    y_ref[0] = x + _rms(y, gpost_ref[...])

    @pl.when(t == nt - 1)
    def _():
        st_ref[0] = tail_ref[SUBLANES - (CONV_W - 1):, :]


def _layer_spec(shape, layer):
    nd = len(shape)
    return pl.BlockSpec((None,) + tuple(shape), lambda *_: (layer,) + (0,) * nd, pipeline_mode=pl.Buffered(1))


def _ffn_prompt(x, layer, gpre, gpost, wg, wu, conv_w, wd, tm, fc):
    bsz, n, d = x.shape
    f = wg.shape[-1]
    return pl.pallas_call(
        functools.partial(_ffn_prompt_kernel, fc=fc),
        grid=(bsz, n // tm),
        in_specs=[
            pl.BlockSpec((1, tm, d), lambda b, t: (b, t, 0)),
            _layer_spec((1, d), layer), _layer_spec((1, d), layer),
            _const_spec((d, f)), _const_spec((d, f)),
            _layer_spec((CONV_W, f), layer), _const_spec((f, d)),
        ],
        out_specs=[
            pl.BlockSpec((1, tm, d), lambda b, t: (b, t, 0)),
            pl.BlockSpec((1, CONV_W - 1, f), lambda b, t: (b, 0, 0)),
        ],
        out_shape=[jax.ShapeDtypeStruct((bsz, n, d), F32),
                   jax.ShapeDtypeStruct((bsz, CONV_W - 1, f), F32)],
        scratch_shapes=[pltpu.VMEM((SUBLANES, f), F32), pltpu.VMEM((tm, f), BF16)],
        compiler_params=_cparams("arbitrary", "arbitrary"),
        name="ffn_prompt",
    )(x, gpre, gpost, wg, wu, conv_w, wd)


def _s5_prep_kernel(lr_ref, li_ref, ldt_ref, bre_ref, bim_ref,
                    pwr_ref, pwi_ref, sgr_ref, sgi_ref, bbr_ref, bbi_ref):
    lr = lr_ref[...]
    li = li_ref[...]
    dt = jnp.exp(ldt_ref[...])
    mag = jnp.exp(lr * dt)
    abar_r = mag * jnp.cos(li * dt)
    abar_i = mag * jnp.sin(li * dt)

    def powers(ref_r, ref_i, br, bi):
        cr, ci = br, bi
        for k in range(ref_r.shape[0]):
            ref_r[k] = cr
            ref_i[k] = ci
            cr, ci = cr * br - ci * bi, cr * bi + ci * br

    seg = pwr_ref.shape[0]
    powers(pwr_ref, pwi_ref, abar_r, abar_i)
    powers(sgr_ref, sgi_ref, pwr_ref[seg - 1], pwi_ref[seg - 1])
    den = lr * lr + li * li
    nr = abar_r - 1.0
    kr = (nr * lr + abar_i * li) / den
    ki = (abar_i * lr - nr * li) / den
    bre = bre_ref[...]
    bim = bim_ref[...]
    bbr_ref[...] = kr[:, None, :] * bre - ki[:, None, :] * bim
    bbi_ref[...] = kr[:, None, :] * bim + ki[:, None, :] * bre


def _s5_prep(lam_re, lam_im, log_dt, b_re_t, b_im_t, seg):
    g, p = lam_re.shape
    n = b_re_t.shape[1]
    return pl.pallas_call(
        _s5_prep_kernel,
        out_shape=[jax.ShapeDtypeStruct((seg, g, p), F32), jax.ShapeDtypeStruct((seg, g, p), F32),
                   jax.ShapeDtypeStruct((SUBLANES, g, p), F32), jax.ShapeDtypeStruct((SUBLANES, g, p), F32),
                   jax.ShapeDtypeStruct((g, n, p), F32), jax.ShapeDtypeStruct((g, n, p), F32)],
        name="s5_prep",
    )(lam_re, lam_im, log_dt.reshape(g, 1), b_re_t, b_im_t)


def _s5_block_diag(w, sgroups):
    g, a, b = w.shape
    w4 = w.reshape(g // sgroups, sgroups, a, b)
    eye = jnp.eye(sgroups, dtype=w.dtype)
    return jnp.einsum('sgab,gh->sgahb', w4, eye).reshape(g // sgroups, sgroups * a, sgroups * b)


def _s5_tables(lam_re, lam_im, log_dt, b_re, b_im, c_re, c_im, seg):
    g, p = lam_re.shape
    pwr, pwi, sgr, sgi, bbr, bbi = _s5_prep(lam_re, lam_im, log_dt,
                                            jnp.swapaxes(b_re, 1, 2), jnp.swapaxes(b_im, 1, 2), seg)
    sg = MXU_DIM // S5_GROUP
    w_bu = jnp.concatenate([_s5_block_diag(bbr, sg), _s5_block_diag(bbi, sg)], axis=-1).astype(BF16)
    w_cr = _s5_block_diag(jnp.swapaxes(c_re, 1, 2), sg).astype(BF16)
    w_ci = _s5_block_diag(jnp.swapaxes(c_im, 1, 2), sg).astype(BF16)
    ptab = jnp.stack([pwr.reshape(seg, g * p)[:1], pwi.reshape(seg, g * p)[:1]])
    sgr = sgr.reshape(SUBLANES, g * p)
    sgi = sgi.reshape(SUBLANES, g * p)
    rows = jnp.arange(SUBLANES)[:, None]
    steps = []
    for s in (1, 2, 4):
        steps.append(jnp.where(rows >= s, sgr[s - 1][None, :], 0.0))
        steps.append(jnp.where(rows >= s, sgi[s - 1][None, :], 0.0))
    coef = jnp.stack(steps + [sgr, sgi])
    return ptab, coef, w_bu, w_cr, w_ci


def _cmul_add(xr, xi, ar, ai, sr, si):
    return xr + ar * sr - ai * si, xi + ar * si + ai * sr


def _s5_prompt_kernel(x_ref, gpre_ref, gpost_ref, perm_ref, win_ref, ptab_ref, coef_ref, wbu_ref, wcr_ref,
                      wci_ref, dsk_ref, wa_ref, wb_ref, h0r_ref, h0i_ref,
                      y_ref, sr_ref, si_ref, br_ref, bi_ref, cr_ref, ci_ref):
    t = pl.program_id(1)
    tm = x_ref.shape[1]
    seg = tm // SUBLANES
    nsg = wbu_ref.shape[0]
    cw = wbu_ref.shape[1]
    sw = wbu_ref.shape[2] // 2

    @pl.when(t == 0)
    def _():
        cr_ref[...] = jnp.broadcast_to(h0r_ref[0], cr_ref.shape)
        ci_ref[...] = jnp.broadcast_to(h0i_ref[0], ci_ref.shape)

    x = x_ref[0]
    h = _dot(perm_ref[0], _rms(x, gpre_ref[...]).astype(BF16)).astype(BF16)
    u = _dot(h, win_ref[...])
    ub = u.astype(BF16)
    lc = sw
    for c in range(nsg):
        bu = _dot(ub[:, c * cw:(c + 1) * cw], wbu_ref[c])
        br_ref[:, c * lc:(c + 1) * lc] = bu[:, :sw]
        bi_ref[:, c * lc:(c + 1) * lc] = bu[:, sw:]

    rid = lax.broadcasted_iota(jnp.int32, (SUBLANES, lc), 0)
    ys = []
    for c in range(nsg):
        cs = slice(c * lc, (c + 1) * lc)
        ar = ptab_ref[0, 0:1, cs]
        ai = ptab_ref[1, 0:1, cs]

        def local_step(j, hh, cs=cs, ar=ar, ai=ai):
            r0 = pl.multiple_of(j * SUBLANES, SUBLANES)
            nr, ni = _cmul_add(br_ref[pl.ds(r0, SUBLANES), cs], bi_ref[pl.ds(r0, SUBLANES), cs],
                               ar, ai, hh[0], hh[1])
            br_ref[pl.ds(r0, SUBLANES), cs] = nr
            bi_ref[pl.ds(r0, SUBLANES), cs] = ni
            return nr, ni

        zero = jnp.zeros((SUBLANES, lc), F32)
        er, ei = lax.fori_loop(0, seg, local_step, (zero, zero), unroll=SCAN_UNROLL)
        for j, sh in enumerate((1, 2, 4)):
            er, ei = _cmul_add(er, ei, coef_ref[2 * j, :, cs], coef_ref[2 * j + 1, :, cs],
                               pltpu.roll(er, sh, axis=0), pltpu.roll(ei, sh, axis=0))
        er, ei = _cmul_add(er, ei, coef_ref[6, :, cs], coef_ref[7, :, cs], cr_ref[:, cs], ci_ref[:, cs])
        inr = jnp.where(rid == 0, cr_ref[:, cs], pltpu.roll(er, 1, axis=0))
        ini = jnp.where(rid == 0, ci_ref[:, cs], pltpu.roll(ei, 1, axis=0))
        cr_ref[:, cs] = jnp.broadcast_to(er[SUBLANES - 1:, :], (SUBLANES, lc))
        ci_ref[:, cs] = jnp.broadcast_to(ei[SUBLANES - 1:, :], (SUBLANES, lc))

        def carry_step(j, cc, cs=cs, ar=ar, ai=ai):
            r0 = pl.multiple_of(j * SUBLANES, SUBLANES)
            nr = ar * cc[0] - ai * cc[1]
            ni = ar * cc[1] + ai * cc[0]
            br_ref[pl.ds(r0, SUBLANES), cs] = br_ref[pl.ds(r0, SUBLANES), cs] + nr
            bi_ref[pl.ds(r0, SUBLANES), cs] = bi_ref[pl.ds(r0, SUBLANES), cs] + ni
            return nr, ni

        lax.fori_loop(0, seg, carry_step, (inr, ini), unroll=SCAN_UNROLL)
        ys.append(_dot(br_ref[:, cs].astype(BF16), wcr_ref[c]) - _dot(bi_ref[:, cs].astype(BF16), wci_ref[c]))

    yy = jnp.concatenate(ys, axis=1) + dsk_ref[...] * u
    z = _dot(perm_ref[1], jax.nn.gelu(yy).astype(BF16)).astype(BF16)
    out = _dot(z, wa_ref[...]) * jax.nn.sigmoid(_dot(z, wb_ref[...]))
    y_ref[0] = x + _rms(out, gpost_ref[...])
    sr_ref[0] = cr_ref[0:1, :]
    si_ref[0] = ci_ref[0:1, :]


def _s5_prompt(x, gpre, gpost, w_in, ptab, coef, w_bu, w_cr, w_ci, d_skip, wa, wb, h0r, h0i, tm):
    bsz, n, d = x.shape
    ns = coef.shape[-1]
    seg = tm // SUBLANES
    src = (jnp.arange(tm) % SUBLANES) * seg + jnp.arange(tm) // SUBLANES
    gather = (src[:, None] == jnp.arange(tm)[None, :])
    perm = jnp.stack([gather, gather.T]).astype(BF16)
    return pl.pallas_call(
        _s5_prompt_kernel,
        grid=(bsz, n // tm),
        in_specs=[
            pl.BlockSpec((1, tm, d), lambda b, t: (b, t, 0)),
            _const_spec((1, d)), _const_spec((1, d)), _const_spec(perm.shape), _const_spec((d, d)),
            _const_spec(ptab.shape), _const_spec(coef.shape),
            _const_spec(w_bu.shape), _const_spec(w_cr.shape), _const_spec(w_ci.shape),
            _const_spec((1, d)), _const_spec((d, d)), _const_spec((d, d)),
            pl.BlockSpec((1, 1, ns), lambda b, t: (b, 0, 0)),
            pl.BlockSpec((1, 1, ns), lambda b, t: (b, 0, 0)),
        ],
        out_specs=[
            pl.BlockSpec((1, tm, d), lambda b, t: (b, t, 0)),
            pl.BlockSpec((1, 1, ns), lambda b, t: (b, 0, 0)),
            pl.BlockSpec((1, 1, ns), lambda b, t: (b, 0, 0)),
        ],
        out_shape=[jax.ShapeDtypeStruct((bsz, n, d), F32),
                   jax.ShapeDtypeStruct((bsz, 1, ns), F32), jax.ShapeDtypeStruct((bsz, 1, ns), F32)],
        scratch_shapes=[pltpu.VMEM((tm, ns), F32), pltpu.VMEM((tm, ns), F32),
                        pltpu.VMEM((SUBLANES, ns), F32), pltpu.VMEM((SUBLANES, ns), F32)],
        compiler_params=_cparams("arbitrary", "arbitrary"),
        name="s5_prompt",
    )(x, gpre, gpost, perm, w_in, ptab, coef, w_bu, w_cr, w_ci, d_skip, wa, wb, h0r, h0i)


def _log_sigmoid(x):
    return -(jnp.maximum(-x, 0.0) + jnp.log1p(jnp.exp(-jnp.abs(x))))


def _split3(x):
    hi = x.astype(BF16).astype(F32)
    r = x - hi
    mid = r.astype(BF16).astype(F32)
    lo = r - mid
    return hi, mid, lo


def _fox_placement(nh):
    hd = LANES // 2
    pq = [[0.0] * (nh * LANES) for _ in range(LANES)]
    pk = [[0.0] * (nh * LANES) for _ in range(LANES)]
    one = 3 * nh
    for h in range(nh):
        base = h * LANES + hd
        for j in range(3):
            pq[j * nh + h][base + j] = 1.0
            pq[one][base + 3 + j] = 1.0
            pk[one][base + j] = 1.0
            pk[j * nh + h][base + 3 + j] = -1.0
    return jnp.array(pq, BF16), jnp.array(pk, BF16)


def _qkv_prompt_kernel(x_ref, gpre_ref, w_ref, bf_ref, pq_ref, pk_ref,
                       qa_ref, ka_ref, va_ref, k_ref, v_ref, lf_ref, carry_ref):
    t = pl.program_id(1)
    tm = x_ref.shape[1]
    d = x_ref.shape[-1]
    nh = qa_ref.shape[1]
    hd = d // nh

    @pl.when(t == 0)
    def _():
        carry_ref[...] = jnp.zeros_like(carry_ref)

    x = x_ref[0]
    h = _rms(x, gpre_ref[...]).astype(BF16)
    lane = lax.broadcasted_iota(jnp.int32, (tm, LANES), 1)
    logf = jnp.where(lane < nh, _log_sigmoid(_dot(h, w_ref[:, 3 * d:]) + bf_ref[...]), 0.0)
    proj = _dot(h, w_ref[:, :3 * d])
    ri = lax.broadcasted_iota(jnp.int32, (tm, tm), 0)
    ci = lax.broadcasted_iota(jnp.int32, (tm, tm), 1)
    tri = (ci <= ri).astype(F32)
    cum = jnp.dot(tri, logf, preferred_element_type=F32, precision=lax.Precision.HIGHEST) + carry_ref[0:1, :]
    carry_ref[...] = jnp.broadcast_to(cum[tm - 1:, :], carry_ref.shape)
    hi, mid, lo = _split3(cum * LOG2E)
    src = hi + pltpu.roll(mid, nh, axis=1) + pltpu.roll(lo, 2 * nh, axis=1) + (lane == 3 * nh).astype(F32)
    src = src.astype(BF16)
    aug_q = _dot(src, pq_ref[...])
    aug_k = _dot(src, pk_ref[...])
    low = lane < hd
    one_at_hd = (lane == hd).astype(F32)
    for c in range(d // LANES):
        for o, dst in enumerate((qa_ref, ka_ref, va_ref)):
            blk = proj[:, o * d + c * LANES:o * d + (c + 1) * LANES]
            if o == 0:
                blk = blk * LOG2E
            for half in range(LANES // hd):
                hh = c * (LANES // hd) + half
                v = blk if half == 0 else pltpu.roll(blk, LANES - half * hd, axis=1)
                fill = (aug_q, aug_k)[o][:, hh * LANES:(hh + 1) * LANES] if o < 2 else one_at_hd
                dst[0, hh] = jnp.where(low, v, fill).astype(BF16)
    k_ref[0] = proj[:, d:2 * d].T
    v_ref[0] = proj[:, 2 * d:3 * d].T
    lf_ref[0] = logf.T[:nh, :]


def _qkv_prompt(x, gpre, w_all, bf_pad, pq, pk, nh, tm):
    bsz, n, d = x.shape
    head_spec = pl.BlockSpec((1, nh, tm, LANES), lambda b, t: (b, 0, t, 0))
    row_spec = pl.BlockSpec((1, tm, d), lambda b, t: (b, t, 0))
    chan_spec = pl.BlockSpec((1, d, tm), lambda b, t: (b, 0, t))
    head_shape = jax.ShapeDtypeStruct((bsz, nh, n, LANES), BF16)
    return pl.pallas_call(
        _qkv_prompt_kernel,
        grid=(bsz, n // tm),
        in_specs=[row_spec, _const_spec((1, d)), _const_spec(w_all.shape), _const_spec((1, LANES)),
                  _const_spec(pq.shape), _const_spec(pk.shape)],
        out_specs=[head_spec, head_spec, head_spec, chan_spec, chan_spec,
                   pl.BlockSpec((1, nh, tm), lambda b, t: (b, 0, t))],
        out_shape=[head_shape, head_shape, head_shape,
                   jax.ShapeDtypeStruct((bsz, d, n), F32), jax.ShapeDtypeStruct((bsz, d, n), F32),
                   jax.ShapeDtypeStruct((bsz, nh, n), F32)],
        scratch_shapes=[pltpu.VMEM((SUBLANES, LANES), F32)],
        compiler_params=_cparams("arbitrary", "arbitrary"),
        name="qkv_prompt",
    )(x, gpre, w_all, bf_pad, pq, pk)


def _attn_prompt_kernel(qe_ref, qo_ref, k_ref, v_ref, oe_ref, oo_ref, q_ref, m_ref, acc_ref, *, hd, nfull):
    e = pl.program_id(2)
    hpb, tq = qe_ref.shape[1], qe_ref.shape[2]
    tk = 2 * tq
    nt = (((1,), (1,)), ((), ()))
    q_ref[0] = qe_ref[0]
    q_ref[1] = qo_ref[0]
    m_ref[...] = jnp.full_like(m_ref, NEG_BIG)
    acc_ref[...] = jnp.zeros_like(acc_ref)

    def scores(w, start, cols, mask_off):
        del mask_off
        return [lax.dot_general(q_ref[w, j], k_ref[0, j, pl.ds(start, cols), :], nt,
                                preferred_element_type=F32) for j in range(hpb)]

    def update(ss, w, start, cols, mask_off):
        for j, s in enumerate(ss):
            if mask_off is not None:
                row = lax.broadcasted_iota(jnp.int32, (tq, cols), 0)
                col = lax.broadcasted_iota(jnp.int32, (tq, cols), 1)
                s = jnp.where(col <= row + mask_off, s, NEG_BIG)
            m_prev = m_ref[w, j]
            m_new = jnp.maximum(m_prev, jnp.max(s, axis=1, keepdims=True))
            pb = jnp.concatenate([jnp.exp2(s[:, c * LANES:(c + 1) * LANES] - m_new).astype(BF16)
                                  for c in range(cols // LANES)], axis=1)
            acc_ref[w, j] = (jnp.exp2(m_prev - m_new) * acc_ref[w, j]
                             + _dot(pb, v_ref[0, j, pl.ds(start, cols), :]))
            m_ref[w, j] = m_new

    slots = []
    for s in range(nfull):
        w = (s >= e).astype(jnp.int32)
        blk = jnp.where(s < e, s, s - e)
        slots.append((w, pl.multiple_of(blk * tk, tk), tk, None))
    slots.append((0, pl.multiple_of(e * tk, tk), tq, 0))
    slots.append((1, pl.multiple_of((nfull - e) * tk, tk), tk, tq))
    ahead = [scores(*slots[i]) for i in range(min(SCORES_AHEAD, len(slots)))]
    for i, slot in enumerate(slots):
        if i + SCORES_AHEAD < len(slots):
            ahead.append(scores(*slots[i + SCORES_AHEAD]))
        update(ahead.pop(0), *slot)

    lane = lax.broadcasted_iota(jnp.int32, (tq, LANES), 1)
    hpl = LANES // hd
    for w, o_ref in enumerate((oe_ref, oo_ref)):
        for g in range(hpb // hpl):
            out = None
            for i in range(hpl):
                acc = acc_ref[w, g * hpl + i]
                oj = acc / jnp.sum(jnp.where(lane == hd, acc, 0.0), axis=1, keepdims=True)
                out = oj if i == 0 else jnp.where(lane < i * hd, out, pltpu.roll(oj, i * hd, axis=1))
            o_ref[0, 0, :, g * LANES:(g + 1) * LANES] = out


def _attn_prompt(qa, ka, va, hd, tq):
    bsz, nh, n, _ = qa.shape
    hpb = min(ATTN_HEADS_PER_STEP, nh)
    nq = n // tq
    assert nq % 2 == 0 and (hpb * hd) % LANES == 0 and nh % hpb == 0
    half = nq // 2
    seq_spec = pl.BlockSpec((1, hpb, n, LANES), lambda b, h, e: (b, h, 0, 0))
    out_spec = pl.BlockSpec((1, 1, tq, hpb * hd), lambda b, h, e: (b, e, 0, h))
    out_shape = jax.ShapeDtypeStruct((bsz, half, tq, nh * hd), F32)
    return pl.pallas_call(
        functools.partial(_attn_prompt_kernel, hd=hd, nfull=half - 1),
        grid=(bsz, nh // hpb, half),
        in_specs=[pl.BlockSpec((1, hpb, tq, LANES), lambda b, h, e: (b, h, 2 * e, 0)),
                  pl.BlockSpec((1, hpb, tq, LANES), lambda b, h, e: (b, h, nq - 1 - 2 * e, 0)),
                  seq_spec, seq_spec],
        out_specs=[out_spec, out_spec],
        out_shape=[out_shape, out_shape],
        scratch_shapes=[pltpu.VMEM((2, hpb, tq, LANES), BF16), pltpu.VMEM((2, hpb, tq, LANES), F32),
                        pltpu.VMEM((2, hpb, tq, LANES), F32)],
        compiler_params=_cparams("arbitrary", "arbitrary", "arbitrary"),
        name="attn_prompt",
    )(qa, qa, ka, va)


def _oproj_kernel(o_ref, x_ref, w_ref, g_ref, y_ref):
    y = _dot(o_ref[...].astype(BF16), w_ref[...])
    y_ref[...] = x_ref[...] + _rms(y, g_ref[...])


def _oproj_pair_kernel(oe_ref, oo_ref, x_ref, w_ref, g_ref, y_ref):
    even = pl.program_id(0) % 2 == 0
    o = jnp.where(even, oe_ref[0, 0], oo_ref[0, 0])
    y = _dot(o.astype(BF16), w_ref[...])
    y_ref[...] = x_ref[...] + _rms(y, g_ref[...])


def _oproj_pair(o_even, o_odd, x, w, g):
    bsz, half, tq, d = o_even.shape
    nq = 2 * half
    row = pl.BlockSpec((tq, d), lambda t: (t, 0))
    return pl.pallas_call(
        _oproj_pair_kernel,
        grid=(bsz * nq,),
        in_specs=[pl.BlockSpec((1, 1, tq, d), lambda t: (t // nq, (t % nq) // 2, 0, 0)),
                  pl.BlockSpec((1, 1, tq, d), lambda t: (t // nq, (nq - 1 - t % nq) // 2, 0, 0)),
                  row, _const_spec((d, d)), _const_spec((1, d))],
        out_specs=row,
        out_shape=jax.ShapeDtypeStruct(x.shape, F32),
        compiler_params=_cparams("arbitrary"),
        name="oproj_pair",
    )(o_even, o_odd, x, w, g)


def _oproj(o, x, w, g, tm):
    m, d = x.shape
    row = pl.BlockSpec((tm, d), lambda t: (t, 0))
    return pl.pallas_call(
        _oproj_kernel,
        grid=(m // tm,),
        in_specs=[row, row, _const_spec((d, d)), _const_spec((1, d))],
        out_specs=row,
        out_shape=jax.ShapeDtypeStruct((m, d), F32),
        compiler_params=_cparams("arbitrary"),
        name="oproj",
    )(o, x, w, g)


def _sconv_sample_kernel(x_ref, st_ref, gpre_ref, gpost_ref, win_ref, cw_ref, wout_ref,
                         y_ref, ns_ref, winb_ref, woutb_ref):
    d = x_ref.shape[-1]
    winb_ref[...] = win_ref[...].astype(BF16)
    woutb_ref[...] = wout_ref[...].astype(BF16)
    x = x_ref[...]
    h = _rms(x, gpre_ref[...]).astype(BF16)
    proj = _dot(h, winb_ref[...])
    b = proj[:, :d]
    ch = proj[:, d:2 * d] * proj[:, 2 * d:]
    s0 = st_ref[0]
    s1 = st_ref[1]
    w = cw_ref[...]
    z = s0 * w[0:1, :] + s1 * w[1:2, :] + ch * w[2:3, :]
    y = _dot((b * z).astype(BF16), woutb_ref[...])
    y_ref[...] = x + _rms(y, gpost_ref[...])
    ns_ref[0] = s1
    ns_ref[1] = ch


def _sconv_sample(x, st, gpre, gpost, w_in, conv_w, w_out):
    m, d = x.shape
    return pl.pallas_call(
        _sconv_sample_kernel,
        out_shape=[jax.ShapeDtypeStruct((m, d), F32), jax.ShapeDtypeStruct((CONV_W - 1, m, d), F32),
                   jax.ShapeDtypeStruct(w_in.shape, BF16), jax.ShapeDtypeStruct(w_out.shape, BF16)],
        compiler_params=pltpu.CompilerParams(vmem_limit_bytes=VMEM_LIMIT_BYTES),
        name="sconv_sample",
    )(x, st, gpre, gpost, w_in, conv_w, w_out)


def _ffn_sample_kernel(x_ref, st_ref, gpre_ref, gpost_ref, wg_ref, wu_ref, cw_ref, wd_ref,
                       y_ref, ns_ref, wgb_ref, wub_ref, wdb_ref, acc_ref):
    c = pl.program_id(0)

    @pl.when(c == 0)
    def _():
        acc_ref[...] = jnp.zeros_like(acc_ref)

    wgb_ref[...] = wg_ref[...].astype(BF16)
    wub_ref[...] = wu_ref[...].astype(BF16)
    wdb_ref[...] = wd_ref[...].astype(BF16)
    x = x_ref[...]
    h = _rms(x, gpre_ref[...]).astype(BF16)
    g = _dot(h, wgb_ref[...])
    u = _dot(h, wub_ref[...])
    s0 = st_ref[0]
    s1 = st_ref[1]
    w = cw_ref[...]
    gc = s0 * w[0:1, :] + s1 * w[1:2, :] + g * w[2:3, :]
    act = (gc * jax.nn.sigmoid(gc) * u).astype(BF16)
    acc_ref[...] += _dot(act, wdb_ref[...])
    ns_ref[0] = s1
    ns_ref[1] = g

    @pl.when(c == pl.num_programs(0) - 1)
    def _():
        y_ref[...] = x + _rms(acc_ref[...], gpost_ref[...])


def _ffn_sample(x, layer, st, gpre, gpost, wg, wu, conv_w, wd, fc):
    m, d = x.shape
    f = wg.shape[-1]
    full = pl.BlockSpec((m, d), lambda c: (0, 0))
    vec = pl.BlockSpec((None, 1, d), lambda c: (layer, 0, 0))
    return pl.pallas_call(
        _ffn_sample_kernel,
        grid=(f // fc,),
        in_specs=[full, pl.BlockSpec((None, CONV_W - 1, m, fc), lambda c: (layer, 0, 0, c)), vec, vec,
                  pl.BlockSpec((None, d, fc), lambda c: (layer, 0, c)),
                  pl.BlockSpec((None, d, fc), lambda c: (layer, 0, c)),
                  pl.BlockSpec((None, CONV_W, fc), lambda c: (layer, 0, c)),
                  pl.BlockSpec((None, fc, d), lambda c: (layer, c, 0))],
        out_specs=[full, pl.BlockSpec((CONV_W - 1, m, fc), lambda c: (0, 0, c)),
                   pl.BlockSpec((d, fc), lambda c: (0, c)), pl.BlockSpec((d, fc), lambda c: (0, c)),
                   pl.BlockSpec((fc, d), lambda c: (c, 0))],
        out_shape=[jax.ShapeDtypeStruct((m, d), F32), jax.ShapeDtypeStruct((CONV_W - 1, m, f), F32),
                   jax.ShapeDtypeStruct((d, f), BF16), jax.ShapeDtypeStruct((d, f), BF16),
                   jax.ShapeDtypeStruct((f, d), BF16)],
        scratch_shapes=[pltpu.VMEM((m, d), F32)],
        compiler_params=_cparams("arbitrary"),
        name="ffn_sample",
    )(x, st, gpre, gpost, wg, wu, conv_w, wd)


def _s5_sample_kernel(x_ref, h0r_ref, h0i_ref, gpre_ref, gpost_ref, win_ref, ptab_ref, wbu_ref,
                      wcr_ref, wci_ref, dsk_ref, wa_ref, wb_ref, y_ref, sr_ref, si_ref):
    nsg = wbu_ref.shape[0]
    cw = wbu_ref.shape[1]
    sw = wbu_ref.shape[2] // 2
    x = x_ref[...]
    h = _rms(x, gpre_ref[...]).astype(BF16)
    u = _dot(h, win_ref[...])
    ub = u.astype(BF16)
    ar = ptab_ref[0, 0:1, :]
    ai = ptab_ref[1, 0:1, :]
    ys = []
    for s in range(nsg):
        cs = slice(s * sw, (s + 1) * sw)
        bu = _dot(ub[:, s * cw:(s + 1) * cw], wbu_ref[s])
        hr, hi = _cmul_add(bu[:, :sw], bu[:, sw:], ar[:, cs], ai[:, cs], h0r_ref[:, cs], h0i_ref[:, cs])
        sr_ref[:, cs] = hr
        si_ref[:, cs] = hi
        ys.append(_dot(hr.astype(BF16), wcr_ref[s]) - _dot(hi.astype(BF16), wci_ref[s]))
    yy = jnp.concatenate(ys, axis=1) + dsk_ref[...] * u
    z = jax.nn.gelu(yy).astype(BF16)
    out = _dot(z, wa_ref[...]) * jax.nn.sigmoid(_dot(z, wb_ref[...]))
    y_ref[...] = x + _rms(out, gpost_ref[...])


def _s5_sample(x, h0r, h0i, gpre, gpost, w_in, ptab, w_bu, w_cr, w_ci, d_skip, wa, wb):
    m, d = x.shape
    ns = h0r.shape[1]
    return pl.pallas_call(
        _s5_sample_kernel,
        out_shape=[jax.ShapeDtypeStruct((m, d), F32),
                   jax.ShapeDtypeStruct((m, ns), F32), jax.ShapeDtypeStruct((m, ns), F32)],
        compiler_params=pltpu.CompilerParams(vmem_limit_bytes=VMEM_LIMIT_BYTES),
        name="s5_sample",
    )(x, h0r, h0i, gpre, gpost, w_in, ptab, w_bu, w_cr, w_ci, d_skip, wa, wb)


def _qkv_sample_kernel(x_ref, gpre_ref, w_ref, bf_ref, q_ref, k_ref, v_ref, lf_ref):
    d = x_ref.shape[-1]
    h = _rms(x_ref[...], gpre_ref[...]).astype(BF16)
    proj = _dot(h, w_ref[...])
    q_ref[...] = proj[:, :d]
    k_ref[...] = proj[:, d:2 * d]
    v_ref[...] = proj[:, 2 * d:3 * d]
    lf_ref[...] = _log_sigmoid(proj[:, 3 * d:] + bf_ref[...])


def _qkv_sample(x, gpre, w_all, bf_pad):
    m, d = x.shape
    row = jax.ShapeDtypeStruct((m, d), F32)
    return pl.pallas_call(
        _qkv_sample_kernel,
        out_shape=[row, row, row, jax.ShapeDtypeStruct((m, LANES), F32)],
        compiler_params=pltpu.CompilerParams(vmem_limit_bytes=VMEM_LIMIT_BYTES),
        name="qkv_sample",
    )(x, gpre, w_all, bf_pad)


def _attn_sample_kernel(pt_ref, q_ref, kn_ref, vn_ref, lfn_ref, *rest, npar):
    del pt_ref
    kc, vc, lfc = rest[:npar], rest[npar:2 * npar], rest[2 * npar:3 * npar]
    o_ref, m_ref, l_ref, acc_ref, suf_ref = rest[3 * npar:]
    st = pl.program_id(1)
    nst = pl.num_programs(1)
    nh, hd, rows = kc[0].shape[1], kc[0].shape[2], kc[0].shape[3]
    d = nh * hd
    nt = (((1,), (1,)), ((), ()))
    rid = lax.broadcasted_iota(jnp.int32, (nh, d), 0)
    lid = lax.broadcasted_iota(jnp.int32, (nh, d), 1)
    own = (lid >= rid * hd) & (lid < (rid + 1) * hd)
    qbd = jnp.where(own, q_ref[0], 0.0).astype(BF16)

    @pl.when(st == 0)
    def _():
        kb = jnp.broadcast_to(kn_ref[0], (LANES, d)).astype(BF16)
        m_ref[...] = lax.dot_general(qbd, kb, nt, preferred_element_type=F32)
        l_ref[...] = jnp.ones_like(l_ref)
        acc_ref[...] = jnp.broadcast_to(vn_ref[0], (nh, d))
        suf_ref[...] = lfn_ref[0]

    lf_all = jnp.concatenate([r[0] for r in lfc], axis=0)
    ri = lax.broadcasted_iota(jnp.int32, (rows, 2 * rows), 0)
    ci = lax.broadcasted_iota(jnp.int32, (rows, 2 * rows), 1)
    later = ((ri > ci) | (ci >= rows)).astype(F32)
    sums = jnp.dot(lf_all, later, preferred_element_type=F32, precision=lax.Precision.HIGHEST)
    carry = suf_ref[...]
    scores = []
    for g in range(npar):
        bias = sums[g * nh:(g + 1) * nh, :rows] + carry
        carry = carry + sums[g * nh:(g + 1) * nh, rows:]
        scores.append(_dot(qbd, kc[g][0].reshape(d, rows).astype(BF16)) + bias)
    suf_ref[...] = carry
    s = jnp.concatenate(scores, axis=1)
    m_prev = m_ref[...]
    m_new = jnp.maximum(m_prev, jnp.max(s, axis=1, keepdims=True))
    pr = jnp.exp(s - jnp.concatenate([m_new] * (npar * rows // LANES), axis=1))
    alpha = jnp.exp(m_prev - m_new)
    l_ref[...] = alpha * l_ref[...] + jnp.sum(pr, axis=1, keepdims=True)
    pb = pr.astype(BF16)
    pv = None
    for g in range(npar):
        part = lax.dot_general(pb[:, g * rows:(g + 1) * rows], vc[g][0].reshape(d, rows).astype(BF16), nt,
                               preferred_element_type=F32)
        pv = part if pv is None else pv + part
    acc_ref[...] = jnp.concatenate([alpha] * (d // LANES), axis=1) * acc_ref[...] + pv
    m_ref[...] = m_new

    @pl.when(st == nst - 1)
    def _():
        out = acc_ref[...] / jnp.concatenate([l_ref[...]] * (d // LANES), axis=1)
        o_ref[0] = jnp.sum(jnp.where(own, out, 0.0), axis=0, keepdims=True)


def _attn_sample(page_table, q, k_new, v_new, lf_new, cache_kt, cache_vt, cache_lft, npar):
    m, _, d = q.shape
    npg = page_table.shape[1]
    _, nh, hd, rows = cache_kt.shape
    assert rows == LANES and npg % npar == 0

    def page_idx(g):
        return lambda b, s, pt: (pt[b, npg - 1 - (s * npar + g)], 0, 0, 0)

    def lf_idx(g):
        return lambda b, s, pt: (pt[b, npg - 1 - (s * npar + g)], 0, 0)

    tok = pl.BlockSpec((1, 1, d), lambda b, s, pt: (b, 0, 0))
    grid_spec = pltpu.PrefetchScalarGridSpec(
        num_scalar_prefetch=1,
        grid=(m, npg // npar),
        in_specs=([tok, tok, tok, pl.BlockSpec((1, nh, LANES), lambda b, s, pt: (b, 0, 0))]
                  + [pl.BlockSpec((1, nh, hd, rows), page_idx(g)) for g in range(npar)]
                  + [pl.BlockSpec((1, nh, hd, rows), page_idx(g)) for g in range(npar)]
                  + [pl.BlockSpec((1, nh, rows), lf_idx(g)) for g in range(npar)]),
        out_specs=tok,
        scratch_shapes=[pltpu.VMEM((nh, LANES), F32), pltpu.VMEM((nh, LANES), F32),
                        pltpu.VMEM((nh, d), F32), pltpu.VMEM((nh, LANES), F32)],
    )
    return pl.pallas_call(
        functools.partial(_attn_sample_kernel, npar=npar),
        grid_spec=grid_spec,
        out_shape=jax.ShapeDtypeStruct((m, 1, d), F32),
        compiler_params=_cparams("arbitrary", "arbitrary"),
        name="attn_sample",
    )(page_table, q, k_new, v_new, lf_new, *([cache_kt] * npar), *([cache_vt] * npar), *([cache_lft] * npar))


TM_MIX = 256
TM_S5 = 256
TM_FFN = 512
FC_FFN = 256
FC_FFN_SAMPLE = 256
TQ_ATTN = 512
ATTN_HEADS_PER_STEP = 2
SCORES_AHEAD = 1
PAGES_PER_STEP = 16


def kernel(x_prompt, x_sample, state_sconv_l0, state_ssm_re_l1, state_ssm_im_l1, cache_k_l2, cache_v_l2, cache_logf_l2, state_sconv_l3, state_ffn_conv, page_table, sc_w_in_l0, sc_conv_w_l0, sc_w_out_l0, s5_w_in_l1, s5_lambda_re_l1, s5_lambda_im_l1, s5_log_dt_l1, s5_b_re_l1, s5_b_im_l1, s5_c_re_l1, s5_c_im_l1, s5_d_l1, s5_glu_wa_l1, s5_glu_wb_l1, fox_w_qkvf_l2, fox_b_f_l2, fox_w_o_l2, sc_w_in_l3, sc_conv_w_l3, sc_w_out_l3, norm_mix_pre, norm_mix_post, norm_ffn_pre, norm_ffn_post, ffn_w_gate, ffn_w_up, ffn_conv_w, ffn_w_down):
    bp, n, d = x_prompt.shape
    m = x_sample.shape[0]
    nh = fox_b_f_l2.shape[0]
    hd = d // nh
    f = ffn_w_gate.shape[-1]
    g, p = s5_lambda_re_l1.shape
    ns = g * p
    bf = lambda w: w.astype(BF16)
    row = lambda v: v.reshape(1, -1)

    tm_mix = min(TM_MIX, n)
    tm_s5 = min(TM_S5, n)
    tm_ffn = min(TM_FFN, n)
    tq = min(TQ_ATTN, n)
    fc = min(FC_FFN, f)
    fc_sample = max(c for c in range(LANES, min(FC_FFN_SAMPLE, f) + 1, LANES) if f % c == 0)

    ptab, coef, w_bu, w_cr, w_ci = _s5_tables(s5_lambda_re_l1, s5_lambda_im_l1, s5_log_dt_l1,
                                              s5_b_re_l1, s5_b_im_l1, s5_c_re_l1, s5_c_im_l1,
                                              tm_s5 // SUBLANES)
    scale = hd ** -0.5
    w_qkvf = jnp.concatenate([fox_w_qkvf_l2[:, :d] * scale, fox_w_qkvf_l2[:, d:],
                              jnp.zeros((d, LANES - nh), F32)], axis=1).astype(BF16)
    bf_pad = jnp.concatenate([fox_b_f_l2, jnp.zeros((LANES - nh,), F32)]).reshape(1, LANES)
    pq, pk = _fox_placement(nh)
    s5_tail = (w_bu, w_cr, w_ci, row(s5_d_l1), bf(s5_glu_wa_l1), bf(s5_glu_wb_l1))
    s5_w_in = bf(s5_w_in_l1)
    w_o = bf(fox_w_o_l2)
    ffn_g = (norm_ffn_pre[:, None, :], norm_ffn_post[:, None, :])

    xs = x_sample.reshape(m, d)
    ffn_s = []
    ffn_wb = []
    st_ffn = jnp.swapaxes(state_ffn_conv, 1, 2)

    def ffn_s_layer(xs, i):
        y, ns_, wgb, wub, wdb = _ffn_sample(xs, i, st_ffn, *ffn_g, ffn_w_gate, ffn_w_up, ffn_conv_w,
                                            ffn_w_down, fc_sample)
        ffn_s.append(jnp.swapaxes(ns_, 0, 1))
        ffn_wb.append((wgb, wub, wdb))
        return y

    xs, sc0_s, w_in0, w_out0 = _sconv_sample(xs, jnp.swapaxes(state_sconv_l0, 0, 1), row(norm_mix_pre[0]),
                                             row(norm_mix_post[0]), sc_w_in_l0, sc_conv_w_l0, sc_w_out_l0)
    xs = ffn_s_layer(xs, 0)
    xs, sr_s, si_s = _s5_sample(xs, state_ssm_re_l1.reshape(m, ns), state_ssm_im_l1.reshape(m, ns),
                                row(norm_mix_pre[1]), row(norm_mix_post[1]), s5_w_in, ptab, *s5_tail)
    xs = ffn_s_layer(xs, 1)
    q_s, k_s, v_s, lf_s = _qkv_sample(xs, row(norm_mix_pre[2]), w_qkvf, bf_pad)
    npg = page_table.shape[1]
    npar = max(c for c in range(1, PAGES_PER_STEP + 1) if npg % c == 0)
    o_s = _attn_sample(page_table, q_s.reshape(m, 1, d), k_s.reshape(m, 1, d), v_s.reshape(m, 1, d),
                       jnp.broadcast_to(lf_s[:, :nh, None], (m, nh, LANES)),
                       jnp.transpose(cache_k_l2, (0, 2, 3, 1)), jnp.transpose(cache_v_l2, (0, 2, 3, 1)),
                       jnp.transpose(cache_logf_l2, (0, 2, 1)), npar)
    xs = _oproj(o_s.reshape(m, d), xs, w_o, row(norm_mix_post[2]), m)
    xs = ffn_s_layer(xs, 2)
    xs, sc3_s, w_in3, w_out3 = _sconv_sample(xs, jnp.swapaxes(state_sconv_l3, 0, 1), row(norm_mix_pre[3]),
                                             row(norm_mix_post[3]), sc_w_in_l3, sc_conv_w_l3, sc_w_out_l3)
    xs = ffn_s_layer(xs, 3)

    def ffn_p_layer(xp, i):
        wgb, wub, wdb = ffn_wb[i]
        y, fb = _ffn_prompt(xp, i, *ffn_g, wgb, wub, ffn_conv_w, wdb, tm_ffn, fc)
        ffn_p.append(fb)
        return y

    xp = x_prompt
    ffn_p = []
    xp, sc0_p = _sconv_prompt(xp, row(norm_mix_pre[0]), row(norm_mix_post[0]), w_in0, sc_conv_w_l0, w_out0, tm_ffn)
    xp = ffn_p_layer(xp, 0)
    zeros_state = jnp.zeros((bp, 1, ns), F32)
    xp, sr_p, si_p = _s5_prompt(xp, row(norm_mix_pre[1]), row(norm_mix_post[1]), s5_w_in, ptab, coef, *s5_tail,
                                zeros_state, zeros_state, tm_s5)
    xp = ffn_p_layer(xp, 1)
    qa, ka, va, k_p, v_p, lf_p = _qkv_prompt(xp, row(norm_mix_pre[2]), w_qkvf, bf_pad, pq, pk, nh, tm_mix)
    o_even, o_odd = _attn_prompt(qa, ka, va, hd, tq)
    xp = _oproj_pair(o_even, o_odd, xp.reshape(bp * n, d), w_o, row(norm_mix_post[2])).reshape(bp, n, d)
    xp = ffn_p_layer(xp, 2)
    xp, sc3_p = _sconv_prompt(xp, row(norm_mix_pre[3]), row(norm_mix_post[3]), w_in3, sc_conv_w_l3, w_out3, tm_ffn)
    xp = ffn_p_layer(xp, 3)

    return (xp, xs.reshape(m, 1, d),
            sc0_p, jnp.swapaxes(sc0_s, 0, 1),
            sr_p.reshape(bp, g, p), sr_s.reshape(m, g, p), si_p.reshape(bp, g, p), si_s.reshape(m, g, p),
            jnp.transpose(k_p.reshape(bp, nh, hd, n), (0, 3, 1, 2)), k_s.reshape(m, 1, nh, hd),
            jnp.transpose(v_p.reshape(bp, nh, hd, n), (0, 3, 1, 2)), v_s.reshape(m, 1, nh, hd),
            jnp.transpose(lf_p, (0, 2, 1)), lf_s[:, :nh].reshape(m, 1, nh),
            sc3_p, jnp.swapaxes(sc3_s, 0, 1),
            jnp.stack(ffn_p), jnp.stack(ffn_s))
```
